```python
import math
import jax, jax.numpy as jnp
from jax import lax
import numpy as np

D_MODEL = 2048
BATCH = 8
SEQ = 4096
DEPTH = 4

N_EVEN = (DEPTH + 1) // 2
N_ODD = DEPTH // 2
N_V_RES = N_ODD - 1
MIX_WIDTH = D_MODEL
DN_HEAD_DIM = 128
DN_HEADS = (MIX_WIDTH // 2) // DN_HEAD_DIM
DN_WIDTH = DN_HEADS * DN_HEAD_DIM
CONV_K = 4
DN_CHUNK = 64
HG_HEAD_DIM = 128
HG_HEADS = (MIX_WIDTH - DN_WIDTH) // HG_HEAD_DIM
HG_WIDTH = HG_HEADS * HG_HEAD_DIM
HG_CHUNK = 16
DN_Q = 0
DN_K = DN_Q + DN_WIDTH
DN_V = DN_K + DN_WIDTH
DN_Z = DN_V + DN_WIDTH
DN_BETA = DN_Z + DN_WIDTH
DN_ALPHA = DN_BETA + DN_HEADS
HG_Q = DN_ALPHA + DN_HEADS
HG_F = HG_Q + HG_WIDTH
HG_I = HG_F + HG_WIDTH
HG_Z = HG_I + HG_WIDTH
EVEN_IN_COLS = HG_Z + HG_WIDTH
RW_HEAD_DIM = 64
RW_HEADS = D_MODEL // RW_HEAD_DIM
DECAY_LORA = max(32, int(round(1.8 * D_MODEL ** 0.5 / 32)) * 32)
A_LORA = max(32, int(round(1.8 * D_MODEL ** 0.5 / 32)) * 32)
V_LORA = max(32, int(round(1.3 * D_MODEL ** 0.5 / 32)) * 32)
NORM_EPS = 1e-6
GN_EPS = 64e-5
L2_EPS = 1e-6

kernel_name = "hybrid_deltanet_hgrn2_rwkv7_trunk"


def rms_norm(x, gain):
    xf = x.astype(jnp.float32)
    y = xf * lax.rsqrt(jnp.mean(xf * xf, axis=-1, keepdims=True) + NORM_EPS)
    return (y * gain.astype(jnp.float32)).astype(x.dtype)


def l2norm(x):
    return x * lax.rsqrt(jnp.sum(x * x, axis=-1, keepdims=True) + L2_EPS)


def causal_depthwise_conv(x, w):
    k_len, t_len = w.shape[0], x.shape[1]
    xp = jnp.pad(x, ((0, 0), (k_len - 1, 0), (0, 0)))
    return sum(xp[:, j:j + t_len] * w[j] for j in range(k_len))


def to_chunks(x, c):
    b, t, h, d = x.shape
    return x.reshape(b, t // c, c, h, d).transpose(0, 3, 1, 2, 4)


def gated_delta_rule_chunked(q, k, v, beta, log_alpha):
    b, t, h, dk = q.shape
    dv = v.shape[-1]
    c = DN_CHUNK
    n = t // c
    qc, kc, vc = to_chunks(q, c), to_chunks(k, c), to_chunks(v, c)
    bc = beta.reshape(b, n, c, h).transpose(0, 3, 1, 2)
    G = jnp.cumsum(log_alpha.reshape(b, n, c, h).transpose(0, 3, 1, 2), axis=-1)
    causal = jnp.tril(jnp.ones((c, c), bool))
    strict = jnp.tril(jnp.ones((c, c), bool), -1)
    decay = jnp.exp(jnp.where(causal, G[..., :, None] - G[..., None, :], -jnp.inf))
    kk = jnp.einsum('bhntd,bhnsd->bhnts', kc, kc)
    lower = jnp.where(strict, bc[..., :, None] * kk * decay, 0.0)
    gamma = jnp.exp(G)
    rhs = jnp.concatenate([vc * bc[..., None], kc * (bc * gamma)[..., None]], axis=-1)
    eye = jnp.eye(c, dtype=jnp.float32)
    sol = lax.linalg.triangular_solve(eye + lower, rhs, left_side=True, lower=True,
                                      unit_diagonal=True)
    u0, w = sol[..., :dv], sol[..., dv:]
    a_qk = jnp.einsum('bhntd,bhnsd->bhnts', qc, kc) * decay
    q_dec = qc * gamma[..., None]
    g_last = G[..., -1]
    k_dec = kc * jnp.exp(g_last[..., None] - G)[..., None]
    chunk_decay = jnp.exp(g_last)

    def step(S, inp):
        u0_n, w_n, aqk_n, qd_n, kd_n, cd_n = inp
        u = u0_n - jnp.einsum('bhck,bhkv->bhcv', w_n, S)
        o = jnp.einsum('bhck,bhkv->bhcv', qd_n, S) + jnp.einsum('bhts,bhsv->bhtv', aqk_n, u)
        S = cd_n[..., None, None] * S + jnp.einsum('bhck,bhcv->bhkv', kd_n, u)
        return S, o

    xs = tuple(jnp.moveaxis(a, 2, 0) for a in (u0, w, a_qk, q_dec, k_dec, chunk_decay))
    _, o = lax.scan(step, jnp.zeros((b, h, dk, dv), jnp.float32), xs)
    return o.transpose(1, 0, 3, 2, 4).reshape(b, t, h, dv)


def gla_chunked(q, k, v, log_f):
    b, t, h, dk = q.shape
    dv = v.shape[-1]
    c = HG_CHUNK
    causal = jnp.tril(jnp.ones((c, c), bool))[:, :, None]
    qc, kc, vc = to_chunks(q, c), to_chunks(k, c), to_chunks(v, c)
    bcum = jnp.cumsum(to_chunks(log_f, c), axis=3)

    def step(S, inp):
        q_n, k_n, v_n, b_n = inp
        diff = b_n[:, :, :, None, :] - b_n[:, :, None, :, :]
        dec = jnp.exp(jnp.where(causal, diff, -jnp.inf))
        att = jnp.einsum('bhtd,bhsd,bhtsd->bhts', q_n, k_n, dec)
        b_last = b_n[:, :, -1]
        o = (jnp.einsum('bhtk,bhkv->bhtv', q_n * jnp.exp(b_n), S)
             + jnp.einsum('bhts,bhsv->bhtv', att, v_n))
        S = (jnp.exp(b_last)[..., None] * S
             + jnp.einsum('bhsk,bhsv->bhkv', k_n * jnp.exp(b_last[:, :, None, :] - b_n), v_n))
        return S, o

    xs = tuple(jnp.moveaxis(a, 2, 0) for a in (qc, kc, vc, bcum))
    _, o = lax.scan(step, jnp.zeros((b, h, dk, dv), jnp.float32), xs)
    return o.transpose(1, 0, 3, 2, 4).reshape(b, t, h, dv)


def rwkv7_scan(r, w, k, v, a, bb):
    b, t, h, d = r.shape

    def step(S, inp):
        r_t, w_t, k_t, v_t, a_t, b_t = inp
        sa = jnp.einsum('bhvk,bhk->bhv', S, a_t)
        S = S * w_t[:, :, None, :] + sa[..., None] * b_t[:, :, None, :] + v_t[..., None] * k_t[:, :, None, :]
        return S, jnp.einsum('bhvk,bhk->bhv', S, r_t)

    xs = tuple(jnp.moveaxis(z, 1, 0) for z in (r, w, k, v, a, bb))
    _, y = lax.scan(step, jnp.zeros((b, h, d, d), jnp.float32), xs)
    return jnp.moveaxis(y, 0, 1)


def head_group_norm(y, w, b):
    mu = jnp.mean(y, axis=-1, keepdims=True)
    var = jnp.mean(jnp.square(y - mu), axis=-1, keepdims=True)
    hs = y.shape[-2:]
    return (y - mu) * lax.rsqrt(var + GN_EPS) * w.reshape(hs) + b.reshape(hs)


def even_layer(hn, w_in, conv_w, a_log, dt_bias, dn_norm, lower_bound, hg_norm, w_out):
    b, t, _ = hn.shape
    f32 = jnp.float32
    p = hn @ w_in
    qkv = jax.nn.silu(causal_depthwise_conv(p[..., DN_Q:DN_Z], conv_w)).astype(f32)
    heads = lambda z, nh, hd: z.reshape(b, t, nh, hd)
    dq = l2norm(heads(qkv[..., :DN_WIDTH], DN_HEADS, DN_HEAD_DIM)) * DN_HEAD_DIM ** -0.5
    dk = l2norm(heads(qkv[..., DN_WIDTH:2 * DN_WIDTH], DN_HEADS, DN_HEAD_DIM))
    dv = heads(qkv[..., 2 * DN_WIDTH:], DN_HEADS, DN_HEAD_DIM)
    beta = jax.nn.sigmoid(p[..., DN_BETA:DN_ALPHA].astype(f32))
    log_alpha = -jnp.exp(a_log.astype(f32)) * jax.nn.softplus(
        p[..., DN_ALPHA:HG_Q].astype(f32) + dt_bias.astype(f32))
    o_a = gated_delta_rule_chunked(dq, dk, dv, beta, log_alpha)
    z_a = heads(p[..., DN_Z:DN_BETA].astype(f32), DN_HEADS, DN_HEAD_DIM)
    o_a = rms_norm(o_a, dn_norm) * jax.nn.silu(z_a)
    hq = heads(jax.nn.silu(p[..., HG_Q:HG_F].astype(f32)), HG_HEADS, HG_HEAD_DIM) * HG_HEAD_DIM ** -0.5
    forget = lower_bound + (1.0 - lower_bound) * jax.nn.sigmoid(p[..., HG_F:HG_I].astype(f32))
    hk = heads(1.0 - forget, HG_HEADS, HG_HEAD_DIM)
    log_f = heads(jnp.log(forget), HG_HEADS, HG_HEAD_DIM)
    hv = heads(p[..., HG_I:HG_Z].astype(f32), HG_HEADS, HG_HEAD_DIM)
    o_b = gla_chunked(hq, hk, hv, log_f)
    z_b = heads(p[..., HG_Z:].astype(f32), HG_HEADS, HG_HEAD_DIM)
    o_b = rms_norm(o_b, hg_norm) * jax.nn.silu(z_b)
    o = jnp.concatenate([o_a.reshape(b, t, DN_WIDTH), o_b.reshape(b, t, HG_WIDTH)], axis=-1)
    return o.astype(hn.dtype) @ w_out


def odd_layer(hn, v_first, mu, w_rkvz, w0, w1, w2, a0, a1, a2, v_res, k_k, k_a, r_k,
              ln_w, ln_b, w_out):
    b, t, d = hn.shape
    f32 = jnp.float32
    xx = jnp.pad(hn, ((0, 0), (1, 0), (0, 0)))[:, :-1] - hn
    mix = lambda i: hn + xx * mu[i]
    xv = mix(3)
    r = (mix(0) @ w_rkvz[0]).astype(f32)
    k = (mix(2) @ w_rkvz[1]).astype(f32)
    v = (xv @ w_rkvz[2]).astype(f32)
    z = (mix(5) @ w_rkvz[3]).astype(f32)
    log_w = -jax.nn.softplus(-(w0 + jnp.tanh(mix(1) @ w1) @ w2).astype(f32)) - 0.5
    decay = jnp.exp(-jnp.exp(log_w))
    if v_first is None:
        v_first = v
    else:
        v0, v1, v2 = v_res
        v = v + (v_first - v) * jax.nn.sigmoid((v0 + (xv @ v1) @ v2).astype(f32))
    a = jax.nn.sigmoid((a0 + (mix(4) @ a1) @ a2).astype(f32))
    heads = lambda u: u.reshape(b, t, RW_HEADS, RW_HEAD_DIM)
    kk = l2norm(heads(k * k_k.astype(f32)))
    k = k * (1.0 + (a - 1.0) * k_a.astype(f32))
    rh, kh, vh = heads(r), heads(k), heads(v)
    y = rwkv7_scan(rh, heads(decay), kh, vh, -kk, kk * heads(a))
    y = head_group_norm(y, ln_w.astype(f32), ln_b.astype(f32))
    y = y + jnp.sum(rh * kh * r_k.astype(f32), axis=-1, keepdims=True) * vh
    y = y.reshape(b, t, d) * jax.nn.silu(z)
    return y.astype(hn.dtype) @ w_out, v_first


def setup_inputs(seed: int = 0) -> dict:
    key = jax.random.key(seed)
    ks = iter(jax.random.split(key, 48))
    nrm = lambda shape, s: jax.random.normal(next(ks), shape, jnp.float32) * s
    uni = lambda shape, lo, hi: jax.random.uniform(next(ks), shape, jnp.float32, lo, hi)
    D = D_MODEL
    dt = jnp.exp(uni((N_EVEN, DN_HEADS), math.log(1e-3), math.log(1e-1)))
    return {
        "x": nrm((BATCH, SEQ, D), 1.0),
        "norm_gains": 1.0 + nrm((DEPTH, D), 0.02),
        "mix_w_in": nrm((N_EVEN, D, EVEN_IN_COLS), D ** -0.5),
        "dn_conv": nrm((N_EVEN, CONV_K, 3 * DN_WIDTH), CONV_K ** -0.5),
        "dn_a_log": jnp.log(uni((N_EVEN, DN_HEADS), 1.0, 16.0)),
        "dn_dt_bias": dt + jnp.log(-jnp.expm1(-dt)),
        "dn_out_norm": 1.0 + nrm((N_EVEN, DN_HEAD_DIM), 0.02),
        "hg_lb_logits": nrm((N_EVEN, HG_WIDTH), 0.5),
        "hg_out_norm": 1.0 + nrm((N_EVEN, HG_HEAD_DIM), 0.02),
        "mix_w_out": nrm((N_EVEN, MIX_WIDTH, D), MIX_WIDTH ** -0.5),
        "rw_mu": uni((N_ODD, 6, D), 0.0, 1.0),
        "rw_w_rkvz": nrm((N_ODD, 4, D, D), D ** -0.5),
        "rw_w0": uni((N_ODD, D), -6.0, -1.0),
        "rw_w1": nrm((N_ODD, D, DECAY_LORA), D ** -0.5),
        "rw_w2": nrm((N_ODD, DECAY_LORA, D), 0.1 * DECAY_LORA ** -0.5),
        "rw_a0": nrm((N_ODD, D), 0.1),
        "rw_a1": nrm((N_ODD, D, A_LORA), D ** -0.5),
        "rw_a2": nrm((N_ODD, A_LORA, D), 0.1 * A_LORA ** -0.5),
        "rw_v0": 1.0 + nrm((N_V_RES, D), 0.1),
        "rw_v1": nrm((N_V_RES, D, V_LORA), D ** -0.5),
        "rw_v2": nrm((N_V_RES, V_LORA, D), 0.1 * V_LORA ** -0.5),
        "rw_k_k": 0.85 + nrm((N_ODD, D), 0.02),
        "rw_k_a": 1.0 + nrm((N_ODD, D), 0.02),
        "rw_r_k": nrm((N_ODD, RW_HEADS, RW_HEAD_DIM), 0.1),
        "rw_ln_w": 1.0 + nrm((N_ODD, D), 0.02),
        "rw_ln_b": nrm((N_ODD, D), 0.02),
        "rw_w_out": nrm((N_ODD, D, D), D ** -0.5),
        "final_norm": 1.0 + nrm((D,), 0.02),
    }


def reference(x, norm_gains, mix_w_in, dn_conv, dn_a_log, dn_dt_bias, dn_out_norm,
              hg_lb_logits, hg_out_norm, mix_w_out, rw_mu, rw_w_rkvz, rw_w0, rw_w1, rw_w2,
              rw_a0, rw_a1, rw_a2, rw_v0, rw_v1, rw_v2, rw_k_k, rw_k_a, rw_r_k,
              rw_ln_w, rw_ln_b, rw_w_out, final_norm):
    lb_p = jax.nn.softmax(hg_lb_logits.astype(jnp.float32), axis=0)
    lower_bounds = jnp.cumsum(lb_p, axis=0) - lb_p[0]
    h = x
    v_first = None
    for layer in range(DEPTH):
        hn = rms_norm(h, norm_gains[layer])
        if layer % 2 == 0:
            e = layer // 2
            h = h + even_layer(hn, mix_w_in[e], dn_conv[e], dn_a_log[e], dn_dt_bias[e],
                               dn_out_norm[e], lower_bounds[e], hg_out_norm[e], mix_w_out[e])
        else:
            o = layer // 2
            v_res = None if o == 0 else (rw_v0[o - 1], rw_v1[o - 1], rw_v2[o - 1])
            out, v_first = odd_layer(hn, v_first, rw_mu[o], rw_w_rkvz[o], rw_w0[o], rw_w1[o],
                                     rw_w2[o], rw_a0[o], rw_a1[o], rw_a2[o], v_res, rw_k_k[o],
                                     rw_k_a[o], rw_r_k[o], rw_ln_w[o], rw_ln_b[o], rw_w_out[o])
            h = h + out
    return rms_norm(h, final_norm)
```

```python
import functools
import math

import jax
import jax.numpy as jnp
from jax import lax
from jax.experimental import pallas as pl
from jax.experimental.pallas import tpu as pltpu

F32 = jnp.float32
BF16 = jnp.bfloat16
HI = lax.Precision.HIGHEST

NORM_EPS = 1e-6
GN_EPS = 64e-5
L2_EPS = 1e-6
CONV_TAPS = 4

LANES = 128
SUBLANES = 8
V7X_VMEM_LIMIT_BYTES = 56 * 1024 * 1024

DN_CHUNK = 64
HG_CHUNK = 16
RW_CHUNK = 64
RW_HEAD = 64
EXP_NEG_HALF = math.exp(-0.5)


def _params(*semantics):
    return pltpu.CompilerParams(dimension_semantics=semantics, vmem_limit_bytes=V7X_VMEM_LIMIT_BYTES)


def _tile(dim, pref):
    t = min(dim, pref)
    assert dim % t == 0, (dim, pref)
    return t


def _dot(a, b, precision=None):
    return jnp.dot(a, b, preferred_element_type=F32, precision=precision)


def _dot_nt(a, b, precision=None):
    return lax.dot_general(a, b, (((1,), (1,)), ((), ())), preferred_element_type=F32, precision=precision)


def _dot_tn(a, b, precision=None):
    return lax.dot_general(a, b, (((0,), (0,)), ((), ())), preferred_element_type=F32, precision=precision)


def _bdot(a, b):
    return _dot(a.astype(BF16), b.astype(BF16))


def _bdot_nt(a, b):
    return _dot_nt(a.astype(BF16), b.astype(BF16))


def _bdot_tn(a, b):
    return _dot_tn(a.astype(BF16), b.astype(BF16))


def _split3(x):
    x1 = x.astype(BF16)
    r1 = x - x1.astype(F32)
    x2 = r1.astype(BF16)
    x3 = (r1 - x2.astype(F32)).astype(BF16)
    return x1, x2, x3


def _dot01(m01, x):
    x1, x2, x3 = _split3(x)
    return _dot(m01, x1) + _dot(m01, x2) + _dot(m01, x3)


def _dot01_r(x, m01):
    x1, x2, x3 = _split3(x)
    return _dot(x1, m01) + _dot(x2, m01) + _dot(x3, m01)


def _silu(x):
    return x * jax.nn.sigmoid(x)


def _softplus(x):
    return jnp.maximum(x, 0.0) + jnp.log1p(jnp.exp(-jnp.abs(x)))


def _iota2(shape, dim):
    return lax.broadcasted_iota(jnp.int32, shape, dim)


def _shift_rows(x, halo, j):
    full = pltpu.roll(x, j, axis=0)
    hfix = pltpu.roll(halo, j, axis=0)
    top = jnp.where(_iota2(hfix.shape, 0) < j, hfix, full[0:SUBLANES])
    return jnp.concatenate([top, full[SUBLANES:]], axis=0)


def _neumann_inverse(n):
    size = n.shape[0]
    eye = (_iota2((size, size), 0) == _iota2((size, size), 1)).astype(F32)
    p = eye + n
    for _ in range(5):
        n = _dot(n, n, HI)
        p = p + _dot(p, n, HI)
    return p


def _rms_kernel(x_ref, g_ref, o_ref):
    x = x_ref[...]
    y = x * lax.rsqrt(jnp.mean(x * x, axis=-1, keepdims=True) + NORM_EPS)
    o_ref[...] = (y * g_ref[...]).astype(o_ref.dtype)


def _rmsnorm(x, gain, out_dtype):
    m, d = x.shape
    tm = _tile(m, 512)
    return pl.pallas_call(
        _rms_kernel,
        grid=(m // tm,),
        in_specs=[pl.BlockSpec((tm, d), lambda i: (i, 0)), pl.BlockSpec((1, d), lambda i: (0, 0))],
        out_specs=pl.BlockSpec((tm, d), lambda i: (i, 0)),
        out_shape=jax.ShapeDtypeStruct((m, d), out_dtype),
        compiler_params=_params("parallel"),
        name="rmsnorm",
    )(x, gain.reshape(1, d))


def _rms_mix_kernel(x_ref, halo_ref, g_ref, mu_ref, o_ref):
    t = pl.program_id(1)
    g = g_ref[...]

    def norm(x):
        return x * lax.rsqrt(jnp.mean(x * x, axis=-1, keepdims=True) + NORM_EPS) * g

    hn = norm(x_ref[0])
    hp = jnp.where(t > 0, norm(halo_ref[0]), 0.0)
    xx = _shift_rows(hn, hp, 1) - hn
    for i in range(o_ref.shape[0]):
        o_ref[i, 0] = (hn + xx * mu_ref[i:i + 1, :]).astype(o_ref.dtype)


def _rms_mix(h, gain, mu):
    b, t, d = h.shape
    tt = _tile(t, 256)
    nmix = mu.shape[0]
    return pl.pallas_call(
        _rms_mix_kernel,
        grid=(b, t // tt),
        in_specs=[
            pl.BlockSpec((1, tt, d), lambda i, j: (i, j, 0)),
            pl.BlockSpec((1, SUBLANES, d), lambda i, j: (i, jnp.maximum(j * (tt // SUBLANES) - 1, 0), 0)),
            pl.BlockSpec((1, d), lambda i, j: (0, 0)),
            pl.BlockSpec((nmix, d), lambda i, j: (0, 0)),
        ],
        out_specs=pl.BlockSpec((nmix, 1, tt, d), lambda i, j: (0, i, j, 0)),
        out_shape=jax.ShapeDtypeStruct((nmix, b, t, d), BF16),
        compiler_params=_params("parallel", "parallel"),
        name="rms_mix",
    )(h, h, gain.reshape(1, d), mu)


def _mm_kernel(x_ref, w_ref, o_ref):
    o_ref[0] = _dot(x_ref[0], w_ref[0]).astype(o_ref.dtype)


def _mm_res_kernel(x_ref, w_ref, r_ref, o_ref):
    o_ref[0] = (r_ref[...] + _dot(x_ref[0], w_ref[0])).astype(o_ref.dtype)


def _matmul(x, w, out_dtype=F32, residual=None, groups=None):
    g = w.shape[0] if groups is None else groups
    _, m, k = x.shape
    n = w.shape[2]
    tm = _tile(m, 1024)
    tn = _tile(n, 1024)
    in_specs = [pl.BlockSpec((1, tm, k), lambda a, i, j: (a, i, 0)),
                pl.BlockSpec((1, k, tn), lambda a, i, j: (a, 0, j))]
    args = [x, w]
    body = _mm_kernel
    if residual is not None:
        assert g == 1
        in_specs.append(pl.BlockSpec((tm, tn), lambda a, i, j: (i, j)))
        args.append(residual)
        body = _mm_res_kernel
    return pl.pallas_call(
        body,
        grid=(g, m // tm, n // tn),
        in_specs=in_specs,
        out_specs=pl.BlockSpec((1, tm, tn), lambda a, i, j: (a, i, j)),
        out_shape=jax.ShapeDtypeStruct((g, m, n), out_dtype),
        compiler_params=_params("parallel", "parallel", "parallel"),
        name="matmul",
    )(*args)


def _lora_kernel(x_ref, a_ref, b_ref, o_ref, *, use_tanh):
    mid = _dot(x_ref[0], a_ref[...])
    if use_tanh:
        mid = jnp.tanh(mid)
    o_ref[...] = _dot(mid.astype(BF16), b_ref[...])


def _lora(xs, sel, a, b, use_tanh):
    _, m, k = xs.shape
    rank = a.shape[1]
    n = b.shape[1]
    rpad = -(-rank // LANES) * LANES
    a_p = jnp.pad(a, ((0, 0), (0, rpad - rank))).astype(BF16)
    b_p = jnp.pad(b, ((0, rpad - rank), (0, 0))).astype(BF16)
    tm = _tile(m, 512)
    return pl.pallas_call(
        functools.partial(_lora_kernel, use_tanh=use_tanh),
        grid=(m // tm,),
        in_specs=[pl.BlockSpec((1, tm, k), lambda i: (sel, i, 0)),
                  pl.BlockSpec((k, rpad), lambda i: (0, 0)),
                  pl.BlockSpec((rpad, n), lambda i: (0, 0))],
        out_specs=pl.BlockSpec((tm, n), lambda i: (i, 0)),
        out_shape=jax.ShapeDtypeStruct((m, n), F32),
        compiler_params=_params("parallel"),
        name="lora",
    )(xs, a_p, b_p)


def _dn_kernel(q_ref, k_ref, v_ref, qh_ref, kh_ref, vh_ref, z_ref, ba_ref, sel_ref, hp_ref, cw_ref, on_ref,
               o_ref, s_ref):
    t = pl.program_id(2)
    tc = q_ref.shape[1]
    c = DN_CHUNK

    @pl.when(t == 0)
    def _():
        s_ref[...] = jnp.zeros_like(s_ref)

    def conv_silu(x_ref, h_ref, w):
        x = x_ref[0]
        halo = jnp.where(t > 0, h_ref[0], 0.0)
        y = x * w[CONV_TAPS - 1:CONV_TAPS, :]
        for j in range(1, CONV_TAPS):
            y = y + _shift_rows(x, halo, j) * w[CONV_TAPS - 1 - j:CONV_TAPS - j, :]
        return _silu(y)

    def l2n(x):
        return x * lax.rsqrt(jnp.sum(x * x, axis=-1, keepdims=True) + L2_EPS)

    cw = cw_ref[0]
    q = l2n(conv_silu(q_ref, qh_ref, cw[0:CONV_TAPS])) * (LANES ** -0.5)
    k = l2n(conv_silu(k_ref, kh_ref, cw[CONV_TAPS:2 * CONV_TAPS]))
    v = conv_silu(v_ref, vh_ref, cw[2 * CONV_TAPS:3 * CONV_TAPS])

    ba = ba_ref[0]
    sel = sel_ref[0]
    hp = hp_ref[0]
    beta = jax.nn.sigmoid(jnp.sum(ba * sel[0:1], axis=-1, keepdims=True))
    alpha_pre = jnp.sum(ba * sel[1:2], axis=-1, keepdims=True)
    beta = jnp.broadcast_to(beta, (tc, LANES))
    la = -jnp.exp(hp[0:1]) * _softplus(jnp.broadcast_to(alpha_pre, (tc, LANES)) + hp[1:2])

    row = _iota2((c, c), 0)
    col = _iota2((c, c), 1)
    causal = row >= col
    strict = row > col
    tril = causal.astype(BF16)
    after = strict.astype(F32)
    onorm = on_ref[...]

    for ci in range(tc // c):
        rs = slice(ci * c, (ci + 1) * c)
        qc, kc, vc = q[rs], k[rs], v[rs]
        bc, lac = beta[rs], la[rs]
        g = _dot01(tril, lac)
        dmat = _dot01(tril, lac[:, 0:c] * after)
        decay = jnp.where(causal, jnp.exp(jnp.where(causal, dmat, 0.0)), 0.0)
        kb = kc.astype(BF16)
        kk = _dot_nt(kb, kb)
        qk = _dot_nt(qc.astype(BF16), kb)
        lower = jnp.where(strict, bc[:, 0:c] * kk * decay, 0.0)
        tinv = _neumann_inverse(-lower)
        gamma = jnp.exp(g)
        rhs = jnp.concatenate([vc * bc, kc * (bc * gamma)], axis=1)
        sol = _dot(tinv, rhs, HI)
        u0, w = sol[:, 0:LANES], sol[:, LANES:]
        aqk = jnp.where(causal, qk * decay, 0.0)
        glast = g[c - 1:c, :]
        qd = qc * gamma
        kd = kc * jnp.exp(glast - g)
        s = s_ref[...]
        sb = s.astype(BF16)
        u = u0 - _dot(w.astype(BF16), sb)
        o = _dot(qd.astype(BF16), sb) + _bdot(aqk, u)
        s_ref[...] = jnp.exp(glast) * s + _bdot_tn(kd, u)
        on = o * lax.rsqrt(jnp.mean(o * o, axis=-1, keepdims=True) + NORM_EPS) * onorm
        o_ref[0, rs, :] = (on * _silu(z_ref[0, rs, :])).astype(o_ref.dtype)


def _deltanet(p, ba, conv_w, a_log, dt_bias, out_norm, nheads):
    b, t, _ = p.shape
    tc = _tile(t, 256)
    assert tc % DN_CHUNK == 0
    hb = tc // SUBLANES
    nh = nheads
    cw = conv_w.reshape(CONV_TAPS, 3, nh, LANES).transpose(2, 1, 0, 3).reshape(nh, 3 * CONV_TAPS, LANES)
    lane = jnp.arange(LANES)[None, None, :]
    head = jnp.arange(nh)[:, None, None]
    sel = jnp.concatenate([(lane == head), (lane == head + nh)], axis=1).astype(F32)
    hp = jnp.broadcast_to(jnp.stack([a_log, dt_bias], axis=1)[:, :, None], (nh, 2, LANES)).astype(F32)

    def main(sec):
        return pl.BlockSpec((1, tc, LANES), lambda i, h, j: (i, j, sec * nh + h))

    def halo(sec):
        return pl.BlockSpec((1, SUBLANES, LANES), lambda i, h, j: (i, jnp.maximum(j * hb - 1, 0), sec * nh + h))

    per_head = lambda rows: pl.BlockSpec((1, rows, LANES), lambda i, h, j: (h, 0, 0))
    return pl.pallas_call(
        _dn_kernel,
        grid=(b, nh, t // tc),
        in_specs=[main(0), main(1), main(2), halo(0), halo(1), halo(2), main(3),
                  pl.BlockSpec((1, tc, LANES), lambda i, h, j: (i, j, 0)),
                  per_head(2), per_head(2), per_head(3 * CONV_TAPS),
                  pl.BlockSpec((1, LANES), lambda i, h, j: (0, 0))],
        out_specs=pl.BlockSpec((1, tc, LANES), lambda i, h, j: (i, j, h)),
        out_shape=jax.ShapeDtypeStruct((b, t, nh * LANES), BF16),
        scratch_shapes=[pltpu.VMEM((LANES, LANES), F32)],
        compiler_params=_params("parallel", "parallel", "arbitrary"),
        name="deltanet",
    )(p, p, p, p, p, p, p, ba, sel, hp, cw, out_norm.reshape(1, LANES))


def _hg_kernel(q_ref, f_ref, i_ref, z_ref, lb_ref, on_ref, o_ref, st_ref, *, layer):
    t = pl.program_id(2)
    tc = q_ref.shape[1]
    c = HG_CHUNK

    @pl.when(t == 0)
    def _():
        st_ref[...] = jnp.zeros_like(st_ref)

    logits = lb_ref[...]
    e = jnp.exp(logits - jnp.max(logits, axis=0, keepdims=True))
    share = e / jnp.sum(e, axis=0, keepdims=True)
    lbound = jnp.sum(share[0:layer + 1], axis=0, keepdims=True) - share[0:1]

    q = _silu(q_ref[0]) * (LANES ** -0.5)
    forget = lbound + (1.0 - lbound) * jax.nn.sigmoid(f_ref[0])
    k = 1.0 - forget
    logf = jnp.log(forget)
    v = i_ref[0]

    tril = (_iota2((c, c), 0) >= _iota2((c, c), 1)).astype(BF16)
    causal3 = _iota2((c, c, LANES), 0) >= _iota2((c, c, LANES), 1)
    onorm = on_ref[...]

    for ci in range(tc // c):
        rs = slice(ci * c, (ci + 1) * c)
        qc, kc, vc = q[rs], k[rs], v[rs]
        bcum = _dot01(tril, logf[rs])
        diff = bcum[:, None, :] - bcum[None, :, :]
        dec = jnp.exp(jnp.where(causal3, diff, -jnp.inf))
        att = jnp.sum(qc[:, None, :] * kc[None, :, :] * dec, axis=-1, keepdims=True)
        o_intra = jnp.sum(att * vc[None, :, :], axis=1)
        blast = bcum[c - 1:c, :]
        st = st_ref[...]
        o = _bdot_nt(qc * jnp.exp(bcum), st) + o_intra
        st_ref[...] = st * jnp.exp(blast) + _bdot_tn(vc, kc * jnp.exp(blast - bcum))
        on = o * lax.rsqrt(jnp.mean(o * o, axis=-1, keepdims=True) + NORM_EPS) * onorm
        o_ref[0, rs, :] = (on * _silu(z_ref[0, rs, :])).astype(o_ref.dtype)


def _hgrn2(p, col0, lb_logits, out_norm, nheads, layer):
    b, t, _ = p.shape
    tc = _tile(t, 256)
    nh = nheads

    def main(sec):
        return pl.BlockSpec((1, tc, LANES), lambda i, h, j: (i, j, col0 + sec * nh + h))

    nl = lb_logits.shape[0]
    return pl.pallas_call(
        functools.partial(_hg_kernel, layer=layer),
        grid=(b, nh, t // tc),
        in_specs=[main(0), main(1), main(2), main(3),
                  pl.BlockSpec((nl, LANES), lambda i, h, j: (0, h)),
                  pl.BlockSpec((1, LANES), lambda i, h, j: (0, 0))],
        out_specs=pl.BlockSpec((1, tc, LANES), lambda i, h, j: (i, j, h)),
        out_shape=jax.ShapeDtypeStruct((b, t, nh * LANES), BF16),
        scratch_shapes=[pltpu.VMEM((LANES, LANES), F32)],
        compiler_params=_params("parallel", "parallel", "arbitrary"),
        name="hgrn2",
    )(p, p, p, p, lb_logits, out_norm.reshape(1, LANES))


def _rw_kernel(*refs, has_vres):
    if has_vres:
        r_ref, k_ref, v_ref, z_ref, wl_ref, al_ref, vl_ref, vf_ref, prm_ref, o_ref, ht_ref = refs
    else:
        r_ref, k_ref, v_ref, z_ref, wl_ref, al_ref, prm_ref, o_ref, ht_ref = refs
    t = pl.program_id(2)
    tc = o_ref.shape[1]
    c = RW_CHUNK

    @pl.when(t == 0)
    def _():
        ht_ref[...] = jnp.zeros_like(ht_ref)

    prm = prm_ref[...]
    w0, a0, k_k, k_a, r_k, ln_w, ln_b, v0 = (prm[i:i + 1] for i in range(8))

    lane1 = _iota2((1, LANES), 1)
    m0 = (lane1 < RW_HEAD).astype(F32)
    m1 = 1.0 - m0
    same_head = (_iota2((LANES, LANES), 0) < RW_HEAD) == (_iota2((LANES, LANES), 1) < RW_HEAD)
    bones = same_head.astype(BF16)
    bdmask = same_head.astype(F32)

    def gsum(x):
        return _dot01_r(x, bones)

    r = r_ref[0, 0]
    k = k_ref[0, 0]
    v = v_ref[0, 0]
    lw = -EXP_NEG_HALF * jax.nn.sigmoid(w0 + wl_ref[0])
    a_gate = jax.nn.sigmoid(a0 + al_ref[0])
    if has_vres:
        v = v + (vf_ref[0, 0] - v) * jax.nn.sigmoid(v0 + vl_ref[0])
    kkr = k * k_k
    kk = kkr * lax.rsqrt(gsum(kkr * kkr) + L2_EPS)
    k2 = k * (1.0 + (a_gate - 1.0) * k_a)
    av = -kk
    bv = kk * a_gate
    bonus = gsum(r * k2 * r_k)

    row = _iota2((c, LANES), 0)
    scol = jnp.bitwise_and(_iota2((c, LANES), 1), RW_HEAD - 1)
    incl = row >= scol
    strict = row > scol
    tril = (_iota2((c, c), 0) >= _iota2((c, c), 1)).astype(BF16)

    def bd(x):
        return jnp.concatenate([x * m0, x * m1], axis=0)

    for ci in range(tc // c):
        rs = slice(ci * c, (ci + 1) * c)
        lwc = lw[rs]
        g = _dot01(tril, lwc)
        e_g = jnp.exp(g)
        e_gn = jnp.exp(-g)
        rt = r[rs] * e_g
        at = av[rs] * jnp.exp(g - lwc)
        kt = k2[rs] * e_gn
        bt = bv[rs] * e_gn
        vc = v[rs]
        x = _dot_nt(jnp.concatenate([at, rt], axis=0), jnp.concatenate([bd(kt), bd(bt)], axis=0), HI)
        a_ak = jnp.where(strict, x[0:c, 0:LANES], 0.0)
        a_ab = jnp.where(strict, x[0:c, LANES:], 0.0)
        a_rk = jnp.where(incl, x[c:, 0:LANES], 0.0)
        a_rb = jnp.where(incl, x[c:, LANES:], 0.0)
        tinv = _neumann_inverse(bd(a_ab))
        vbd = bd(vc)
        wu = _dot(tinv, jnp.concatenate([bd(at), _dot(bd(a_ak), vbd, HI)], axis=1), HI)
        wc = wu[0:c, 0:LANES] + wu[c:, 0:LANES]
        uc = wu[0:c, LANES:] + wu[c:, LANES:]
        ht = ht_ref[...]
        sa = _dot_nt(wc, ht, HI) + uc
        y = (_dot_nt(rt, ht, HI)
             + _dot(jnp.concatenate([a_rb, a_rk], axis=1), jnp.concatenate([bd(sa), vbd], axis=0), HI))
        e_last = e_g[c - 1:c, :]
        ht_ref[...] = ht * e_last + bdmask * _dot_tn(jnp.concatenate([sa, vc], axis=0),
                                                     jnp.concatenate([bt * e_last, kt * e_last], axis=0), HI)
        mu = gsum(y) * (1.0 / RW_HEAD)
        d = y - mu
        var = gsum(d * d) * (1.0 / RW_HEAD)
        yn = d * lax.rsqrt(var + GN_EPS) * ln_w + ln_b
        out = (yn + bonus[rs] * vc) * _silu(z_ref[0, 0, rs, :])
        o_ref[0, rs, :] = out.astype(o_ref.dtype)


def _rwkv7(rkvz, wl, al, vl, v_first, prm):
    _, b, t, d = rkvz.shape
    tc = _tile(t, 256)
    has_vres = vl is not None

    def sec(s):
        return pl.BlockSpec((1, 1, tc, LANES), lambda i, p, j: (s, i, j, p))

    flat = pl.BlockSpec((1, tc, LANES), lambda i, p, j: (i, j, p))
    in_specs = [sec(0), sec(1), sec(2), sec(3), flat, flat]
    args = [rkvz, rkvz, rkvz, rkvz, wl, al]
    if has_vres:
        in_specs += [flat, sec(2)]
        args += [vl, v_first]
    in_specs.append(pl.BlockSpec((8, LANES), lambda i, p, j: (0, p)))
    args.append(prm)
    return pl.pallas_call(
        functools.partial(_rw_kernel, has_vres=has_vres),
        grid=(b, d // LANES, t // tc),
        in_specs=in_specs,
        out_specs=flat,
        out_shape=jax.ShapeDtypeStruct((b, t, d), BF16),
        scratch_shapes=[pltpu.VMEM((LANES, LANES), F32)],
        compiler_params=_params("parallel", "parallel", "arbitrary"),
        name="rwkv7",
    )(*args)


def kernel(x, norm_gains, mix_w_in, dn_conv, dn_a_log, dn_dt_bias, dn_out_norm, hg_lb_logits, hg_out_norm,
           mix_w_out, rw_mu, rw_w_rkvz, rw_w0, rw_w1, rw_w2, rw_a0, rw_a1, rw_a2, rw_v0, rw_v1, rw_v2,
           rw_k_k, rw_k_a, rw_r_k, rw_ln_w, rw_ln_b, rw_w_out, final_norm):
    b, t, d = x.shape
    m = b * t
    depth = norm_gains.shape[0]
    dn_heads = dn_a_log.shape[1]
    dn_width = dn_conv.shape[2] // 3
    hg_width = hg_lb_logits.shape[1]
    hg_heads = hg_width // LANES
    assert dn_width == dn_heads * LANES and hg_out_norm.shape[1] == LANES and rw_r_k.shape[2] == RW_HEAD
    dn_ba = 4 * dn_width
    hg_q = dn_ba + 2 * dn_heads
    assert 2 * dn_heads <= LANES

    h = x.reshape(m, d)
    v_first = None
    for layer in range(depth):
        gain = norm_gains[layer]
        if layer % 2 == 0:
            e = layer // 2
            w_in = mix_w_in[e]
            w_main = jnp.concatenate([w_in[:, :dn_ba], w_in[:, hg_q:]], axis=1).astype(BF16)
            w_ba = jnp.pad(w_in[:, dn_ba:hg_q], ((0, 0), (0, LANES - 2 * dn_heads))).astype(BF16)
            hn = _rmsnorm(h, gain, BF16)[None]
            p = _matmul(hn, w_main[None])[0].reshape(b, t, -1)
            ba = _matmul(hn, w_ba[None])[0].reshape(b, t, LANES)
            o_a = _deltanet(p, ba, dn_conv[e], dn_a_log[e], dn_dt_bias[e], dn_out_norm[e], dn_heads)
            o_b = _hgrn2(p, dn_ba // LANES, hg_lb_logits, hg_out_norm[e], hg_heads, e)
            o = jnp.concatenate([o_a, o_b], axis=-1).reshape(1, m, -1)
            h = _matmul(o, mix_w_out[e].astype(BF16)[None], residual=h)[0]
        else:
            o = layer // 2
            mu = rw_mu[o][jnp.array([0, 2, 3, 5, 1, 4])]
            mixes = _rms_mix(h.reshape(b, t, d), gain, mu).reshape(6, m, d)
            rkvz = _matmul(mixes, rw_w_rkvz[o].astype(BF16), groups=4).reshape(4, b, t, d)
            wl = _lora(mixes, 4, rw_w1[o], rw_w2[o], True).reshape(b, t, d)
            al = _lora(mixes, 5, rw_a1[o], rw_a2[o], False).reshape(b, t, d)
            vl = None
            v0 = jnp.zeros((d,), F32)
            if v_first is None:
                v_first = rkvz
            else:
                vl = _lora(mixes, 2, rw_v1[o - 1], rw_v2[o - 1], False).reshape(b, t, d)
                v0 = rw_v0[o - 1]
            prm = jnp.stack([rw_w0[o], rw_a0[o], rw_k_k[o], rw_k_a[o], rw_r_k[o].reshape(d),
                             rw_ln_w[o], rw_ln_b[o], v0]).astype(F32)
            y = _rwkv7(rkvz, wl, al, vl, v_first, prm)
            h = _matmul(y.reshape(1, m, d), rw_w_out[o].astype(BF16)[None], residual=h)[0]
    return _rmsnorm(h, final_norm, x.dtype).reshape(b, t, d)
```

```python
import functools
import math

import jax
import jax.numpy as jnp
from jax import lax
from jax.experimental import pallas as pl
from jax.experimental.pallas import tpu as pltpu

F32 = jnp.float32
BF16 = jnp.bfloat16

NORM_EPS = 1e-6
GN_EPS = 64e-5
L2_EPS = 1e-6
CONV_TAPS = 4

LANES = 128
SUBLANES = 8
V7X_VMEM_LIMIT_BYTES = 56 * 1024 * 1024

DN_CHUNK = 64
HG_CHUNK = 16
RW_CHUNK = 64
RW_HEAD = 64
EXP_NEG_HALF = math.exp(-0.5)


def _params(*semantics):
    return pltpu.CompilerParams(dimension_semantics=semantics, vmem_limit_bytes=V7X_VMEM_LIMIT_BYTES)


def _tile(dim, pref):
    t = min(dim, pref)
    assert dim % t == 0, (dim, pref)
    return t


def _dot(a, b):
    return jnp.dot(a, b, preferred_element_type=F32)


def _dot_nt(a, b):
    return lax.dot_general(a, b, (((1,), (1,)), ((), ())), preferred_element_type=F32)


def _dot_tn(a, b):
    return lax.dot_general(a, b, (((0,), (0,)), ((), ())), preferred_element_type=F32)


def _bdot_nt(a, b):
    return _dot_nt(a.astype(BF16), b.astype(BF16))


def _bdot_tn(a, b):
    return _dot_tn(a.astype(BF16), b.astype(BF16))


def _dot01(m01, x):
    x1 = x.astype(BF16)
    x2 = (x - x1.astype(F32)).astype(BF16)
    return _dot(m01, x1) + _dot(m01, x2)


def _silu(x):
    return x * jax.nn.sigmoid(x)


def _softplus(x):
    return jnp.maximum(x, 0.0) + jnp.log1p(jnp.exp(-jnp.abs(x)))


def _iota2(shape, dim):
    return lax.broadcasted_iota(jnp.int32, shape, dim)


def _shift_rows(x, halo, j):
    full = pltpu.roll(x, j, axis=0)
    hfix = pltpu.roll(halo, j, axis=0)
    top = jnp.where(_iota2(hfix.shape, 0) < j, hfix, full[0:SUBLANES])
    return jnp.concatenate([top, full[SUBLANES:]], axis=0)


def _rms_kernel(x_ref, g_ref, o_ref):
    x = x_ref[...]
    y = x * lax.rsqrt(jnp.mean(x * x, axis=-1, keepdims=True) + NORM_EPS)
    o_ref[...] = (y * g_ref[...]).astype(o_ref.dtype)


def _rmsnorm(x, gain, out_dtype):
    m, d = x.shape
    tm = _tile(m, 512)
    return pl.pallas_call(
        _rms_kernel,
        grid=(m // tm,),
        in_specs=[pl.BlockSpec((tm, d), lambda i: (i, 0)), pl.BlockSpec((1, d), lambda i: (0, 0))],
        out_specs=pl.BlockSpec((tm, d), lambda i: (i, 0)),
        out_shape=jax.ShapeDtypeStruct((m, d), out_dtype),
        compiler_params=_params("parallel"),
        name="rmsnorm",
    )(x, gain.reshape(1, d))


def _rms_mix_kernel(x_ref, halo_ref, g_ref, mu_ref, o_ref):
    t = pl.program_id(1)
    g = g_ref[...]

    def norm(x):
        return x * lax.rsqrt(jnp.mean(x * x, axis=-1, keepdims=True) + NORM_EPS) * g

    hn = norm(x_ref[0])
    hp = jnp.where(t > 0, norm(halo_ref[0]), 0.0)
    xx = _shift_rows(hn, hp, 1) - hn
    for i in range(o_ref.shape[0]):
        o_ref[i, 0] = (hn + xx * mu_ref[i:i + 1, :]).astype(o_ref.dtype)


def _rms_mix(h, gain, mu):
    b, t, d = h.shape
    tt = _tile(t, 256)
    nmix = mu.shape[0]
    return pl.pallas_call(
        _rms_mix_kernel,
        grid=(b, t // tt),
        in_specs=[
            pl.BlockSpec((1, tt, d), lambda i, j: (i, j, 0)),
            pl.BlockSpec((1, SUBLANES, d), lambda i, j: (i, jnp.maximum(j * (tt // SUBLANES) - 1, 0), 0)),
            pl.BlockSpec((1, d), lambda i, j: (0, 0)),
            pl.BlockSpec((nmix, d), lambda i, j: (0, 0)),
        ],
        out_specs=pl.BlockSpec((nmix, 1, tt, d), lambda i, j: (0, i, j, 0)),
        out_shape=jax.ShapeDtypeStruct((nmix, b, t, d), BF16),
        compiler_params=_params("parallel", "parallel"),
        name="rms_mix",
    )(h, h, gain.reshape(1, d), mu)


def _mm_kernel(x_ref, w_ref, o_ref):
    o_ref[0] = _dot(x_ref[0], w_ref[0]).astype(o_ref.dtype)


def _mm_res_kernel(x_ref, w_ref, r_ref, o_ref):
    o_ref[0] = (r_ref[...] + _dot(x_ref[0], w_ref[0])).astype(o_ref.dtype)


def _matmul(x, w, out_dtype=F32, residual=None, groups=None):
    g = w.shape[0] if groups is None else groups
    _, m, k = x.shape
    n = w.shape[2]
    tm = _tile(m, 1024)
    tn = _tile(n, 1024)
    in_specs = [pl.BlockSpec((1, tm, k), lambda a, i, j: (a, i, 0)),
                pl.BlockSpec((1, k, tn), lambda a, i, j: (a, 0, j))]
    args = [x, w]
    body = _mm_kernel
    if residual is not None:
        assert g == 1
        in_specs.append(pl.BlockSpec((tm, tn), lambda a, i, j: (i, j)))
        args.append(residual)
        body = _mm_res_kernel
    return pl.pallas_call(
        body,
        grid=(g, m // tm, n // tn),
        in_specs=in_specs,
        out_specs=pl.BlockSpec((1, tm, tn), lambda a, i, j: (a, i, j)),
        out_shape=jax.ShapeDtypeStruct((g, m, n), out_dtype),
        compiler_params=_params("parallel", "parallel", "parallel"),
        name="matmul",
    )(*args)


def _lora_kernel(x_ref, a_ref, b_ref, o_ref, *, use_tanh):
    mid = _dot(x_ref[0], a_ref[...])
    if use_tanh:
        mid = jnp.tanh(mid)
    o_ref[...] = _dot(mid.astype(BF16), b_ref[...])


def _lora(xs, sel, a, b, use_tanh):
    _, m, k = xs.shape
    rank = a.shape[1]
    n = b.shape[1]
    rpad = -(-rank // LANES) * LANES
    a_p = jnp.pad(a, ((0, 0), (0, rpad - rank))).astype(BF16)
    b_p = jnp.pad(b, ((0, rpad - rank), (0, 0))).astype(BF16)
    tm = _tile(m, 512)
    return pl.pallas_call(
        functools.partial(_lora_kernel, use_tanh=use_tanh),
        grid=(m // tm,),
        in_specs=[pl.BlockSpec((1, tm, k), lambda i: (sel, i, 0)),
                  pl.BlockSpec((k, rpad), lambda i: (0, 0)),
                  pl.BlockSpec((rpad, n), lambda i: (0, 0))],
        out_specs=pl.BlockSpec((tm, n), lambda i: (i, 0)),
        out_shape=jax.ShapeDtypeStruct((m, n), F32),
        compiler_params=_params("parallel"),
        name="lora",
    )(xs, a_p, b_p)


_DN_F128 = dict(q=0, k=1, vb=2, bc=3, g=4, u0=5, o=6)
_DN_B128 = dict(qg=0, w=1)
_DN_B64 = dict(nc=0, aqk=1)
_DN_F64 = dict(p=0, dmat=1)


def _dn_kernel(q_ref, k_ref, v_ref, qh_ref, kh_ref, vh_ref, z_ref, ba_ref, sel_ref, hp_ref, cw_ref, on_ref,
               o_ref, s_ref, f128, b128, b64, f64, rhs_s, kdt_s, cd_s):
    t = pl.program_id(2)
    tc = q_ref.shape[1]
    heads = s_ref.shape[0]
    c = DN_CHUNK
    qf, qb, q6, qp = _DN_F128, _DN_B128, _DN_B64, _DN_F64

    @pl.when(t == 0)
    def _():
        s_ref[...] = jnp.zeros_like(s_ref)

    row = _iota2((c, c), 0)
    col = _iota2((c, c), 1)
    causal = row >= col
    strict = row > col
    strict_f = strict.astype(F32)
    eye = (row == col).astype(F32)
    tril = causal.astype(BF16)
    onorm = on_ref[...]

    bodies = [(ci, hh) for ci in range(tc // c) for hh in range(heads)]

    def where(ci, hh):
        return ci * heads + hh, slice(ci * c, (ci + 1) * c), slice(hh * LANES, (hh + 1) * LANES)

    def conv_silu(x_ref, h_ref, ci, rs, ls, w):
        x = x_ref[0, rs, ls]
        if ci == 0:
            halo = jnp.where(t > 0, h_ref[0, :, ls], 0.0)
        else:
            halo = x_ref[0, ci * c - SUBLANES:ci * c, ls]
        y = x * w[CONV_TAPS - 1:CONV_TAPS, :]
        for j in range(1, CONV_TAPS):
            y = y + _shift_rows(x, halo, j) * w[CONV_TAPS - 1 - j:CONV_TAPS - j, :]
        return _silu(y)

    def l2n(x):
        return x * lax.rsqrt(jnp.sum(x * x, axis=-1, keepdims=True) + L2_EPS)

    for ci, hh in bodies:
        i, rs, ls = where(ci, hh)
        cw = cw_ref[hh]
        sel = sel_ref[hh]
        hp = hp_ref[hh]
        ba = ba_ref[0, rs, :]
        qc = l2n(conv_silu(q_ref, qh_ref, ci, rs, ls, cw[0:CONV_TAPS])) * (LANES ** -0.5)
        kc = l2n(conv_silu(k_ref, kh_ref, ci, rs, ls, cw[CONV_TAPS:2 * CONV_TAPS]))
        vc = conv_silu(v_ref, vh_ref, ci, rs, ls, cw[2 * CONV_TAPS:3 * CONV_TAPS])
        beta = jax.nn.sigmoid(jnp.sum(ba * sel[0:1], axis=-1, keepdims=True))
        alpha_pre = jnp.sum(ba * sel[1:2], axis=-1, keepdims=True)
        bc = jnp.broadcast_to(beta, (c, LANES))
        lac = -jnp.exp(hp[0:1]) * _softplus(jnp.broadcast_to(alpha_pre, (c, LANES)) + hp[1:2])
        f128[i, qf["q"]] = qc
        f128[i, qf["k"]] = kc
        f128[i, qf["vb"]] = vc * bc
        f128[i, qf["bc"]] = bc
        f128[i, qf["g"]] = _dot01(tril, lac)
        f64[i, qp["dmat"]] = _dot01(tril, lac[:, 0:c] * strict_f)

    for ci, hh in bodies:
        i, rs, ls = where(ci, hh)
        qc = f128[i, qf["q"]]
        kc = f128[i, qf["k"]]
        bc = f128[i, qf["bc"]]
        g = f128[i, qf["g"]]
        decay = jnp.where(causal, jnp.exp(jnp.where(causal, f64[i, qp["dmat"]], 0.0)), 0.0)
        kb = kc.astype(BF16)
        qkk = _dot_nt(jnp.concatenate([qc.astype(BF16), kb], axis=0), kb)
        lower = jnp.where(strict, bc[:, 0:c] * qkk[c:] * decay, 0.0)
        gamma = jnp.exp(g)
        glast = g[c - 1:c, :]
        b64[i, q6["aqk"]] = jnp.where(causal, qkk[0:c] * decay, 0.0).astype(BF16)
        b64[i, q6["nc"]] = (-lower).astype(BF16)
        f64[i, qp["p"]] = eye - lower
        rhs_s[i] = jnp.concatenate([f128[i, qf["vb"]], kc * (bc * gamma)], axis=1).astype(BF16)
        b128[i, qb["qg"]] = (qc * gamma).astype(BF16)
        kdt_s[i] = (kc * jnp.exp(glast - g)).T.astype(BF16)
        cd_s[i] = jnp.broadcast_to(jnp.exp(glast), (SUBLANES, LANES))

    for _ in range(5):
        for ci, hh in bodies:
            i, _, _ = where(ci, hh)
            nc = b64[i, q6["nc"]]
            b64[i, q6["nc"]] = _dot(nc, nc).astype(BF16)
        for ci, hh in bodies:
            i, _, _ = where(ci, hh)
            p = f64[i, qp["p"]]
            f64[i, qp["p"]] = p + _dot(p.astype(BF16), b64[i, q6["nc"]])

    for ci, hh in bodies:
        i, _, _ = where(ci, hh)
        sol = _dot(f64[i, qp["p"]].astype(BF16), rhs_s[i])
        f128[i, qf["u0"]] = sol[:, 0:LANES]
        b128[i, qb["w"]] = sol[:, LANES:].astype(BF16)

    for ci, hh in bodies:
        i, _, _ = where(ci, hh)
        s = s_ref[hh]
        sb = s.astype(BF16)
        ub = (f128[i, qf["u0"]] - _dot(b128[i, qb["w"]], sb)).astype(BF16)
        f128[i, qf["o"]] = _dot(b128[i, qb["qg"]], sb) + _dot(b64[i, q6["aqk"]], ub)
        s_ref[hh] = cd_s[i][0:1] * s + _dot(kdt_s[i], ub)

    for ci, hh in bodies:
        i, rs, ls = where(ci, hh)
        o = f128[i, qf["o"]]
        on = o * lax.rsqrt(jnp.mean(o * o, axis=-1, keepdims=True) + NORM_EPS) * onorm
        o_ref[0, rs, ls] = (on * _silu(z_ref[0, rs, ls])).astype(o_ref.dtype)


def _deltanet(p, ba, conv_w, a_log, dt_bias, out_norm, nheads):
    b, t, _ = p.shape
    tc = _tile(t, 256)
    c = DN_CHUNK
    assert tc % c == 0
    hb = tc // SUBLANES
    nh = nheads
    hg = _tile(nh, 4)
    ng = nh // hg
    wd = hg * LANES
    nbody = (tc // c) * hg
    cw = conv_w.reshape(CONV_TAPS, 3, nh, LANES).transpose(2, 1, 0, 3).reshape(nh, 3 * CONV_TAPS, LANES)
    lane = jnp.arange(LANES)[None, None, :]
    head = jnp.arange(nh)[:, None, None]
    sel = jnp.concatenate([(lane == head), (lane == head + nh)], axis=1).astype(F32)
    hp = jnp.broadcast_to(jnp.stack([a_log, dt_bias], axis=1)[:, :, None], (nh, 2, LANES)).astype(F32)

    def main(sec):
        return pl.BlockSpec((1, tc, wd), lambda i, h, j: (i, j, sec * ng + h))

    def halo(sec):
        return pl.BlockSpec((1, SUBLANES, wd), lambda i, h, j: (i, jnp.maximum(j * hb - 1, 0), sec * ng + h))

    per_head = lambda rows: pl.BlockSpec((hg, rows, LANES), lambda i, h, j: (h, 0, 0))
    return pl.pallas_call(
        _dn_kernel,
        grid=(b, ng, t // tc),
        in_specs=[main(0), main(1), main(2), halo(0), halo(1), halo(2), main(3),
                  pl.BlockSpec((1, tc, LANES), lambda i, h, j: (i, j, 0)),
                  per_head(2), per_head(2), per_head(3 * CONV_TAPS),
                  pl.BlockSpec((1, LANES), lambda i, h, j: (0, 0))],
        out_specs=pl.BlockSpec((1, tc, wd), lambda i, h, j: (i, j, h)),
        out_shape=jax.ShapeDtypeStruct((b, t, nh * LANES), BF16),
        scratch_shapes=[pltpu.VMEM((hg, LANES, LANES), F32),
                        pltpu.VMEM((nbody, len(_DN_F128), c, LANES), F32),
                        pltpu.VMEM((nbody, len(_DN_B128), c, LANES), BF16),
                        pltpu.VMEM((nbody, len(_DN_B64), c, c), BF16),
                        pltpu.VMEM((nbody, len(_DN_F64), c, c), F32),
                        pltpu.VMEM((nbody, c, 2 * LANES), BF16),
                        pltpu.VMEM((nbody, LANES, c), BF16),
                        pltpu.VMEM((nbody, SUBLANES, LANES), F32)],
        compiler_params=_params("parallel", "parallel", "arbitrary"),
        name="deltanet",
    )(p, p, p, p, p, p, p, ba, sel, hp, cw, out_norm.reshape(1, LANES))


def _hg_kernel(q_ref, f_ref, i_ref, z_ref, lb_ref, on_ref, o_ref, st_ref, *, layer):
    t = pl.program_id(2)
    tc = q_ref.shape[1]
    c = HG_CHUNK

    @pl.when(t == 0)
    def _():
        st_ref[...] = jnp.zeros_like(st_ref)

    logits = lb_ref[...]
    e = jnp.exp(logits - jnp.max(logits, axis=0, keepdims=True))
    share = e / jnp.sum(e, axis=0, keepdims=True)
    lbound = jnp.sum(share[0:layer + 1], axis=0, keepdims=True) - share[0:1]

    q = _silu(q_ref[0]) * (LANES ** -0.5)
    forget = lbound + (1.0 - lbound) * jax.nn.sigmoid(f_ref[0])
    k = 1.0 - forget
    logf = jnp.log(forget)
    v = i_ref[0]

    tril = (_iota2((c, c), 0) >= _iota2((c, c), 1)).astype(BF16)
    causal3 = _iota2((c, c, LANES), 0) >= _iota2((c, c, LANES), 1)
    onorm = on_ref[...]

    for ci in range(tc // c):
        rs = slice(ci * c, (ci + 1) * c)
        qc, kc, vc = q[rs], k[rs], v[rs]
        bcum = _dot01(tril, logf[rs])
        diff = bcum[:, None, :] - bcum[None, :, :]
        dec = jnp.exp(jnp.where(causal3, diff, -jnp.inf))
        att = jnp.sum(qc[:, None, :] * kc[None, :, :] * dec, axis=-1, keepdims=True)
        o_intra = jnp.sum(att * vc[None, :, :], axis=1)
        blast = bcum[c - 1:c, :]
        st = st_ref[...]
        o = _bdot_nt(qc * jnp.exp(bcum), st) + o_intra
        st_ref[...] = st * jnp.exp(blast) + _bdot_tn(vc, kc * jnp.exp(blast - bcum))
        on = o * lax.rsqrt(jnp.mean(o * o, axis=-1, keepdims=True) + NORM_EPS) * onorm
        o_ref[0, rs, :] = (on * _silu(z_ref[0, rs, :])).astype(o_ref.dtype)


def _hgrn2(p, col0, lb_logits, out_norm, nheads, layer):
    b, t, _ = p.shape
    tc = _tile(t, 256)
    nh = nheads

    def main(sec):
        return pl.BlockSpec((1, tc, LANES), lambda i, h, j: (i, j, col0 + sec * nh + h))

    nl = lb_logits.shape[0]
    return pl.pallas_call(
        functools.partial(_hg_kernel, layer=layer),
        grid=(b, nh, t // tc),
        in_specs=[main(0), main(1), main(2), main(3),
                  pl.BlockSpec((nl, LANES), lambda i, h, j: (0, h)),
                  pl.BlockSpec((1, LANES), lambda i, h, j: (0, 0))],
        out_specs=pl.BlockSpec((1, tc, LANES), lambda i, h, j: (i, j, h)),
        out_shape=jax.ShapeDtypeStruct((b, t, nh * LANES), BF16),
        scratch_shapes=[pltpu.VMEM((LANES, LANES), F32)],
        compiler_params=_params("parallel", "parallel", "arbitrary"),
        name="hgrn2",
    )(p, p, p, p, lb_logits, out_norm.reshape(1, LANES))


_RW_B64 = dict(nc=0, aak=1, rt=2, wc=3, vb=4, arb=5, ark=6)
_RW_B128 = dict(nb=0, vbd=1, bkt=2, atbd=3, avbd=4)
_RW_F64 = dict(p=0, uc=1, g=2, bv=3, lw=4, kkr=5, k2=6, ag=7, v=8, ssq=9, y=10, d=11)


def _rw_kernel(*refs, has_vres):
    if has_vres:
        (r_ref, k_ref, v_ref, z_ref, wl_ref, al_ref, vl_ref, vf_ref, prm_ref,
         o_ref, h_ref, s64, s128, f64, f128) = refs
    else:
        r_ref, k_ref, v_ref, z_ref, wl_ref, al_ref, prm_ref, o_ref, h_ref, s64, s128, f64, f128 = refs
    t = pl.program_id(2)
    tc = o_ref.shape[1]
    pairs = h_ref.shape[0]
    c = RW_CHUNK
    nchunk = tc // c
    b64, b128, q64 = _RW_B64, _RW_B128, _RW_F64

    @pl.when(t == 0)
    def _():
        h_ref[...] = jnp.zeros_like(h_ref)

    lane1 = _iota2((1, LANES), 1)
    m0 = (lane1 < RW_HEAD).astype(F32)
    m1 = 1.0 - m0
    same_head = (_iota2((LANES, LANES), 0) < RW_HEAD) == (_iota2((LANES, LANES), 1) < RW_HEAD)
    bones = same_head.astype(BF16)
    bdmask = same_head.astype(F32)
    row = _iota2((c, LANES), 0)
    scol = jnp.bitwise_and(_iota2((c, LANES), 1), RW_HEAD - 1)
    incl = row >= scol
    strict = row > scol
    eye_cat = (row == scol).astype(F32)
    tril = (_iota2((c, c), 0) >= _iota2((c, c), 1)).astype(BF16)

    def bd(x):
        return jnp.concatenate([x * m0, x * m1], axis=0)

    def gsum(x):
        return _dot(x.astype(BF16), bones)

    bodies = [(ci, pp) for ci in range(nchunk) for pp in range(pairs)]

    def where(ci, pp):
        return ci * pairs + pp, slice(ci * c, (ci + 1) * c), slice(pp * LANES, (pp + 1) * LANES)

    for ci, pp in bodies:
        i, rs, ls = where(ci, pp)
        prm = prm_ref[:, ls]
        w0, a0, k_k, k_a, r_k, v0 = prm[0:1], prm[1:2], prm[2:3], prm[3:4], prm[4:5], prm[7:8]
        r = r_ref[0, 0, rs, ls]
        k = k_ref[0, 0, rs, ls]
        v = v_ref[0, 0, rs, ls]
        lw = -EXP_NEG_HALF * jax.nn.sigmoid(w0 + wl_ref[0, rs, ls])
        ag = jax.nn.sigmoid(a0 + al_ref[0, rs, ls])
        if has_vres:
            v = v + (vf_ref[0, 0, rs, ls] - v) * jax.nn.sigmoid(v0 + vl_ref[0, rs, ls])
        kkr = k * k_k
        k2 = k * (1.0 + (ag - 1.0) * k_a)
        sums = gsum(jnp.concatenate([kkr * kkr, r * k2 * r_k], axis=0))
        f64[i, q64["g"]] = _dot01(tril, lw)
        f64[i, q64["lw"]] = lw
        f64[i, q64["ag"]] = ag
        f64[i, q64["v"]] = v
        f64[i, q64["kkr"]] = kkr
        f64[i, q64["k2"]] = k2
        f64[i, q64["ssq"]] = sums[0:c]
        f64[i, q64["bv"]] = sums[c:] * v

    for ci, pp in bodies:
        i, rs, ls = where(ci, pp)
        g = f64[i, q64["g"]]
        lw = f64[i, q64["lw"]]
        v = f64[i, q64["v"]]
        kk = f64[i, q64["kkr"]] * lax.rsqrt(f64[i, q64["ssq"]] + L2_EPS)
        e_g = jnp.exp(g)
        e_gn = jnp.exp(-g)
        rt = r_ref[0, 0, rs, ls] * e_g
        at = -kk * jnp.exp(g - lw)
        kt = f64[i, q64["k2"]] * e_gn
        bt = kk * f64[i, q64["ag"]] * e_gn
        x = _bdot_nt(jnp.concatenate([at, rt], axis=0), jnp.concatenate([bd(kt), bd(bt)], axis=0))
        a_ab = jnp.where(strict, x[0:c, LANES:], 0.0)
        e_last = e_g[c - 1:c, :]
        s64[i, b64["aak"]] = jnp.where(strict, x[0:c, 0:LANES], 0.0).astype(BF16)
        s64[i, b64["ark"]] = jnp.where(incl, x[c:, 0:LANES], 0.0).astype(BF16)
        s64[i, b64["arb"]] = jnp.where(incl, x[c:, LANES:], 0.0).astype(BF16)
        s64[i, b64["nc"]] = a_ab.astype(BF16)
        s128[i, b128["nb"]] = bd(a_ab).astype(BF16)
        f64[i, q64["p"]] = eye_cat + a_ab
        s64[i, b64["rt"]] = rt.astype(BF16)
        s64[i, b64["vb"]] = v.astype(BF16)
        s128[i, b128["atbd"]] = bd(at).astype(BF16)
        s128[i, b128["vbd"]] = bd(v).astype(BF16)
        s128[i, b128["bkt"]] = jnp.concatenate([bt * e_last, kt * e_last], axis=0).T.astype(BF16)
        f128[i] = jnp.broadcast_to(e_last, (LANES, LANES)).T

    for ci, pp in bodies:
        i, _, _ = where(ci, pp)
        s128[i, b128["avbd"]] = bd(_dot(s64[i, b64["aak"]], s128[i, b128["vbd"]])).astype(BF16)

    for _ in range(5):
        for ci, pp in bodies:
            i, _, _ = where(ci, pp)
            n2 = _dot(s64[i, b64["nc"]], s128[i, b128["nb"]])
            s64[i, b64["nc"]] = n2.astype(BF16)
            s128[i, b128["nb"]] = bd(n2).astype(BF16)
        for ci, pp in bodies:
            i, _, _ = where(ci, pp)
            p = f64[i, q64["p"]]
            f64[i, q64["p"]] = p + _dot(p.astype(BF16), s128[i, b128["nb"]])

    for ci, pp in bodies:
        i, _, _ = where(ci, pp)
        rhs = jnp.concatenate([s128[i, b128["atbd"]], s128[i, b128["avbd"]]], axis=1)
        wu = _dot(f64[i, q64["p"]].astype(BF16), rhs)
        s64[i, b64["wc"]] = wu[:, 0:LANES].astype(BF16)
        f64[i, q64["uc"]] = wu[:, LANES:]

    for ci, pp in bodies:
        i, _, _ = where(ci, pp)
        h = h_ref[pp]
        hb = h.astype(BF16)
        sa = _dot(s64[i, b64["wc"]], hb) + f64[i, q64["uc"]]
        sab = sa.astype(BF16)
        y = (_dot(s64[i, b64["rt"]], hb)
             + _dot(jnp.concatenate([s64[i, b64["arb"]], s64[i, b64["ark"]]], axis=1),
                    jnp.concatenate([bd(sa).astype(BF16), s128[i, b128["vbd"]]], axis=0)))
        h_ref[pp] = f128[i] * h + bdmask * _dot(s128[i, b128["bkt"]],
                                                jnp.concatenate([sab, s64[i, b64["vb"]]], axis=0))
        f64[i, q64["y"]] = y

    for ci, pp in bodies:
        i, _, _ = where(ci, pp)
        y = f64[i, q64["y"]]
        f64[i, q64["d"]] = y - gsum(y) * (1.0 / RW_HEAD)
    for ci, pp in bodies:
        i, rs, ls = where(ci, pp)
        prm = prm_ref[:, ls]
        d = f64[i, q64["d"]]
        var = gsum(d * d) * (1.0 / RW_HEAD)
        yn = d * lax.rsqrt(var + GN_EPS) * prm[5:6] + prm[6:7]
        o_ref[0, rs, ls] = ((yn + f64[i, q64["bv"]]) * _silu(z_ref[0, 0, rs, ls])).astype(o_ref.dtype)


def _rwkv7(rkvz, wl, al, vl, v_first, prm):
    _, b, t, d = rkvz.shape
    tc = _tile(t, 256)
    has_vres = vl is not None
    pg = _tile(d // LANES, 4)
    wd = pg * LANES
    c = RW_CHUNK
    nbody = (tc // c) * pg

    def sec(s):
        return pl.BlockSpec((1, 1, tc, wd), lambda i, p, j: (s, i, j, p))

    flat = pl.BlockSpec((1, tc, wd), lambda i, p, j: (i, j, p))
    in_specs = [sec(0), sec(1), sec(2), sec(3), flat, flat]
    args = [rkvz, rkvz, rkvz, rkvz, wl, al]
    if has_vres:
        in_specs += [flat, sec(2)]
        args += [vl, v_first]
    in_specs.append(pl.BlockSpec((8, wd), lambda i, p, j: (0, p)))
    args.append(prm)
    return pl.pallas_call(
        functools.partial(_rw_kernel, has_vres=has_vres),
        grid=(b, d // wd, t // tc),
        in_specs=in_specs,
        out_specs=flat,
        out_shape=jax.ShapeDtypeStruct((b, t, d), BF16),
        scratch_shapes=[pltpu.VMEM((pg, LANES, LANES), F32),
                        pltpu.VMEM((nbody, len(_RW_B64), c, LANES), BF16),
                        pltpu.VMEM((nbody, len(_RW_B128), 2 * c, LANES), BF16),
                        pltpu.VMEM((nbody, len(_RW_F64), c, LANES), F32),
                        pltpu.VMEM((nbody, LANES, LANES), F32)],
        compiler_params=_params("parallel", "parallel", "arbitrary"),
        name="rwkv7",
    )(*args)


def kernel(x, norm_gains, mix_w_in, dn_conv, dn_a_log, dn_dt_bias, dn_out_norm, hg_lb_logits, hg_out_norm,
           mix_w_out, rw_mu, rw_w_rkvz, rw_w0, rw_w1, rw_w2, rw_a0, rw_a1, rw_a2, rw_v0, rw_v1, rw_v2,
           rw_k_k, rw_k_a, rw_r_k, rw_ln_w, rw_ln_b, rw_w_out, final_norm):
    b, t, d = x.shape
    m = b * t
    depth = norm_gains.shape[0]
    dn_heads = dn_a_log.shape[1]
    dn_width = dn_conv.shape[2] // 3
    hg_width = hg_lb_logits.shape[1]
    hg_heads = hg_width // LANES
    assert dn_width == dn_heads * LANES and hg_out_norm.shape[1] == LANES and rw_r_k.shape[2] == RW_HEAD
    dn_ba = 4 * dn_width
    hg_q = dn_ba + 2 * dn_heads
    assert 2 * dn_heads <= LANES

    h = x.reshape(m, d)
    v_first = None
    for layer in range(depth):
        gain = norm_gains[layer]
        if layer % 2 == 0:
            e = layer // 2
            w_in = mix_w_in[e]
            w_main = jnp.concatenate([w_in[:, :dn_ba], w_in[:, hg_q:]], axis=1).astype(BF16)
            w_ba = jnp.pad(w_in[:, dn_ba:hg_q], ((0, 0), (0, LANES - 2 * dn_heads))).astype(BF16)
            hn = _rmsnorm(h, gain, BF16)[None]
            p = _matmul(hn, w_main[None])[0].reshape(b, t, -1)
            ba = _matmul(hn, w_ba[None])[0].reshape(b, t, LANES)
            o_a = _deltanet(p, ba, dn_conv[e], dn_a_log[e], dn_dt_bias[e], dn_out_norm[e], dn_heads)
            o_b = _hgrn2(p, dn_ba // LANES, hg_lb_logits, hg_out_norm[e], hg_heads, e)
            o = jnp.concatenate([o_a, o_b], axis=-1).reshape(1, m, -1)
            h = _matmul(o, mix_w_out[e].astype(BF16)[None], residual=h)[0]
        else:
            o = layer // 2
            mu = rw_mu[o][jnp.array([0, 2, 3, 5, 1, 4])]
            mixes = _rms_mix(h.reshape(b, t, d), gain, mu).reshape(6, m, d)
            rkvz = _matmul(mixes, rw_w_rkvz[o].astype(BF16), groups=4).reshape(4, b, t, d)
            wl = _lora(mixes, 4, rw_w1[o], rw_w2[o], True).reshape(b, t, d)
            al = _lora(mixes, 5, rw_a1[o], rw_a2[o], False).reshape(b, t, d)
            vl = None
            v0 = jnp.zeros((d,), F32)
            if v_first is None:
                v_first = rkvz
            else:
                vl = _lora(mixes, 2, rw_v1[o - 1], rw_v2[o - 1], False).reshape(b, t, d)
                v0 = rw_v0[o - 1]
            prm = jnp.stack([rw_w0[o], rw_a0[o], rw_k_k[o], rw_k_a[o], rw_r_k[o].reshape(d),
                             rw_ln_w[o], rw_ln_b[o], v0]).astype(F32)
            y = _rwkv7(rkvz, wl, al, vl, v_first, prm)
            h = _matmul(y.reshape(1, m, d), rw_w_out[o].astype(BF16)[None], residual=h)[0]
    return _rmsnorm(h, final_norm, x.dtype).reshape(b, t, d)
```

```python
import functools
import math

import jax
import jax.numpy as jnp
from jax import lax
from jax.experimental import pallas as pl
from jax.experimental.pallas import tpu as pltpu

F32 = jnp.float32
BF16 = jnp.bfloat16

NORM_EPS = 1e-6
GN_EPS = 64e-5
L2_EPS = 1e-6
CONV_TAPS = 4

LANES = 128
SUBLANES = 8
V7X_VMEM_LIMIT_BYTES = 56 * 1024 * 1024

DN_CHUNK = 64
RW_CHUNK = 64
RW_HEAD = 64
EXP_NEG_HALF = math.exp(-0.5)


def _params(*semantics):
    return pltpu.CompilerParams(dimension_semantics=semantics, vmem_limit_bytes=V7X_VMEM_LIMIT_BYTES)


def _tile(dim, pref):
    t = min(dim, pref)
    assert dim % t == 0, (dim, pref)
    return t


def _dot(a, b):
    return jnp.dot(a, b, preferred_element_type=F32)


def _dot_nt(a, b):
    return lax.dot_general(a, b, (((1,), (1,)), ((), ())), preferred_element_type=F32)


def _bdot_nt(a, b):
    return _dot_nt(a.astype(BF16), b.astype(BF16))


def _dot01(m01, x):
    x1 = x.astype(BF16)
    x2 = (x - x1.astype(F32)).astype(BF16)
    return _dot(m01, x1) + _dot(m01, x2)


def _silu(x):
    return x * jax.nn.sigmoid(x)


def _softplus(x):
    return jnp.maximum(x, 0.0) + jnp.log1p(jnp.exp(-jnp.abs(x)))


def _iota2(shape, dim):
    return lax.broadcasted_iota(jnp.int32, shape, dim)


def _shift_rows(x, halo, j):
    full = pltpu.roll(x, j, axis=0)
    hfix = pltpu.roll(halo, j, axis=0)
    top = jnp.where(_iota2(hfix.shape, 0) < j, hfix, full[0:SUBLANES])
    return jnp.concatenate([top, full[SUBLANES:]], axis=0)


def _rms_kernel(x_ref, g_ref, o_ref):
    x = x_ref[...]
    y = x * lax.rsqrt(jnp.mean(x * x, axis=-1, keepdims=True) + NORM_EPS)
    o_ref[...] = (y * g_ref[...]).astype(o_ref.dtype)


def _rmsnorm(x, gain, out_dtype):
    m, d = x.shape
    tm = _tile(m, 512)
    return pl.pallas_call(
        _rms_kernel,
        grid=(m // tm,),
        in_specs=[pl.BlockSpec((tm, d), lambda i: (i, 0)), pl.BlockSpec((1, d), lambda i: (0, 0))],
        out_specs=pl.BlockSpec((tm, d), lambda i: (i, 0)),
        out_shape=jax.ShapeDtypeStruct((m, d), out_dtype),
        compiler_params=_params("parallel"),
        name="rmsnorm",
    )(x, gain.reshape(1, d))


def _rms_mix_kernel(x_ref, halo_ref, g_ref, mu_ref, o_ref):
    t = pl.program_id(1)
    g = g_ref[...]

    def norm(x):
        return x * lax.rsqrt(jnp.mean(x * x, axis=-1, keepdims=True) + NORM_EPS) * g

    hn = norm(x_ref[0])
    hp = jnp.where(t > 0, norm(halo_ref[0]), 0.0)
    xx = _shift_rows(hn, hp, 1) - hn
    for i in range(o_ref.shape[0]):
        o_ref[i, 0] = (hn + xx * mu_ref[i:i + 1, :]).astype(o_ref.dtype)


def _rms_mix(h, gain, mu):
    b, t, d = h.shape
    tt = _tile(t, 256)
    nmix = mu.shape[0]
    return pl.pallas_call(
        _rms_mix_kernel,
        grid=(b, t // tt),
        in_specs=[
            pl.BlockSpec((1, tt, d), lambda i, j: (i, j, 0)),
            pl.BlockSpec((1, SUBLANES, d), lambda i, j: (i, jnp.maximum(j * (tt // SUBLANES) - 1, 0), 0)),
            pl.BlockSpec((1, d), lambda i, j: (0, 0)),
            pl.BlockSpec((nmix, d), lambda i, j: (0, 0)),
        ],
        out_specs=pl.BlockSpec((nmix, 1, tt, d), lambda i, j: (0, i, j, 0)),
        out_shape=jax.ShapeDtypeStruct((nmix, b, t, d), BF16),
        compiler_params=_params("parallel", "parallel"),
        name="rms_mix",
    )(h, h, gain.reshape(1, d), mu)


def _mm_kernel(x_ref, w_ref, o_ref):
    o_ref[0] = _dot(x_ref[0], w_ref[0]).astype(o_ref.dtype)


def _mm_res_kernel(x_ref, w_ref, r_ref, o_ref):
    o_ref[0] = (r_ref[...] + _dot(x_ref[0], w_ref[0])).astype(o_ref.dtype)


def _matmul(x, w, out_dtype=F32, residual=None, groups=None):
    g = w.shape[0] if groups is None else groups
    _, m, k = x.shape
    n = w.shape[2]
    tm = _tile(m, 1024)
    tn = _tile(n, 1024)
    in_specs = [pl.BlockSpec((1, tm, k), lambda a, i, j: (a, i, 0)),
                pl.BlockSpec((1, k, tn), lambda a, i, j: (a, 0, j))]
    args = [x, w]
    body = _mm_kernel
    if residual is not None:
        assert g == 1
        in_specs.append(pl.BlockSpec((tm, tn), lambda a, i, j: (i, j)))
        args.append(residual)
        body = _mm_res_kernel
    return pl.pallas_call(
        body,
        grid=(g, m // tm, n // tn),
        in_specs=in_specs,
        out_specs=pl.BlockSpec((1, tm, tn), lambda a, i, j: (a, i, j)),
        out_shape=jax.ShapeDtypeStruct((g, m, n), out_dtype),
        compiler_params=_params("parallel", "parallel", "parallel"),
        name="matmul",
    )(*args)


def _lora_kernel(x_ref, a_ref, b_ref, o_ref, *, use_tanh):
    mid = _dot(x_ref[0], a_ref[...])
    if use_tanh:
        mid = jnp.tanh(mid)
    o_ref[...] = _dot(mid.astype(BF16), b_ref[...])


def _lora(xs, sel, a, b, use_tanh):
    _, m, k = xs.shape
    rank = a.shape[1]
    n = b.shape[1]
    rpad = -(-rank // LANES) * LANES
    a_p = jnp.pad(a, ((0, 0), (0, rpad - rank))).astype(BF16)
    b_p = jnp.pad(b, ((0, rpad - rank), (0, 0))).astype(BF16)
    tm = _tile(m, 512)
    return pl.pallas_call(
        functools.partial(_lora_kernel, use_tanh=use_tanh),
        grid=(m // tm,),
        in_specs=[pl.BlockSpec((1, tm, k), lambda i: (sel, i, 0)),
                  pl.BlockSpec((k, rpad), lambda i: (0, 0)),
                  pl.BlockSpec((rpad, n), lambda i: (0, 0))],
        out_specs=pl.BlockSpec((tm, n), lambda i: (i, 0)),
        out_shape=jax.ShapeDtypeStruct((m, n), F32),
        compiler_params=_params("parallel"),
        name="lora",
    )(xs, a_p, b_p)


_DN_F128 = dict(q=0, k=1, vb=2, bc=3, g=4, u0=5, o=6)
_DN_B128 = dict(qg=0, w=1)
_DN_B64 = dict(nc=0, aqk=1)
_DN_F64 = dict(p=0, dmat=1)


def _dn_kernel(q_ref, k_ref, v_ref, qh_ref, kh_ref, vh_ref, z_ref, ba_ref, sel_ref, hp_ref, cw_ref, on_ref,
               o_ref, s_ref, f128, b128, b64, f64, rhs_s, kdt_s, cd_s):
    t = pl.program_id(2)
    tc = q_ref.shape[1]
    heads = s_ref.shape[0]
    c = DN_CHUNK
    qf, qb, q6, qp = _DN_F128, _DN_B128, _DN_B64, _DN_F64

    @pl.when(t == 0)
    def _():
        s_ref[...] = jnp.zeros_like(s_ref)

    row = _iota2((c, c), 0)
    col = _iota2((c, c), 1)
    causal = row >= col
    strict = row > col
    strict_f = strict.astype(F32)
    eye = (row == col).astype(F32)
    tril = causal.astype(BF16)
    onorm = on_ref[...]

    bodies = [(ci, hh) for ci in range(tc // c) for hh in range(heads)]

    def where(ci, hh):
        return ci * heads + hh, slice(ci * c, (ci + 1) * c), slice(hh * LANES, (hh + 1) * LANES)

    def conv_silu(x_ref, h_ref, ci, rs, ls, w):
        x = x_ref[0, rs, ls]
        if ci == 0:
            halo = jnp.where(t > 0, h_ref[0, :, ls], 0.0)
        else:
            halo = x_ref[0, ci * c - SUBLANES:ci * c, ls]
        y = x * w[CONV_TAPS - 1:CONV_TAPS, :]
        for j in range(1, CONV_TAPS):
            y = y + _shift_rows(x, halo, j) * w[CONV_TAPS - 1 - j:CONV_TAPS - j, :]
        return _silu(y)

    def l2n(x):
        return x * lax.rsqrt(jnp.sum(x * x, axis=-1, keepdims=True) + L2_EPS)

    for ci, hh in bodies:
        i, rs, ls = where(ci, hh)
        cw = cw_ref[hh]
        sel = sel_ref[hh]
        hp = hp_ref[hh]
        ba = ba_ref[0, rs, :]
        qc = l2n(conv_silu(q_ref, qh_ref, ci, rs, ls, cw[0:CONV_TAPS])) * (LANES ** -0.5)
        kc = l2n(conv_silu(k_ref, kh_ref, ci, rs, ls, cw[CONV_TAPS:2 * CONV_TAPS]))
        vc = conv_silu(v_ref, vh_ref, ci, rs, ls, cw[2 * CONV_TAPS:3 * CONV_TAPS])
        beta = jax.nn.sigmoid(jnp.sum(ba * sel[0:1], axis=-1, keepdims=True))
        alpha_pre = jnp.sum(ba * sel[1:2], axis=-1, keepdims=True)
        bc = jnp.broadcast_to(beta, (c, LANES))
        lac = -jnp.exp(hp[0:1]) * _softplus(jnp.broadcast_to(alpha_pre, (c, LANES)) + hp[1:2])
        f128[i, qf["q"]] = qc
        f128[i, qf["k"]] = kc
        f128[i, qf["vb"]] = vc * bc
        f128[i, qf["bc"]] = bc
        f128[i, qf["g"]] = _dot01(tril, lac)
        f64[i, qp["dmat"]] = _dot01(tril, lac[:, 0:c] * strict_f)

    for ci, hh in bodies:
        i, rs, ls = where(ci, hh)
        qc = f128[i, qf["q"]]
        kc = f128[i, qf["k"]]
        bc = f128[i, qf["bc"]]
        g = f128[i, qf["g"]]
        decay = jnp.where(causal, jnp.exp(jnp.where(causal, f64[i, qp["dmat"]], 0.0)), 0.0)
        kb = kc.astype(BF16)
        qkk = _dot_nt(jnp.concatenate([qc.astype(BF16), kb], axis=0), kb)
        lower = jnp.where(strict, bc[:, 0:c] * qkk[c:] * decay, 0.0)
        gamma = jnp.exp(g)
        glast = g[c - 1:c, :]
        b64[i, q6["aqk"]] = jnp.where(causal, qkk[0:c] * decay, 0.0).astype(BF16)
        b64[i, q6["nc"]] = (-lower).astype(BF16)
        f64[i, qp["p"]] = eye - lower
        rhs_s[i] = jnp.concatenate([f128[i, qf["vb"]], kc * (bc * gamma)], axis=1).astype(BF16)
        b128[i, qb["qg"]] = (qc * gamma).astype(BF16)
        kdt_s[i] = (kc * jnp.exp(glast - g)).T.astype(BF16)
        cd_s[i] = jnp.broadcast_to(jnp.exp(glast), (SUBLANES, LANES))

    for _ in range(5):
        for ci, hh in bodies:
            i, _, _ = where(ci, hh)
            nc = b64[i, q6["nc"]]
            b64[i, q6["nc"]] = _dot(nc, nc).astype(BF16)
        for ci, hh in bodies:
            i, _, _ = where(ci, hh)
            p = f64[i, qp["p"]]
            f64[i, qp["p"]] = p + _dot(p.astype(BF16), b64[i, q6["nc"]])

    for ci, hh in bodies:
        i, _, _ = where(ci, hh)
        sol = _dot(f64[i, qp["p"]].astype(BF16), rhs_s[i])
        f128[i, qf["u0"]] = sol[:, 0:LANES]
        b128[i, qb["w"]] = sol[:, LANES:].astype(BF16)

    for ci, hh in bodies:
        i, _, _ = where(ci, hh)
        s = s_ref[hh]
        sb = s.astype(BF16)
        ub = (f128[i, qf["u0"]] - _dot(b128[i, qb["w"]], sb)).astype(BF16)
        f128[i, qf["o"]] = _dot(b128[i, qb["qg"]], sb) + _dot(b64[i, q6["aqk"]], ub)
        s_ref[hh] = cd_s[i][0:1] * s + _dot(kdt_s[i], ub)

    for ci, hh in bodies:
        i, rs, ls = where(ci, hh)
        o = f128[i, qf["o"]]
        on = o * lax.rsqrt(jnp.mean(o * o, axis=-1, keepdims=True) + NORM_EPS) * onorm
        o_ref[0, rs, ls] = (on * _silu(z_ref[0, rs, ls])).astype(o_ref.dtype)


def _deltanet(p, ba, conv_w, a_log, dt_bias, out_norm, nheads):
    b, t, _ = p.shape
    tc = _tile(t, 256)
    c = DN_CHUNK
    assert tc % c == 0
    hb = tc // SUBLANES
    nh = nheads
    hg = _tile(nh, 4)
    ng = nh // hg
    wd = hg * LANES
    nbody = (tc // c) * hg
    cw = conv_w.reshape(CONV_TAPS, 3, nh, LANES).transpose(2, 1, 0, 3).reshape(nh, 3 * CONV_TAPS, LANES)
    lane = jnp.arange(LANES)[None, None, :]
    head = jnp.arange(nh)[:, None, None]
    sel = jnp.concatenate([(lane == head), (lane == head + nh)], axis=1).astype(F32)
    hp = jnp.broadcast_to(jnp.stack([a_log, dt_bias], axis=1)[:, :, None], (nh, 2, LANES)).astype(F32)

    def main(sec):
        return pl.BlockSpec((1, tc, wd), lambda i, h, j: (i, j, sec * ng + h))

    def halo(sec):
        return pl.BlockSpec((1, SUBLANES, wd), lambda i, h, j: (i, jnp.maximum(j * hb - 1, 0), sec * ng + h))

    per_head = lambda rows: pl.BlockSpec((hg, rows, LANES), lambda i, h, j: (h, 0, 0))
    return pl.pallas_call(
        _dn_kernel,
        grid=(b, ng, t // tc),
        in_specs=[main(0), main(1), main(2), halo(0), halo(1), halo(2), main(3),
                  pl.BlockSpec((1, tc, LANES), lambda i, h, j: (i, j, 0)),
                  per_head(2), per_head(2), per_head(3 * CONV_TAPS),
                  pl.BlockSpec((1, LANES), lambda i, h, j: (0, 0))],
        out_specs=pl.BlockSpec((1, tc, wd), lambda i, h, j: (i, j, h)),
        out_shape=jax.ShapeDtypeStruct((b, t, nh * LANES), BF16),
        scratch_shapes=[pltpu.VMEM((hg, LANES, LANES), F32),
                        pltpu.VMEM((nbody, len(_DN_F128), c, LANES), F32),
                        pltpu.VMEM((nbody, len(_DN_B128), c, LANES), BF16),
                        pltpu.VMEM((nbody, len(_DN_B64), c, c), BF16),
                        pltpu.VMEM((nbody, len(_DN_F64), c, c), F32),
                        pltpu.VMEM((nbody, c, 2 * LANES), BF16),
                        pltpu.VMEM((nbody, LANES, c), BF16),
                        pltpu.VMEM((nbody, SUBLANES, LANES), F32)],
        compiler_params=_params("parallel", "parallel", "arbitrary"),
        name="deltanet",
    )(p, p, p, p, p, p, p, ba, sel, hp, cw, out_norm.reshape(1, LANES))


HG_LEVELS = 6
HG_BLOCK = 64


def _hg_tables():
    import numpy as np
    c = HG_BLOCK
    t = np.arange(c)[:, None]
    i = np.arange(c)[None, :]
    rows = [i <= t, i > t]
    q_rows, k_rows = [], []
    level = np.full((c, c), -1, np.int32)
    level[np.arange(c), np.arange(c)] = HG_LEVELS
    for li in range(HG_LEVELS):
        m = c >> (li + 1)
        pos = t % (2 * m)
        upper = pos >= m
        ref = t - pos + m - 1
        q_rows.append(upper & (i > ref) & (i <= t))
        k_rows.append((~upper) & (i > t) & (i <= ref))
        same = (t // (2 * m)) == (i // (2 * m))
        level[same & upper & ((i % (2 * m)) < m)] = li
    table = np.concatenate(rows + q_rows + k_rows, axis=0).astype(np.float32)
    return table, level


def _hg_kernel(q_ref, f_ref, i_ref, z_ref, lb_ref, on_ref, tab_ref, lvl_ref, o_ref,
               s_ref, qt_s, kt_s, qb_s, vb_s, klt_s, att_s, ecol_s, inc_s, o_s, *, layer):
    t = pl.program_id(2)
    tc = q_ref.shape[1]
    heads = s_ref.shape[0]
    c = HG_BLOCK
    nl = HG_LEVELS

    @pl.when(t == 0)
    def _():
        s_ref[...] = jnp.zeros_like(s_ref)

    table = tab_ref[...]
    level = lvl_ref[...]
    onorm = on_ref[...]
    bodies = [(ci, hh) for ci in range(tc // c) for hh in range(heads)]

    def where(ci, hh):
        return ci * heads + hh, slice(ci * c, (ci + 1) * c), slice(hh * LANES, (hh + 1) * LANES)

    for ci, hh in bodies:
        i, rs, ls = where(ci, hh)
        logits = lb_ref[:, ls]
        e = jnp.exp(logits - jnp.max(logits, axis=0, keepdims=True))
        share = e / jnp.sum(e, axis=0, keepdims=True)
        lbound = jnp.sum(share[0:layer + 1], axis=0, keepdims=True) - share[0:1]
        q = _silu(q_ref[0, rs, ls]) * (LANES ** -0.5)
        forget = lbound + (1.0 - lbound) * jax.nn.sigmoid(f_ref[0, rs, ls])
        k = 1.0 - forget
        d = _dot01(table, jnp.log(forget))
        b = d[0:c]
        qb_s[i] = (q * jnp.exp(b)).astype(BF16)
        klt_s[i] = (k * jnp.exp(d[c:2 * c])).T.astype(BF16)
        ecol_s[i] = jnp.broadcast_to(jnp.exp(b[c - 1:c]), (LANES, LANES)).T
        for li in range(nl):
            qt_s[i, li] = (q * jnp.exp(d[(2 + li) * c:(3 + li) * c])).astype(BF16)
            kt_s[i, li] = (k * jnp.exp(d[(2 + nl + li) * c:(3 + nl + li) * c])).astype(BF16)
        qt_s[i, nl] = q.astype(BF16)
        kt_s[i, nl] = k.astype(BF16)
        vb_s[i] = i_ref[0, rs, ls].astype(BF16)

    for ci, hh in bodies:
        i, _, _ = where(ci, hh)
        att = jnp.zeros((c, c), F32)
        for li in range(nl + 1):
            att = att + jnp.where(level == li, _dot_nt(qt_s[i, li], kt_s[i, li]), 0.0)
        att_s[i] = att.astype(BF16)
        inc_s[i] = _dot(klt_s[i], vb_s[i])

    for ci, hh in bodies:
        i, _, _ = where(ci, hh)
        s = s_ref[hh]
        o_s[i] = _dot(qb_s[i], s.astype(BF16)) + _dot(att_s[i], vb_s[i])
        s_ref[hh] = ecol_s[i] * s + inc_s[i]

    for ci, hh in bodies:
        i, rs, ls = where(ci, hh)
        o = o_s[i]
        on = o * lax.rsqrt(jnp.mean(o * o, axis=-1, keepdims=True) + NORM_EPS) * onorm
        o_ref[0, rs, ls] = (on * _silu(z_ref[0, rs, ls])).astype(o_ref.dtype)


def _hgrn2(p, col0, lb_logits, out_norm, nheads, layer):
    b, t, _ = p.shape
    tc = _tile(t, 256)
    c = HG_BLOCK
    assert tc % c == 0
    nh = nheads
    hg = _tile(nh, 4)
    ng = nh // hg
    wd = hg * LANES
    assert col0 % hg == 0
    cg = col0 // hg
    nbody = (tc // c) * hg
    table, level = _hg_tables()
    nrow = table.shape[0]

    def main(sec):
        return pl.BlockSpec((1, tc, wd), lambda i, h, j: (i, j, cg + sec * ng + h))

    nl = lb_logits.shape[0]
    const = lambda shape: pl.BlockSpec(shape, lambda i, h, j: (0, 0))
    return pl.pallas_call(
        functools.partial(_hg_kernel, layer=layer),
        grid=(b, ng, t // tc),
        in_specs=[main(0), main(1), main(2), main(3),
                  pl.BlockSpec((nl, wd), lambda i, h, j: (0, h)),
                  const((1, LANES)), const((nrow, c)), const((c, c))],
        out_specs=pl.BlockSpec((1, tc, wd), lambda i, h, j: (i, j, h)),
        out_shape=jax.ShapeDtypeStruct((b, t, nh * LANES), BF16),
        scratch_shapes=[pltpu.VMEM((hg, LANES, LANES), F32),
                        pltpu.VMEM((nbody, HG_LEVELS + 1, c, LANES), BF16),
                        pltpu.VMEM((nbody, HG_LEVELS + 1, c, LANES), BF16),
                        pltpu.VMEM((nbody, c, LANES), BF16),
                        pltpu.VMEM((nbody, c, LANES), BF16),
                        pltpu.VMEM((nbody, LANES, c), BF16),
                        pltpu.VMEM((nbody, c, c), BF16),
                        pltpu.VMEM((nbody, LANES, LANES), F32),
                        pltpu.VMEM((nbody, LANES, LANES), F32),
                        pltpu.VMEM((nbody, c, LANES), F32)],
        compiler_params=_params("parallel", "parallel", "arbitrary"),
        name="hgrn2",
    )(p, p, p, p, lb_logits, out_norm.reshape(1, LANES), jnp.asarray(table, BF16), jnp.asarray(level))


_RW_B64 = dict(nc=0, aak=1, rt=2, wc=3, vb=4, arb=5, ark=6)
_RW_B128 = dict(nb=0, vbd=1, bkt=2, atbd=3, avbd=4)
_RW_F64 = dict(p=0, uc=1, g=2, bv=3, lw=4, kkr=5, k2=6, ag=7, v=8, ssq=9, y=10, d=11)


def _rw_kernel(*refs, has_vres):
    if has_vres:
        (r_ref, k_ref, v_ref, z_ref, wl_ref, al_ref, vl_ref, vf_ref, prm_ref,
         o_ref, h_ref, s64, s128, f64, f128) = refs
    else:
        r_ref, k_ref, v_ref, z_ref, wl_ref, al_ref, prm_ref, o_ref, h_ref, s64, s128, f64, f128 = refs
    t = pl.program_id(2)
    tc = o_ref.shape[1]
    pairs = h_ref.shape[0]
    c = RW_CHUNK
    nchunk = tc // c
    b64, b128, q64 = _RW_B64, _RW_B128, _RW_F64

    @pl.when(t == 0)
    def _():
        h_ref[...] = jnp.zeros_like(h_ref)

    lane1 = _iota2((1, LANES), 1)
    m0 = (lane1 < RW_HEAD).astype(F32)
    m1 = 1.0 - m0
    same_head = (_iota2((LANES, LANES), 0) < RW_HEAD) == (_iota2((LANES, LANES), 1) < RW_HEAD)
    bones = same_head.astype(BF16)
    bdmask = same_head.astype(F32)
    row = _iota2((c, LANES), 0)
    scol = jnp.bitwise_and(_iota2((c, LANES), 1), RW_HEAD - 1)
    incl = row >= scol
    strict = row > scol
    eye_cat = (row == scol).astype(F32)
    tril = (_iota2((c, c), 0) >= _iota2((c, c), 1)).astype(BF16)

    def bd(x):
        return jnp.concatenate([x * m0, x * m1], axis=0)

    def gsum(x):
        return _dot(x.astype(BF16), bones)

    bodies = [(ci, pp) for ci in range(nchunk) for pp in range(pairs)]

    def where(ci, pp):
        return ci * pairs + pp, slice(ci * c, (ci + 1) * c), slice(pp * LANES, (pp + 1) * LANES)

    for ci, pp in bodies:
        i, rs, ls = where(ci, pp)
        prm = prm_ref[:, ls]
        w0, a0, k_k, k_a, r_k, v0 = prm[0:1], prm[1:2], prm[2:3], prm[3:4], prm[4:5], prm[7:8]
        r = r_ref[0, 0, rs, ls]
        k = k_ref[0, 0, rs, ls]
        v = v_ref[0, 0, rs, ls]
        lw = -EXP_NEG_HALF * jax.nn.sigmoid(w0 + wl_ref[0, rs, ls])
        ag = jax.nn.sigmoid(a0 + al_ref[0, rs, ls])
        if has_vres:
            v = v + (vf_ref[0, 0, rs, ls] - v) * jax.nn.sigmoid(v0 + vl_ref[0, rs, ls])
        kkr = k * k_k
        k2 = k * (1.0 + (ag - 1.0) * k_a)
        sums = gsum(jnp.concatenate([kkr * kkr, r * k2 * r_k], axis=0))
        f64[i, q64["g"]] = _dot01(tril, lw)
        f64[i, q64["lw"]] = lw
        f64[i, q64["ag"]] = ag
        f64[i, q64["v"]] = v
        f64[i, q64["kkr"]] = kkr
        f64[i, q64["k2"]] = k2
        f64[i, q64["ssq"]] = sums[0:c]
        f64[i, q64["bv"]] = sums[c:] * v

    for ci, pp in bodies:
        i, rs, ls = where(ci, pp)
        g = f64[i, q64["g"]]
        lw = f64[i, q64["lw"]]
        v = f64[i, q64["v"]]
        kk = f64[i, q64["kkr"]] * lax.rsqrt(f64[i, q64["ssq"]] + L2_EPS)
        e_g = jnp.exp(g)
        e_gn = jnp.exp(-g)
        rt = r_ref[0, 0, rs, ls] * e_g
        at = -kk * jnp.exp(g - lw)
        kt = f64[i, q64["k2"]] * e_gn
        bt = kk * f64[i, q64["ag"]] * e_gn
        x = _bdot_nt(jnp.concatenate([at, rt], axis=0), jnp.concatenate([bd(kt), bd(bt)], axis=0))
        a_ab = jnp.where(strict, x[0:c, LANES:], 0.0)
        e_last = e_g[c - 1:c, :]
        s64[i, b64["aak"]] = jnp.where(strict, x[0:c, 0:LANES], 0.0).astype(BF16)
        s64[i, b64["ark"]] = jnp.where(incl, x[c:, 0:LANES], 0.0).astype(BF16)
        s64[i, b64["arb"]] = jnp.where(incl, x[c:, LANES:], 0.0).astype(BF16)
        s64[i, b64["nc"]] = a_ab.astype(BF16)
        s128[i, b128["nb"]] = bd(a_ab).astype(BF16)
        f64[i, q64["p"]] = eye_cat + a_ab
        s64[i, b64["rt"]] = rt.astype(BF16)
        s64[i, b64["vb"]] = v.astype(BF16)
        s128[i, b128["atbd"]] = bd(at).astype(BF16)
        s128[i, b128["vbd"]] = bd(v).astype(BF16)
        s128[i, b128["bkt"]] = jnp.concatenate([bt * e_last, kt * e_last], axis=0).T.astype(BF16)
        f128[i] = jnp.broadcast_to(e_last, (LANES, LANES)).T

    for ci, pp in bodies:
        i, _, _ = where(ci, pp)
        s128[i, b128["avbd"]] = bd(_dot(s64[i, b64["aak"]], s128[i, b128["vbd"]])).astype(BF16)

    for _ in range(5):
        for ci, pp in bodies:
            i, _, _ = where(ci, pp)
            n2 = _dot(s64[i, b64["nc"]], s128[i, b128["nb"]])
            s64[i, b64["nc"]] = n2.astype(BF16)
            s128[i, b128["nb"]] = bd(n2).astype(BF16)
        for ci, pp in bodies:
            i, _, _ = where(ci, pp)
            p = f64[i, q64["p"]]
            f64[i, q64["p"]] = p + _dot(p.astype(BF16), s128[i, b128["nb"]])

    for ci, pp in bodies:
        i, _, _ = where(ci, pp)
        rhs = jnp.concatenate([s128[i, b128["atbd"]], s128[i, b128["avbd"]]], axis=1)
        wu = _dot(f64[i, q64["p"]].astype(BF16), rhs)
        s64[i, b64["wc"]] = wu[:, 0:LANES].astype(BF16)
        f64[i, q64["uc"]] = wu[:, LANES:]

    for ci, pp in bodies:
        i, _, _ = where(ci, pp)
        h = h_ref[pp]
        hb = h.astype(BF16)
        sa = _dot(s64[i, b64["wc"]], hb) + f64[i, q64["uc"]]
        sab = sa.astype(BF16)
        y = (_dot(s64[i, b64["rt"]], hb)
             + _dot(jnp.concatenate([s64[i, b64["arb"]], s64[i, b64["ark"]]], axis=1),
                    jnp.concatenate([bd(sa).astype(BF16), s128[i, b128["vbd"]]], axis=0)))
        h_ref[pp] = f128[i] * h + bdmask * _dot(s128[i, b128["bkt"]],
                                                jnp.concatenate([sab, s64[i, b64["vb"]]], axis=0))
        f64[i, q64["y"]] = y

    for ci, pp in bodies:
        i, _, _ = where(ci, pp)
        y = f64[i, q64["y"]]
        f64[i, q64["d"]] = y - gsum(y) * (1.0 / RW_HEAD)
    for ci, pp in bodies:
        i, rs, ls = where(ci, pp)
        prm = prm_ref[:, ls]
        d = f64[i, q64["d"]]
        var = gsum(d * d) * (1.0 / RW_HEAD)
        yn = d * lax.rsqrt(var + GN_EPS) * prm[5:6] + prm[6:7]
        o_ref[0, rs, ls] = ((yn + f64[i, q64["bv"]]) * _silu(z_ref[0, 0, rs, ls])).astype(o_ref.dtype)


def _rwkv7(rkvz, wl, al, vl, v_first, prm):
    _, b, t, d = rkvz.shape
    tc = _tile(t, 256)
    has_vres = vl is not None
    pg = _tile(d // LANES, 4)
    wd = pg * LANES
    c = RW_CHUNK
    nbody = (tc // c) * pg

    def sec(s):
        return pl.BlockSpec((1, 1, tc, wd), lambda i, p, j: (s, i, j, p))

    flat = pl.BlockSpec((1, tc, wd), lambda i, p, j: (i, j, p))
    in_specs = [sec(0), sec(1), sec(2), sec(3), flat, flat]
    args = [rkvz, rkvz, rkvz, rkvz, wl, al]
    if has_vres:
        in_specs += [flat, sec(2)]
        args += [vl, v_first]
    in_specs.append(pl.BlockSpec((8, wd), lambda i, p, j: (0, p)))
    args.append(prm)
    return pl.pallas_call(
        functools.partial(_rw_kernel, has_vres=has_vres),
        grid=(b, d // wd, t // tc),
        in_specs=in_specs,
        out_specs=flat,
        out_shape=jax.ShapeDtypeStruct((b, t, d), BF16),
        scratch_shapes=[pltpu.VMEM((pg, LANES, LANES), F32),
                        pltpu.VMEM((nbody, len(_RW_B64), c, LANES), BF16),
                        pltpu.VMEM((nbody, len(_RW_B128), 2 * c, LANES), BF16),
                        pltpu.VMEM((nbody, len(_RW_F64), c, LANES), F32),
                        pltpu.VMEM((nbody, LANES, LANES), F32)],
        compiler_params=_params("parallel", "parallel", "arbitrary"),
        name="rwkv7",
    )(*args)


def kernel(x, norm_gains, mix_w_in, dn_conv, dn_a_log, dn_dt_bias, dn_out_norm, hg_lb_logits, hg_out_norm,
           mix_w_out, rw_mu, rw_w_rkvz, rw_w0, rw_w1, rw_w2, rw_a0, rw_a1, rw_a2, rw_v0, rw_v1, rw_v2,
           rw_k_k, rw_k_a, rw_r_k, rw_ln_w, rw_ln_b, rw_w_out, final_norm):
    b, t, d = x.shape
    m = b * t
    depth = norm_gains.shape[0]
    dn_heads = dn_a_log.shape[1]
    dn_width = dn_conv.shape[2] // 3
    hg_width = hg_lb_logits.shape[1]
    hg_heads = hg_width // LANES
    assert dn_width == dn_heads * LANES and hg_out_norm.shape[1] == LANES and rw_r_k.shape[2] == RW_HEAD
    dn_ba = 4 * dn_width
    hg_q = dn_ba + 2 * dn_heads
    assert 2 * dn_heads <= LANES

    h = x.reshape(m, d)
    v_first = None
    for layer in range(depth):
        gain = norm_gains[layer]
        if layer % 2 == 0:
            e = layer // 2
            w_in = mix_w_in[e]
            w_main = jnp.concatenate([w_in[:, :dn_ba], w_in[:, hg_q:]], axis=1).astype(BF16)
            w_ba = jnp.pad(w_in[:, dn_ba:hg_q], ((0, 0), (0, LANES - 2 * dn_heads))).astype(BF16)
            hn = _rmsnorm(h, gain, BF16)[None]
            p = _matmul(hn, w_main[None])[0].reshape(b, t, -1)
            ba = _matmul(hn, w_ba[None])[0].reshape(b, t, LANES)
            o_a = _deltanet(p, ba, dn_conv[e], dn_a_log[e], dn_dt_bias[e], dn_out_norm[e], dn_heads)
            o_b = _hgrn2(p, dn_ba // LANES, hg_lb_logits, hg_out_norm[e], hg_heads, e)
            o = jnp.concatenate([o_a, o_b], axis=-1).reshape(1, m, -1)
            h = _matmul(o, mix_w_out[e].astype(BF16)[None], residual=h)[0]
        else:
            o = layer // 2
            mu = rw_mu[o][jnp.array([0, 2, 3, 5, 1, 4])]
            mixes = _rms_mix(h.reshape(b, t, d), gain, mu).reshape(6, m, d)
            rkvz = _matmul(mixes, rw_w_rkvz[o].astype(BF16), groups=4).reshape(4, b, t, d)
            wl = _lora(mixes, 4, rw_w1[o], rw_w2[o], True).reshape(b, t, d)
            al = _lora(mixes, 5, rw_a1[o], rw_a2[o], False).reshape(b, t, d)
            vl = None
            v0 = jnp.zeros((d,), F32)
            if v_first is None:
                v_first = rkvz
            else:
                vl = _lora(mixes, 2, rw_v1[o - 1], rw_v2[o - 1], False).reshape(b, t, d)
                v0 = rw_v0[o - 1]
            prm = jnp.stack([rw_w0[o], rw_a0[o], rw_k_k[o], rw_k_a[o], rw_r_k[o].reshape(d),
                             rw_ln_w[o], rw_ln_b[o], v0]).astype(F32)
            y = _rwkv7(rkvz, wl, al, vl, v_first, prm)
            h = _matmul(y.reshape(1, m, d), rw_w_out[o].astype(BF16)[None], residual=h)[0]
    return _rmsnorm(h, final_norm, x.dtype).reshape(b, t, d)
```

```python
import functools
import math

import jax
import jax.numpy as jnp
from jax import lax
from jax.experimental import pallas as pl
from jax.experimental.pallas import tpu as pltpu

F32 = jnp.float32
BF16 = jnp.bfloat16

NORM_EPS = 1e-6
GN_EPS = 64e-5
L2_EPS = 1e-6
CONV_TAPS = 4

LANES = 128
SUBLANES = 8
V7X_VMEM_LIMIT_BYTES = 56 * 1024 * 1024

DN_CHUNK = 64
RW_CHUNK = 64
RW_HEAD = 64
EXP_NEG_HALF = math.exp(-0.5)


def _params(*semantics):
    return pltpu.CompilerParams(dimension_semantics=semantics, vmem_limit_bytes=V7X_VMEM_LIMIT_BYTES)


def _tile(dim, pref):
    t = min(dim, pref)
    assert dim % t == 0, (dim, pref)
    return t


def _dot(a, b):
    return jnp.dot(a, b, preferred_element_type=F32)


def _dot_nt(a, b):
    return lax.dot_general(a, b, (((1,), (1,)), ((), ())), preferred_element_type=F32)


def _bdot_nt(a, b):
    return _dot_nt(a.astype(BF16), b.astype(BF16))


def _dot01(m01, x):
    x1 = x.astype(BF16)
    x2 = (x - x1.astype(F32)).astype(BF16)
    return _dot(m01, x1) + _dot(m01, x2)


def _silu(x):
    return x * jax.nn.sigmoid(x)


def _softplus(x):
    return jnp.maximum(x, 0.0) + jnp.log1p(jnp.exp(-jnp.abs(x)))


def _iota2(shape, dim):
    return lax.broadcasted_iota(jnp.int32, shape, dim)


def _shift_rows(x, halo, j):
    full = pltpu.roll(x, j, axis=0)
    hfix = pltpu.roll(halo, j, axis=0)
    top = jnp.where(_iota2(hfix.shape, 0) < j, hfix, full[0:SUBLANES])
    return jnp.concatenate([top, full[SUBLANES:]], axis=0)


def _rms_kernel(x_ref, g_ref, o_ref):
    x = x_ref[...]
    y = x * lax.rsqrt(jnp.mean(x * x, axis=-1, keepdims=True) + NORM_EPS)
    o_ref[...] = (y * g_ref[...]).astype(o_ref.dtype)


def _rmsnorm(x, gain, out_dtype):
    m, d = x.shape
    tm = _tile(m, 512)
    return pl.pallas_call(
        _rms_kernel,
        grid=(m // tm,),
        in_specs=[pl.BlockSpec((tm, d), lambda i: (i, 0)), pl.BlockSpec((1, d), lambda i: (0, 0))],
        out_specs=pl.BlockSpec((tm, d), lambda i: (i, 0)),
        out_shape=jax.ShapeDtypeStruct((m, d), out_dtype),
        compiler_params=_params("parallel"),
        name="rmsnorm",
    )(x, gain.reshape(1, d))


def _rms_mix_kernel(x_ref, halo_ref, g_ref, mu_ref, o_ref):
    t = pl.program_id(1)
    g = g_ref[...]

    def norm(x):
        return x * lax.rsqrt(jnp.mean(x * x, axis=-1, keepdims=True) + NORM_EPS) * g

    hn = norm(x_ref[0])
    hp = jnp.where(t > 0, norm(halo_ref[0]), 0.0)
    xx = _shift_rows(hn, hp, 1) - hn
    for i in range(o_ref.shape[0]):
        o_ref[i, 0] = (hn + xx * mu_ref[i:i + 1, :]).astype(o_ref.dtype)


def _rms_mix(h, gain, mu):
    b, t, d = h.shape
    tt = _tile(t, 256)
    nmix = mu.shape[0]
    return pl.pallas_call(
        _rms_mix_kernel,
        grid=(b, t // tt),
        in_specs=[
            pl.BlockSpec((1, tt, d), lambda i, j: (i, j, 0)),
            pl.BlockSpec((1, SUBLANES, d), lambda i, j: (i, jnp.maximum(j * (tt // SUBLANES) - 1, 0), 0)),
            pl.BlockSpec((1, d), lambda i, j: (0, 0)),
            pl.BlockSpec((nmix, d), lambda i, j: (0, 0)),
        ],
        out_specs=pl.BlockSpec((nmix, 1, tt, d), lambda i, j: (0, i, j, 0)),
        out_shape=jax.ShapeDtypeStruct((nmix, b, t, d), BF16),
        compiler_params=_params("parallel", "parallel"),
        name="rms_mix",
    )(h, h, gain.reshape(1, d), mu)


def _mm_kernel(x_ref, w_ref, o_ref):
    o_ref[0] = _dot(x_ref[0], w_ref[0]).astype(o_ref.dtype)


def _mm_res_kernel(x_ref, w_ref, r_ref, o_ref):
    o_ref[0] = (r_ref[...] + _dot(x_ref[0], w_ref[0])).astype(o_ref.dtype)


def _matmul(x, w, out_dtype=F32, residual=None, groups=None):
    g = w.shape[0] if groups is None else groups
    _, m, k = x.shape
    n = w.shape[2]
    tm = _tile(m, 1024)
    tn = _tile(n, 1024)
    in_specs = [pl.BlockSpec((1, tm, k), lambda a, i, j: (a, i, 0)),
                pl.BlockSpec((1, k, tn), lambda a, i, j: (a, 0, j))]
    args = [x, w]
    body = _mm_kernel
    if residual is not None:
        assert g == 1
        in_specs.append(pl.BlockSpec((tm, tn), lambda a, i, j: (i, j)))
        args.append(residual)
        body = _mm_res_kernel
    return pl.pallas_call(
        body,
        grid=(g, m // tm, n // tn),
        in_specs=in_specs,
        out_specs=pl.BlockSpec((1, tm, tn), lambda a, i, j: (a, i, j)),
        out_shape=jax.ShapeDtypeStruct((g, m, n), out_dtype),
        compiler_params=_params("parallel", "parallel", "parallel"),
        name="matmul",
    )(*args)


def _lora_kernel(x_ref, a_ref, b_ref, o_ref, *, use_tanh):
    mid = _dot(x_ref[0], a_ref[...])
    if use_tanh:
        mid = jnp.tanh(mid)
    o_ref[...] = _dot(mid.astype(BF16), b_ref[...])


def _lora(xs, sel, a, b, use_tanh):
    _, m, k = xs.shape
    rank = a.shape[1]
    n = b.shape[1]
    rpad = -(-rank // LANES) * LANES
    a_p = jnp.pad(a, ((0, 0), (0, rpad - rank))).astype(BF16)
    b_p = jnp.pad(b, ((0, rpad - rank), (0, 0))).astype(BF16)
    tm = _tile(m, 512)
    return pl.pallas_call(
        functools.partial(_lora_kernel, use_tanh=use_tanh),
        grid=(m // tm,),
        in_specs=[pl.BlockSpec((1, tm, k), lambda i: (sel, i, 0)),
                  pl.BlockSpec((k, rpad), lambda i: (0, 0)),
                  pl.BlockSpec((rpad, n), lambda i: (0, 0))],
        out_specs=pl.BlockSpec((tm, n), lambda i: (i, 0)),
        out_shape=jax.ShapeDtypeStruct((m, n), F32),
        compiler_params=_params("parallel"),
        name="lora",
    )(xs, a_p, b_p)


_DN_F128 = dict(q=0, k=1, vb=2, bc=3, g=4, u0=5, o=6)
_DN_F64 = dict(p=0, dmat=1)


def _dn_kernel(q_ref, k_ref, v_ref, qh_ref, kh_ref, vh_ref, z_ref, ba_ref, sel_ref, hp_ref, cw_ref, on_ref,
               o_ref, s_ref, sb_s, f128, wq_s, ub_s, nc_s, f64, rhs_s, akd_s, cd_s):
    t = pl.program_id(2)
    tc = q_ref.shape[1]
    heads = s_ref.shape[0]
    c = DN_CHUNK
    qf, qp = _DN_F128, _DN_F64

    @pl.when(t == 0)
    def _():
        s_ref[...] = jnp.zeros_like(s_ref)
        sb_s[...] = jnp.zeros_like(sb_s)

    row = _iota2((c, c), 0)
    col = _iota2((c, c), 1)
    causal = row >= col
    strict = row > col
    strict_f = strict.astype(F32)
    eye = (row == col).astype(F32)
    tril = causal.astype(BF16)
    onorm = on_ref[...]

    bodies = [(ci, hh) for ci in range(tc // c) for hh in range(heads)]

    def where(ci, hh):
        return ci * heads + hh, slice(ci * c, (ci + 1) * c), slice(hh * LANES, (hh + 1) * LANES)

    def conv_silu(x_ref, h_ref, ci, rs, ls, w):
        x = x_ref[0, rs, ls]
        if ci == 0:
            halo = jnp.where(t > 0, h_ref[0, :, ls], 0.0)
        else:
            halo = x_ref[0, ci * c - SUBLANES:ci * c, ls]
        y = x * w[CONV_TAPS - 1:CONV_TAPS, :]
        for j in range(1, CONV_TAPS):
            y = y + _shift_rows(x, halo, j) * w[CONV_TAPS - 1 - j:CONV_TAPS - j, :]
        return _silu(y)

    def l2n(x):
        return x * lax.rsqrt(jnp.sum(x * x, axis=-1, keepdims=True) + L2_EPS)

    for ci, hh in bodies:
        i, rs, ls = where(ci, hh)
        cw = cw_ref[hh]
        sel = sel_ref[hh]
        hp = hp_ref[hh]
        ba = ba_ref[0, rs, :]
        qc = l2n(conv_silu(q_ref, qh_ref, ci, rs, ls, cw[0:CONV_TAPS])) * (LANES ** -0.5)
        kc = l2n(conv_silu(k_ref, kh_ref, ci, rs, ls, cw[CONV_TAPS:2 * CONV_TAPS]))
        vc = conv_silu(v_ref, vh_ref, ci, rs, ls, cw[2 * CONV_TAPS:3 * CONV_TAPS])
        beta = jax.nn.sigmoid(jnp.sum(ba * sel[0:1], axis=-1, keepdims=True))
        alpha_pre = jnp.sum(ba * sel[1:2], axis=-1, keepdims=True)
        bc = jnp.broadcast_to(beta, (c, LANES))
        lac = -jnp.exp(hp[0:1]) * _softplus(jnp.broadcast_to(alpha_pre, (c, LANES)) + hp[1:2])
        f128[i, qf["q"]] = qc
        f128[i, qf["k"]] = kc
        f128[i, qf["vb"]] = vc * bc
        f128[i, qf["bc"]] = bc
        f128[i, qf["g"]] = _dot01(tril, lac)
        f64[i, qp["dmat"]] = _dot01(tril, lac[:, 0:c] * strict_f)

    for ci, hh in bodies:
        i, rs, ls = where(ci, hh)
        qc = f128[i, qf["q"]]
        kc = f128[i, qf["k"]]
        bc = f128[i, qf["bc"]]
        g = f128[i, qf["g"]]
        decay = jnp.where(causal, jnp.exp(jnp.where(causal, f64[i, qp["dmat"]], 0.0)), 0.0)
        kb = kc.astype(BF16)
        qkk = _dot_nt(jnp.concatenate([qc.astype(BF16), kb], axis=0), kb)
        lower = jnp.where(strict, bc[:, 0:c] * qkk[c:] * decay, 0.0)
        gamma = jnp.exp(g)
        glast = g[c - 1:c, :]
        akd_s[i, 0:c] = jnp.where(causal, qkk[0:c] * decay, 0.0).astype(BF16)
        nc_s[i] = (-lower).astype(BF16)
        f64[i, qp["p"]] = eye - lower
        rhs_s[i] = jnp.concatenate([f128[i, qf["vb"]], kc * (bc * gamma)], axis=1).astype(BF16)
        wq_s[i, c:] = (qc * gamma).astype(BF16)
        akd_s[i, c:] = (kc * jnp.exp(glast - g)).T.astype(BF16)
        cd_s[i] = jnp.broadcast_to(jnp.exp(glast), (SUBLANES, LANES))

    for _ in range(5):
        for ci, hh in bodies:
            i, _, _ = where(ci, hh)
            nc = nc_s[i]
            nc_s[i] = _dot(nc, nc).astype(BF16)
        for ci, hh in bodies:
            i, _, _ = where(ci, hh)
            p = f64[i, qp["p"]]
            f64[i, qp["p"]] = p + _dot(p.astype(BF16), nc_s[i])

    for ci, hh in bodies:
        i, _, _ = where(ci, hh)
        sol = _dot(f64[i, qp["p"]].astype(BF16), rhs_s[i])
        f128[i, qf["u0"]] = sol[:, 0:LANES]
        wq_s[i, 0:c] = sol[:, LANES:].astype(BF16)

    for ci in range(tc // c):
        for hh in range(heads):
            i, _, _ = where(ci, hh)
            wq = _dot(wq_s[i], sb_s[hh])
            ub_s[i] = (f128[i, qf["u0"]] - wq[0:c]).astype(BF16)
            f128[i, qf["o"]] = wq[c:]
        for hh in range(heads):
            i, _, _ = where(ci, hh)
            ou = _dot(akd_s[i], ub_s[i])
            f128[i, qf["o"]] = f128[i, qf["o"]] + ou[0:c]
            s_new = cd_s[i][0:1] * s_ref[hh] + ou[c:]
            s_ref[hh] = s_new
            sb_s[hh] = s_new.astype(BF16)

    for ci, hh in bodies:
        i, rs, ls = where(ci, hh)
        o = f128[i, qf["o"]]
        on = o * lax.rsqrt(jnp.mean(o * o, axis=-1, keepdims=True) + NORM_EPS) * onorm
        o_ref[0, rs, ls] = (on * _silu(z_ref[0, rs, ls])).astype(o_ref.dtype)


def _deltanet(p, ba, conv_w, a_log, dt_bias, out_norm, nheads):
    b, t, _ = p.shape
    tc = _tile(t, 256)
    c = DN_CHUNK
    assert tc % c == 0
    hb = tc // SUBLANES
    nh = nheads
    hg = _tile(nh, 8)
    ng = nh // hg
    wd = hg * LANES
    nbody = (tc // c) * hg
    cw = conv_w.reshape(CONV_TAPS, 3, nh, LANES).transpose(2, 1, 0, 3).reshape(nh, 3 * CONV_TAPS, LANES)
    lane = jnp.arange(LANES)[None, None, :]
    head = jnp.arange(nh)[:, None, None]
    sel = jnp.concatenate([(lane == head), (lane == head + nh)], axis=1).astype(F32)
    hp = jnp.broadcast_to(jnp.stack([a_log, dt_bias], axis=1)[:, :, None], (nh, 2, LANES)).astype(F32)

    def main(sec):
        return pl.BlockSpec((1, tc, wd), lambda i, h, j: (i, j, sec * ng + h))

    def halo(sec):
        return pl.BlockSpec((1, SUBLANES, wd), lambda i, h, j: (i, jnp.maximum(j * hb - 1, 0), sec * ng + h))

    per_head = lambda rows: pl.BlockSpec((hg, rows, LANES), lambda i, h, j: (h, 0, 0))
    return pl.pallas_call(
        _dn_kernel,
        grid=(b, ng, t // tc),
        in_specs=[main(0), main(1), main(2), halo(0), halo(1), halo(2), main(3),
                  pl.BlockSpec((1, tc, LANES), lambda i, h, j: (i, j, 0)),
                  per_head(2), per_head(2), per_head(3 * CONV_TAPS),
                  pl.BlockSpec((1, LANES), lambda i, h, j: (0, 0))],
        out_specs=pl.BlockSpec((1, tc, wd), lambda i, h, j: (i, j, h)),
        out_shape=jax.ShapeDtypeStruct((b, t, nh * LANES), BF16),
        scratch_shapes=[pltpu.VMEM((hg, LANES, LANES), F32),
                        pltpu.VMEM((hg, LANES, LANES), BF16),
                        pltpu.VMEM((nbody, len(_DN_F128), c, LANES), F32),
                        pltpu.VMEM((nbody, 2 * c, LANES), BF16),
                        pltpu.VMEM((nbody, c, LANES), BF16),
                        pltpu.VMEM((nbody, c, c), BF16),
                        pltpu.VMEM((nbody, len(_DN_F64), c, c), F32),
                        pltpu.VMEM((nbody, c, 2 * LANES), BF16),
                        pltpu.VMEM((nbody, c + LANES, c), BF16),
                        pltpu.VMEM((nbody, SUBLANES, LANES), F32)],
        compiler_params=_params("parallel", "parallel", "arbitrary"),
        name="deltanet",
    )(p, p, p, p, p, p, p, ba, sel, hp, cw, out_norm.reshape(1, LANES))


HG_LEVELS = 6
HG_BLOCK = 64


def _hg_tables():
    import numpy as np
    c = HG_BLOCK
    t = np.arange(c)[:, None]
    i = np.arange(c)[None, :]
    rows = [i <= t, i > t]
    q_rows, k_rows = [], []
    level = np.full((c, c), -1, np.int32)
    level[np.arange(c), np.arange(c)] = HG_LEVELS
    for li in range(HG_LEVELS):
        m = c >> (li + 1)
        pos = t % (2 * m)
        upper = pos >= m
        ref = t - pos + m - 1
        q_rows.append(upper & (i > ref) & (i <= t))
        k_rows.append((~upper) & (i > t) & (i <= ref))
        same = (t // (2 * m)) == (i // (2 * m))
        level[same & upper & ((i % (2 * m)) < m)] = li
    table = np.concatenate(rows + q_rows + k_rows, axis=0).astype(np.float32)
    return table, level


def _hg_kernel(q_ref, f_ref, i_ref, z_ref, lb_ref, on_ref, tab_ref, lvl_ref, o_ref,
               s_ref, sb_s, qt_s, kt_s, qb_s, vb_s, klt_s, att_s, ecol_s, inc_s, o_s, *, layer):
    t = pl.program_id(2)
    tc = q_ref.shape[1]
    heads = s_ref.shape[0]
    c = HG_BLOCK
    nl = HG_LEVELS

    @pl.when(t == 0)
    def _():
        s_ref[...] = jnp.zeros_like(s_ref)
        sb_s[...] = jnp.zeros_like(sb_s)

    table = tab_ref[...]
    level = lvl_ref[...]
    onorm = on_ref[...]
    bodies = [(ci, hh) for ci in range(tc // c) for hh in range(heads)]

    def where(ci, hh):
        return ci * heads + hh, slice(ci * c, (ci + 1) * c), slice(hh * LANES, (hh + 1) * LANES)

    for ci, hh in bodies:
        i, rs, ls = where(ci, hh)
        logits = lb_ref[:, ls]
        e = jnp.exp(logits - jnp.max(logits, axis=0, keepdims=True))
        share = e / jnp.sum(e, axis=0, keepdims=True)
        lbound = jnp.sum(share[0:layer + 1], axis=0, keepdims=True) - share[0:1]
        q = _silu(q_ref[0, rs, ls]) * (LANES ** -0.5)
        forget = lbound + (1.0 - lbound) * jax.nn.sigmoid(f_ref[0, rs, ls])
        k = 1.0 - forget
        d = _dot01(table, jnp.log(forget))
        b = d[0:c]
        qb_s[i] = (q * jnp.exp(b)).astype(BF16)
        klt_s[i] = (k * jnp.exp(d[c:2 * c])).T.astype(BF16)
        ecol_s[i] = jnp.broadcast_to(jnp.exp(b[c - 1:c]), (LANES, LANES)).T
        for li in range(nl):
            qt_s[i, li] = (q * jnp.exp(d[(2 + li) * c:(3 + li) * c])).astype(BF16)
            kt_s[i, li] = (k * jnp.exp(d[(2 + nl + li) * c:(3 + nl + li) * c])).astype(BF16)
        qt_s[i, nl] = q.astype(BF16)
        kt_s[i, nl] = k.astype(BF16)
        vb_s[i] = i_ref[0, rs, ls].astype(BF16)

    for ci, hh in bodies:
        i, _, _ = where(ci, hh)
        att = jnp.zeros((c, c), F32)
        for li in range(nl + 1):
            att = att + jnp.where(level == li, _dot_nt(qt_s[i, li], kt_s[i, li]), 0.0)
        att_s[i] = att.astype(BF16)
        inc_s[i] = _dot(klt_s[i], vb_s[i])

    for ci, hh in bodies:
        i, _, _ = where(ci, hh)
        o_s[i] = _dot(att_s[i], vb_s[i])

    for ci, hh in bodies:
        i, _, _ = where(ci, hh)
        o_s[i] = o_s[i] + _dot(qb_s[i], sb_s[hh])
        s_new = ecol_s[i] * s_ref[hh] + inc_s[i]
        s_ref[hh] = s_new
        sb_s[hh] = s_new.astype(BF16)

    for ci, hh in bodies:
        i, rs, ls = where(ci, hh)
        o = o_s[i]
        on = o * lax.rsqrt(jnp.mean(o * o, axis=-1, keepdims=True) + NORM_EPS) * onorm
        o_ref[0, rs, ls] = (on * _silu(z_ref[0, rs, ls])).astype(o_ref.dtype)


def _hgrn2(p, col0, lb_logits, out_norm, nheads, layer):
    b, t, _ = p.shape
    tc = _tile(t, 256)
    c = HG_BLOCK
    assert tc % c == 0
    nh = nheads
    hg = _tile(nh, 8)
    ng = nh // hg
    wd = hg * LANES
    assert col0 % hg == 0
    cg = col0 // hg
    nbody = (tc // c) * hg
    table, level = _hg_tables()
    nrow = table.shape[0]

    def main(sec):
        return pl.BlockSpec((1, tc, wd), lambda i, h, j: (i, j, cg + sec * ng + h))

    nl = lb_logits.shape[0]
    const = lambda shape: pl.BlockSpec(shape, lambda i, h, j: (0, 0))
    return pl.pallas_call(
        functools.partial(_hg_kernel, layer=layer),
        grid=(b, ng, t // tc),
        in_specs=[main(0), main(1), main(2), main(3),
                  pl.BlockSpec((nl, wd), lambda i, h, j: (0, h)),
                  const((1, LANES)), const((nrow, c)), const((c, c))],
        out_specs=pl.BlockSpec((1, tc, wd), lambda i, h, j: (i, j, h)),
        out_shape=jax.ShapeDtypeStruct((b, t, nh * LANES), BF16),
        scratch_shapes=[pltpu.VMEM((hg, LANES, LANES), F32),
                        pltpu.VMEM((hg, LANES, LANES), BF16),
                        pltpu.VMEM((nbody, HG_LEVELS + 1, c, LANES), BF16),
                        pltpu.VMEM((nbody, HG_LEVELS + 1, c, LANES), BF16),
                        pltpu.VMEM((nbody, c, LANES), BF16),
                        pltpu.VMEM((nbody, c, LANES), BF16),
                        pltpu.VMEM((nbody, LANES, c), BF16),
                        pltpu.VMEM((nbody, c, c), BF16),
                        pltpu.VMEM((nbody, LANES, LANES), F32),
                        pltpu.VMEM((nbody, LANES, LANES), F32),
                        pltpu.VMEM((nbody, c, LANES), F32)],
        compiler_params=_params("parallel", "parallel", "arbitrary"),
        name="hgrn2",
    )(p, p, p, p, lb_logits, out_norm.reshape(1, LANES), jnp.asarray(table, BF16), jnp.asarray(level))


_RW_B64 = dict(aak=0, vb=1, arb=2, ark=3, sab=4)
_RW_B128 = dict(nb=0, vbd=1, bkt=2, atbd=3, avbd=4)
_RW_F64 = dict(p=0, uc=1, g=2, bv=3, lw=4, kkr=5, k2=6, ag=7, v=8, ssq=9, y=10, d=11)


def _rw_kernel(*refs, has_vres):
    if has_vres:
        (r_ref, k_ref, v_ref, z_ref, wl_ref, al_ref, vl_ref, vf_ref, prm_ref,
         o_ref, h_ref, hb_s, s64, s128, wr_s, np_s, f64, f128) = refs
    else:
        (r_ref, k_ref, v_ref, z_ref, wl_ref, al_ref, prm_ref,
         o_ref, h_ref, hb_s, s64, s128, wr_s, np_s, f64, f128) = refs
    t = pl.program_id(2)
    tc = o_ref.shape[1]
    pairs = h_ref.shape[0]
    c = RW_CHUNK
    nchunk = tc // c
    b64, b128, q64 = _RW_B64, _RW_B128, _RW_F64

    @pl.when(t == 0)
    def _():
        h_ref[...] = jnp.zeros_like(h_ref)
        hb_s[...] = jnp.zeros_like(hb_s)

    lane1 = _iota2((1, LANES), 1)
    m0 = (lane1 < RW_HEAD).astype(F32)
    m1 = 1.0 - m0
    same_head = (_iota2((LANES, LANES), 0) < RW_HEAD) == (_iota2((LANES, LANES), 1) < RW_HEAD)
    bones = same_head.astype(BF16)
    bdmask = same_head.astype(F32)
    row = _iota2((c, LANES), 0)
    scol = jnp.bitwise_and(_iota2((c, LANES), 1), RW_HEAD - 1)
    incl = row >= scol
    strict = row > scol
    eye_cat = (row == scol).astype(F32)
    tril = (_iota2((c, c), 0) >= _iota2((c, c), 1)).astype(BF16)

    def bd(x):
        return jnp.concatenate([x * m0, x * m1], axis=0)

    def gsum(x):
        return _dot(x.astype(BF16), bones)

    bodies = [(ci, pp) for ci in range(nchunk) for pp in range(pairs)]

    def where(ci, pp):
        return ci * pairs + pp, slice(ci * c, (ci + 1) * c), slice(pp * LANES, (pp + 1) * LANES)

    for ci, pp in bodies:
        i, rs, ls = where(ci, pp)
        prm = prm_ref[:, ls]
        w0, a0, k_k, k_a, r_k, v0 = prm[0:1], prm[1:2], prm[2:3], prm[3:4], prm[4:5], prm[7:8]
        r = r_ref[0, 0, rs, ls]
        k = k_ref[0, 0, rs, ls]
        v = v_ref[0, 0, rs, ls]
        lw = -EXP_NEG_HALF * jax.nn.sigmoid(w0 + wl_ref[0, rs, ls])
        ag = jax.nn.sigmoid(a0 + al_ref[0, rs, ls])
        if has_vres:
            v = v + (vf_ref[0, 0, rs, ls] - v) * jax.nn.sigmoid(v0 + vl_ref[0, rs, ls])
        kkr = k * k_k
        k2 = k * (1.0 + (ag - 1.0) * k_a)
        sums = gsum(jnp.concatenate([kkr * kkr, r * k2 * r_k], axis=0))
        f64[i, q64["g"]] = _dot01(tril, lw)
        f64[i, q64["lw"]] = lw
        f64[i, q64["ag"]] = ag
        f64[i, q64["v"]] = v
        f64[i, q64["kkr"]] = kkr
        f64[i, q64["k2"]] = k2
        f64[i, q64["ssq"]] = sums[0:c]
        f64[i, q64["bv"]] = sums[c:] * v

    for ci, pp in bodies:
        i, rs, ls = where(ci, pp)
        g = f64[i, q64["g"]]
        lw = f64[i, q64["lw"]]
        v = f64[i, q64["v"]]
        kk = f64[i, q64["kkr"]] * lax.rsqrt(f64[i, q64["ssq"]] + L2_EPS)
        e_g = jnp.exp(g)
        e_gn = jnp.exp(-g)
        rt = r_ref[0, 0, rs, ls] * e_g
        at = -kk * jnp.exp(g - lw)
        kt = f64[i, q64["k2"]] * e_gn
        bt = kk * f64[i, q64["ag"]] * e_gn
        x = _bdot_nt(jnp.concatenate([at, rt], axis=0), jnp.concatenate([bd(kt), bd(bt)], axis=0))
        a_ab = jnp.where(strict, x[0:c, LANES:], 0.0)
        e_last = e_g[c - 1:c, :]
        s64[i, b64["aak"]] = jnp.where(strict, x[0:c, 0:LANES], 0.0).astype(BF16)
        s64[i, b64["ark"]] = jnp.where(incl, x[c:, 0:LANES], 0.0).astype(BF16)
        s64[i, b64["arb"]] = jnp.where(incl, x[c:, LANES:], 0.0).astype(BF16)
        np_s[i, 0:c] = a_ab.astype(BF16)
        np_s[i, c:] = (eye_cat + a_ab).astype(BF16)
        s128[i, b128["nb"]] = bd(a_ab).astype(BF16)
        f64[i, q64["p"]] = eye_cat + a_ab
        wr_s[i, c:] = rt.astype(BF16)
        s64[i, b64["vb"]] = v.astype(BF16)
        s128[i, b128["atbd"]] = bd(at).astype(BF16)
        s128[i, b128["vbd"]] = bd(v).astype(BF16)
        s128[i, b128["bkt"]] = jnp.concatenate([bt * e_last, kt * e_last], axis=0).T.astype(BF16)
        f128[i] = jnp.broadcast_to(e_last, (LANES, LANES)).T

    for ci, pp in bodies:
        i, _, _ = where(ci, pp)
        s128[i, b128["avbd"]] = bd(_dot(s64[i, b64["aak"]], s128[i, b128["vbd"]])).astype(BF16)

    for lvl in range(6):
        for ci, pp in bodies:
            i, _, _ = where(ci, pp)
            lhs = np_s[i, 0:c] if lvl == 0 else (np_s[i, c:] if lvl == 5 else np_s[i])
            out = _dot(lhs, s128[i, b128["nb"]])
            if lvl > 0:
                p = f64[i, q64["p"]] + out[-c:]
                f64[i, q64["p"]] = p
                np_s[i, c:] = p.astype(BF16)
            if lvl < 5:
                n2 = out[0:c]
                np_s[i, 0:c] = n2.astype(BF16)
                s128[i, b128["nb"]] = bd(n2).astype(BF16)

    for ci, pp in bodies:
        i, _, _ = where(ci, pp)
        rhs = jnp.concatenate([s128[i, b128["atbd"]], s128[i, b128["avbd"]]], axis=1)
        wu = _dot(f64[i, q64["p"]].astype(BF16), rhs)
        wr_s[i, 0:c] = wu[:, 0:LANES].astype(BF16)
        f64[i, q64["uc"]] = wu[:, LANES:]

    for ci in range(nchunk):
        for pp in range(pairs):
            i, _, _ = where(ci, pp)
            wr = _dot(wr_s[i], hb_s[pp])
            sa = wr[0:c] + f64[i, q64["uc"]]
            s64[i, b64["sab"]] = sa.astype(BF16)
            s128[i, b128["nb"]] = bd(sa).astype(BF16)
            f64[i, q64["y"]] = wr[c:]
        for pp in range(pairs):
            i, _, _ = where(ci, pp)
            f64[i, q64["y"]] = f64[i, q64["y"]] + _dot(
                jnp.concatenate([s64[i, b64["arb"]], s64[i, b64["ark"]]], axis=1),
                jnp.concatenate([s128[i, b128["nb"]], s128[i, b128["vbd"]]], axis=0))
            h_new = f128[i] * h_ref[pp] + bdmask * _dot(
                s128[i, b128["bkt"]], jnp.concatenate([s64[i, b64["sab"]], s64[i, b64["vb"]]], axis=0))
            h_ref[pp] = h_new
            hb_s[pp] = h_new.astype(BF16)

    for ci, pp in bodies:
        i, _, _ = where(ci, pp)
        y = f64[i, q64["y"]]
        f64[i, q64["d"]] = y - gsum(y) * (1.0 / RW_HEAD)
    for ci, pp in bodies:
        i, rs, ls = where(ci, pp)
        prm = prm_ref[:, ls]
        d = f64[i, q64["d"]]
        var = gsum(d * d) * (1.0 / RW_HEAD)
        yn = d * lax.rsqrt(var + GN_EPS) * prm[5:6] + prm[6:7]
        o_ref[0, rs, ls] = ((yn + f64[i, q64["bv"]]) * _silu(z_ref[0, 0, rs, ls])).astype(o_ref.dtype)


def _rwkv7(rkvz, wl, al, vl, v_first, prm):
    _, b, t, d = rkvz.shape
    tc = _tile(t, 256)
    has_vres = vl is not None
    pg = _tile(d // LANES, 8)
    wd = pg * LANES
    c = RW_CHUNK
    nbody = (tc // c) * pg

    def sec(s):
        return pl.BlockSpec((1, 1, tc, wd), lambda i, p, j: (s, i, j, p))

    flat = pl.BlockSpec((1, tc, wd), lambda i, p, j: (i, j, p))
    in_specs = [sec(0), sec(1), sec(2), sec(3), flat, flat]
    args = [rkvz, rkvz, rkvz, rkvz, wl, al]
    if has_vres:
        in_specs += [flat, sec(2)]
        args += [vl, v_first]
    in_specs.append(pl.BlockSpec((8, wd), lambda i, p, j: (0, p)))
    args.append(prm)
    return pl.pallas_call(
        functools.partial(_rw_kernel, has_vres=has_vres),
        grid=(b, d // wd, t // tc),
        in_specs=in_specs,
        out_specs=flat,
        out_shape=jax.ShapeDtypeStruct((b, t, d), BF16),
        scratch_shapes=[pltpu.VMEM((pg, LANES, LANES), F32),
                        pltpu.VMEM((pg, LANES, LANES), BF16),
                        pltpu.VMEM((nbody, len(_RW_B64), c, LANES), BF16),
                        pltpu.VMEM((nbody, len(_RW_B128), 2 * c, LANES), BF16),
                        pltpu.VMEM((nbody, 2 * c, LANES), BF16),
                        pltpu.VMEM((nbody, 2 * c, LANES), BF16),
                        pltpu.VMEM((nbody, len(_RW_F64), c, LANES), F32),
                        pltpu.VMEM((nbody, LANES, LANES), F32)],
        compiler_params=_params("parallel", "parallel", "arbitrary"),
        name="rwkv7",
    )(*args)


def kernel(x, norm_gains, mix_w_in, dn_conv, dn_a_log, dn_dt_bias, dn_out_norm, hg_lb_logits, hg_out_norm,
           mix_w_out, rw_mu, rw_w_rkvz, rw_w0, rw_w1, rw_w2, rw_a0, rw_a1, rw_a2, rw_v0, rw_v1, rw_v2,
           rw_k_k, rw_k_a, rw_r_k, rw_ln_w, rw_ln_b, rw_w_out, final_norm):
    b, t, d = x.shape
    m = b * t
    depth = norm_gains.shape[0]
    dn_heads = dn_a_log.shape[1]
    dn_width = dn_conv.shape[2] // 3
    hg_width = hg_lb_logits.shape[1]
    hg_heads = hg_width // LANES
    assert dn_width == dn_heads * LANES and hg_out_norm.shape[1] == LANES and rw_r_k.shape[2] == RW_HEAD
    dn_ba = 4 * dn_width
    hg_q = dn_ba + 2 * dn_heads
    assert 2 * dn_heads <= LANES

    h = x.reshape(m, d)
    v_first = None
    for layer in range(depth):
        gain = norm_gains[layer]
        if layer % 2 == 0:
            e = layer // 2
            w_in = mix_w_in[e]
            w_main = jnp.concatenate([w_in[:, :dn_ba], w_in[:, hg_q:]], axis=1).astype(BF16)
            w_ba = jnp.pad(w_in[:, dn_ba:hg_q], ((0, 0), (0, LANES - 2 * dn_heads))).astype(BF16)
            hn = _rmsnorm(h, gain, BF16)[None]
            p = _matmul(hn, w_main[None])[0].reshape(b, t, -1)
            ba = _matmul(hn, w_ba[None])[0].reshape(b, t, LANES)
            o_a = _deltanet(p, ba, dn_conv[e], dn_a_log[e], dn_dt_bias[e], dn_out_norm[e], dn_heads)
            o_b = _hgrn2(p, dn_ba // LANES, hg_lb_logits, hg_out_norm[e], hg_heads, e)
            o = jnp.concatenate([o_a, o_b], axis=-1).reshape(1, m, -1)
            h = _matmul(o, mix_w_out[e].astype(BF16)[None], residual=h)[0]
        else:
            o = layer // 2
            mu = rw_mu[o][jnp.array([0, 2, 3, 5, 1, 4])]
            mixes = _rms_mix(h.reshape(b, t, d), gain, mu).reshape(6, m, d)
            rkvz = _matmul(mixes, rw_w_rkvz[o].astype(BF16), groups=4).reshape(4, b, t, d)
            wl = _lora(mixes, 4, rw_w1[o], rw_w2[o], True).reshape(b, t, d)
            al = _lora(mixes, 5, rw_a1[o], rw_a2[o], False).reshape(b, t, d)
            vl = None
            v0 = jnp.zeros((d,), F32)
            if v_first is None:
                v_first = rkvz
            else:
                vl = _lora(mixes, 2, rw_v1[o - 1], rw_v2[o - 1], False).reshape(b, t, d)
                v0 = rw_v0[o - 1]
            prm = jnp.stack([rw_w0[o], rw_a0[o], rw_k_k[o], rw_k_a[o], rw_r_k[o].reshape(d),
                             rw_ln_w[o], rw_ln_b[o], v0]).astype(F32)
            y = _rwkv7(rkvz, wl, al, vl, v_first, prm)
            h = _matmul(y.reshape(1, m, d), rw_w_out[o].astype(BF16)[None], residual=h)[0]
    return _rmsnorm(h, final_norm, x.dtype).reshape(b, t, d)
```

```python
import functools
import math

import jax
import jax.numpy as jnp
from jax import lax
from jax.experimental import pallas as pl
from jax.experimental.pallas import tpu as pltpu

F32 = jnp.float32
BF16 = jnp.bfloat16

NORM_EPS = 1e-6
GN_EPS = 64e-5
L2_EPS = 1e-6
CONV_TAPS = 4

LANES = 128
SUBLANES = 8
V7X_VMEM_LIMIT_BYTES = 56 * 1024 * 1024

DN_CHUNK = 64
RW_CHUNK = 64
RW_HEAD = 64
EXP_NEG_HALF = math.exp(-0.5)
MIX_PROLOGUE_ROWS = 256
MIX_DOT_ROWS = 512


def _params(*semantics):
    return pltpu.CompilerParams(dimension_semantics=semantics, vmem_limit_bytes=V7X_VMEM_LIMIT_BYTES)


def _tile(dim, pref):
    t = min(dim, pref)
    assert dim % t == 0, (dim, pref)
    return t


def _dot(a, b):
    return jnp.dot(a, b, preferred_element_type=F32)


def _dot_nt(a, b):
    return lax.dot_general(a, b, (((1,), (1,)), ((), ())), preferred_element_type=F32)


def _bdot_nt(a, b):
    return _dot_nt(a.astype(BF16), b.astype(BF16))


def _dot01(m01, x):
    x1 = x.astype(BF16)
    x2 = (x - x1.astype(F32)).astype(BF16)
    return _dot(m01, x1) + _dot(m01, x2)


def _silu(x):
    return x * jax.nn.sigmoid(x)


def _softplus(x):
    return jnp.maximum(x, 0.0) + jnp.log1p(jnp.exp(-jnp.abs(x)))


def _iota2(shape, dim):
    return lax.broadcasted_iota(jnp.int32, shape, dim)


def _shift_rows(x, halo, j):
    full = pltpu.roll(x, j, axis=0)
    hfix = pltpu.roll(halo, j, axis=0)
    top = jnp.where(_iota2(hfix.shape, 0) < j, hfix, full[0:SUBLANES])
    return jnp.concatenate([top, full[SUBLANES:]], axis=0)


def _rms(x, g):
    return x * lax.rsqrt(jnp.mean(x * x, axis=-1, keepdims=True) + NORM_EPS) * g


def _norm_mm_kernel(x_ref, g_ref, w_ref, wn_ref, o_ref, on_ref, hn_s, *, n_main):
    j = pl.program_id(1)

    @pl.when(j == 0)
    def _():
        hn_s[...] = _rms(x_ref[...], g_ref[...]).astype(BF16)

    @pl.when(j < n_main)
    def _():
        o_ref[...] = _dot(hn_s[...], w_ref[...])

    @pl.when(j == n_main)
    def _():
        on_ref[...] = _dot(hn_s[...], wn_ref[...])


def _norm_matmul(h, gain, w, w_narrow):
    m, d = h.shape
    n = w.shape[1]
    tm = _tile(m, 1024)
    tn = _tile(n, 1024)
    n_main = n // tn
    last = n_main - 1
    return pl.pallas_call(
        functools.partial(_norm_mm_kernel, n_main=n_main),
        grid=(m // tm, n_main + 1),
        in_specs=[pl.BlockSpec((tm, d), lambda i, j: (i, 0)),
                  pl.BlockSpec((1, d), lambda i, j: (0, 0)),
                  pl.BlockSpec((d, tn), lambda i, j: (0, jnp.minimum(j, last))),
                  pl.BlockSpec((d, LANES), lambda i, j: (0, 0))],
        out_specs=[pl.BlockSpec((tm, tn), lambda i, j: (i, jnp.minimum(j, last))),
                   pl.BlockSpec((tm, LANES), lambda i, j: (i, 0))],
        out_shape=[jax.ShapeDtypeStruct((m, n), F32), jax.ShapeDtypeStruct((m, LANES), F32)],
        scratch_shapes=[pltpu.VMEM((tm, d), BF16)],
        compiler_params=_params("parallel", "arbitrary"),
        name="norm_matmul",
    )(h, gain.reshape(1, d), w, w_narrow)


def _mix_mm_kernel(x_ref, halo_ref, g_ref, mu_ref, w_ref, a_ref, o_ref, mid_ref, hn_s, xx_s, *,
                   n_main, tiles_per_group, tiles_per_seq, n_groups):
    i = pl.program_id(0)
    j = pl.program_id(1)
    tm = hn_s.shape[0]
    rc = min(tm, MIX_PROLOGUE_ROWS)
    mc = min(tm, MIX_DOT_ROWS)

    @pl.when(j == 0)
    def _():
        g = g_ref[...]
        first = lax.rem(i, tiles_per_seq) == 0
        for r in range(tm // rc):
            rows = slice(r * rc, (r + 1) * rc)
            hn = _rms(x_ref[0, rows, :], g)
            if r == 0:
                hp = jnp.where(first, 0.0, _rms(halo_ref[0], g))
            else:
                hp = _rms(x_ref[0, r * rc - SUBLANES:r * rc, :], g)
            hn_s[rows, :] = hn.astype(BF16)
            xx_s[rows, :] = (_shift_rows(hn, hp, 1) - hn).astype(BF16)

    def mix(rows, mu_row):
        return (hn_s[rows, :].astype(F32) + xx_s[rows, :].astype(F32) * mu_row).astype(BF16)

    @pl.when(j < n_main)
    def _():
        mu_row = mu_ref[pl.ds(lax.div(j, tiles_per_group), 1), :]
        for r in range(tm // mc):
            rows = slice(r * mc, (r + 1) * mc)
            o_ref[rows, :] = _dot(mix(rows, mu_row), w_ref[0])

    @pl.when(j == n_main)
    def _():
        for l in range(a_ref.shape[0]):
            mu_row = mu_ref[n_groups + l:n_groups + l + 1, :]
            for r in range(tm // mc):
                rows = slice(r * mc, (r + 1) * mc)
                mid_ref[rows, l * LANES:(l + 1) * LANES] = _dot(mix(rows, mu_row), a_ref[l])


def _mix_matmul(h, gain, mu, w, a_down):
    b, t, d = h.shape
    m = b * t
    g, _, n = w.shape
    nl = a_down.shape[0]
    tm = _tile(t, 1024)
    tn = _tile(n, 1024)
    tpg = n // tn
    n_main = g * tpg
    last = n_main - 1
    tps = t // tm
    hb = tm // SUBLANES

    def col(j):
        return jnp.minimum(j, last)

    return pl.pallas_call(
        functools.partial(_mix_mm_kernel, n_main=n_main, tiles_per_group=tpg, tiles_per_seq=tps, n_groups=g),
        grid=(m // tm, n_main + 1),
        in_specs=[pl.BlockSpec((1, tm, d), lambda i, j: (i // tps, i % tps, 0)),
                  pl.BlockSpec((1, SUBLANES, d), lambda i, j: (i // tps, jnp.maximum((i % tps) * hb - 1, 0), 0)),
                  pl.BlockSpec((1, d), lambda i, j: (0, 0)),
                  pl.BlockSpec((g + nl, d), lambda i, j: (0, 0)),
                  pl.BlockSpec((1, d, tn), lambda i, j: (col(j) // tpg, 0, col(j) % tpg)),
                  pl.BlockSpec((nl, d, LANES), lambda i, j: (0, 0, 0))],
        out_specs=[pl.BlockSpec((tm, tn), lambda i, j: (i, col(j))),
                   pl.BlockSpec((tm, nl * LANES), lambda i, j: (i, 0))],
        out_shape=[jax.ShapeDtypeStruct((m, g * n), F32), jax.ShapeDtypeStruct((m, nl * LANES), F32)],
        scratch_shapes=[pltpu.VMEM((tm, d), BF16), pltpu.VMEM((tm, d), BF16)],
        compiler_params=_params("parallel", "arbitrary"),
        name="mix_matmul",
    )(h, h, gain.reshape(1, d), mu, w, a_down)


def _lora_up_kernel(mid_ref, b_ref, o_ref, *, use_tanh):
    mid = mid_ref[...]
    if use_tanh:
        mid = jnp.tanh(mid)
    o_ref[...] = _dot(mid.astype(BF16), b_ref[...])


def _lora_up(mids, sel, b_up, use_tanh):
    m = mids.shape[0]
    rank, n = b_up.shape
    b_p = jnp.pad(b_up, ((0, LANES - rank), (0, 0))).astype(BF16)
    tm = _tile(m, 1024)
    return pl.pallas_call(
        functools.partial(_lora_up_kernel, use_tanh=use_tanh),
        grid=(m // tm,),
        in_specs=[pl.BlockSpec((tm, LANES), lambda i: (i, sel)),
                  pl.BlockSpec((LANES, n), lambda i: (0, 0))],
        out_specs=pl.BlockSpec((tm, n), lambda i: (i, 0)),
        out_shape=jax.ShapeDtypeStruct((m, n), F32),
        compiler_params=_params("parallel"),
        name="lora_up",
    )(mids, b_p)


def _out_mm_kernel(*refs, n_x):
    x_refs, w_ref, r_ref, o_ref = refs[:n_x], refs[n_x], refs[n_x + 1], refs[n_x + 2]
    acc = r_ref[...]
    k0 = 0
    for x_ref in x_refs:
        kx = x_ref.shape[1]
        acc = acc + _dot(x_ref[...], w_ref[k0:k0 + kx, :])
        k0 += kx
    o_ref[...] = acc


def _out_matmul(xs, w, residual):
    m = xs[0].shape[0]
    k, n = w.shape
    tm = _tile(m, 1024)
    tn = _tile(n, 1024)
    in_specs = [pl.BlockSpec((tm, x.shape[1]), lambda i, j: (i, 0)) for x in xs]
    in_specs += [pl.BlockSpec((k, tn), lambda i, j: (0, j)), pl.BlockSpec((tm, tn), lambda i, j: (i, j))]
    return pl.pallas_call(
        functools.partial(_out_mm_kernel, n_x=len(xs)),
        grid=(m // tm, n // tn),
        in_specs=in_specs,
        out_specs=pl.BlockSpec((tm, tn), lambda i, j: (i, j)),
        out_shape=jax.ShapeDtypeStruct((m, n), F32),
        compiler_params=_params("parallel", "parallel"),
        name="out_matmul",
    )(*xs, w, residual)


def _rms_kernel(x_ref, g_ref, o_ref):
    o_ref[...] = _rms(x_ref[...], g_ref[...]).astype(o_ref.dtype)


def _rmsnorm(x, gain, out_dtype):
    m, d = x.shape
    tm = _tile(m, 512)
    return pl.pallas_call(
        _rms_kernel,
        grid=(m // tm,),
        in_specs=[pl.BlockSpec((tm, d), lambda i: (i, 0)), pl.BlockSpec((1, d), lambda i: (0, 0))],
        out_specs=pl.BlockSpec((tm, d), lambda i: (i, 0)),
        out_shape=jax.ShapeDtypeStruct((m, d), out_dtype),
        compiler_params=_params("parallel"),
        name="rmsnorm",
    )(x, gain.reshape(1, d))


_DN_F128 = dict(q=0, k=1, vb=2, bc=3, g=4, u0=5, o=6)
_DN_F64 = dict(p=0, dmat=1)


def _dn_kernel(q_ref, k_ref, v_ref, qh_ref, kh_ref, vh_ref, z_ref, ba_ref, sel_ref, hp_ref, cw_ref, on_ref,
               o_ref, s_ref, sb_s, f128, wq_s, ub_s, nc_s, f64, rhs_s, akd_s, cd_s):
    t = pl.program_id(2)
    tc = q_ref.shape[1]
    heads = s_ref.shape[0]
    c = DN_CHUNK
    qf, qp = _DN_F128, _DN_F64

    @pl.when(t == 0)
    def _():
        s_ref[...] = jnp.zeros_like(s_ref)
        sb_s[...] = jnp.zeros_like(sb_s)

    row = _iota2((c, c), 0)
    col = _iota2((c, c), 1)
    causal = row >= col
    strict = row > col
    strict_f = strict.astype(F32)
    eye = (row == col).astype(F32)
    tril = causal.astype(BF16)
    onorm = on_ref[...]

    bodies = [(ci, hh) for ci in range(tc // c) for hh in range(heads)]

    def where(ci, hh):
        return ci * heads + hh, slice(ci * c, (ci + 1) * c), slice(hh * LANES, (hh + 1) * LANES)

    def conv_silu(x_ref, h_ref, ci, rs, ls, w):
        x = x_ref[0, rs, ls]
        if ci == 0:
            halo = jnp.where(t > 0, h_ref[0, :, ls], 0.0)
        else:
            halo = x_ref[0, ci * c - SUBLANES:ci * c, ls]
        y = x * w[CONV_TAPS - 1:CONV_TAPS, :]
        for j in range(1, CONV_TAPS):
            y = y + _shift_rows(x, halo, j) * w[CONV_TAPS - 1 - j:CONV_TAPS - j, :]
        return _silu(y)

    def l2n(x):
        return x * lax.rsqrt(jnp.sum(x * x, axis=-1, keepdims=True) + L2_EPS)

    for ci, hh in bodies:
        i, rs, ls = where(ci, hh)
        cw = cw_ref[hh]
        sel = sel_ref[hh]
        hp = hp_ref[hh]
        ba = ba_ref[0, rs, :]
        qc = l2n(conv_silu(q_ref, qh_ref, ci, rs, ls, cw[0:CONV_TAPS])) * (LANES ** -0.5)
        kc = l2n(conv_silu(k_ref, kh_ref, ci, rs, ls, cw[CONV_TAPS:2 * CONV_TAPS]))
        vc = conv_silu(v_ref, vh_ref, ci, rs, ls, cw[2 * CONV_TAPS:3 * CONV_TAPS])
        beta = jax.nn.sigmoid(jnp.sum(ba * sel[0:1], axis=-1, keepdims=True))
        alpha_pre = jnp.sum(ba * sel[1:2], axis=-1, keepdims=True)
        bc = jnp.broadcast_to(beta, (c, LANES))
        lac = -jnp.exp(hp[0:1]) * _softplus(jnp.broadcast_to(alpha_pre, (c, LANES)) + hp[1:2])
        f128[i, qf["q"]] = qc
        f128[i, qf["k"]] = kc
        f128[i, qf["vb"]] = vc * bc
        f128[i, qf["bc"]] = bc
        f128[i, qf["g"]] = _dot01(tril, lac)
        f64[i, qp["dmat"]] = _dot01(tril, lac[:, 0:c] * strict_f)

    for ci, hh in bodies:
        i, rs, ls = where(ci, hh)
        qc = f128[i, qf["q"]]
        kc = f128[i, qf["k"]]
        bc = f128[i, qf["bc"]]
        g = f128[i, qf["g"]]
        decay = jnp.where(causal, jnp.exp(jnp.where(causal, f64[i, qp["dmat"]], 0.0)), 0.0)
        kb = kc.astype(BF16)
        qkk = _dot_nt(jnp.concatenate([qc.astype(BF16), kb], axis=0), kb)
        lower = jnp.where(strict, bc[:, 0:c] * qkk[c:] * decay, 0.0)
        gamma = jnp.exp(g)
        glast = g[c - 1:c, :]
        akd_s[i, 0:c] = jnp.where(causal, qkk[0:c] * decay, 0.0).astype(BF16)
        nc_s[i] = (-lower).astype(BF16)
        f64[i, qp["p"]] = eye - lower
        rhs_s[i] = jnp.concatenate([f128[i, qf["vb"]], kc * (bc * gamma)], axis=1).astype(BF16)
        wq_s[i, c:] = (qc * gamma).astype(BF16)
        akd_s[i, c:] = (kc * jnp.exp(glast - g)).T.astype(BF16)
        cd_s[i] = jnp.broadcast_to(jnp.exp(glast), (SUBLANES, LANES))

    for _ in range(5):
        for ci, hh in bodies:
            i, _, _ = where(ci, hh)
            nc = nc_s[i]
            nc_s[i] = _dot(nc, nc).astype(BF16)
        for ci, hh in bodies:
            i, _, _ = where(ci, hh)
            p = f64[i, qp["p"]]
            f64[i, qp["p"]] = p + _dot(p.astype(BF16), nc_s[i])

    for ci, hh in bodies:
        i, _, _ = where(ci, hh)
        sol = _dot(f64[i, qp["p"]].astype(BF16), rhs_s[i])
        f128[i, qf["u0"]] = sol[:, 0:LANES]
        wq_s[i, 0:c] = sol[:, LANES:].astype(BF16)

    for ci in range(tc // c):
        for hh in range(heads):
            i, _, _ = where(ci, hh)
            wq = _dot(wq_s[i], sb_s[hh])
            ub_s[i] = (f128[i, qf["u0"]] - wq[0:c]).astype(BF16)
            f128[i, qf["o"]] = wq[c:]
        for hh in range(heads):
            i, _, _ = where(ci, hh)
            ou = _dot(akd_s[i], ub_s[i])
            f128[i, qf["o"]] = f128[i, qf["o"]] + ou[0:c]
            s_new = cd_s[i][0:1] * s_ref[hh] + ou[c:]
            s_ref[hh] = s_new
            sb_s[hh] = s_new.astype(BF16)

    for ci, hh in bodies:
        i, rs, ls = where(ci, hh)
        o = f128[i, qf["o"]]
        on = o * lax.rsqrt(jnp.mean(o * o, axis=-1, keepdims=True) + NORM_EPS) * onorm
        o_ref[0, rs, ls] = (on * _silu(z_ref[0, rs, ls])).astype(o_ref.dtype)


def _deltanet(p, ba, conv_w, a_log, dt_bias, out_norm, nheads):
    b, t, _ = p.shape
    tc = _tile(t, 256)
    c = DN_CHUNK
    assert tc % c == 0
    hb = tc // SUBLANES
    nh = nheads
    hg = _tile(nh, 8)
    ng = nh // hg
    wd = hg * LANES
    nbody = (tc // c) * hg
    cw = conv_w.reshape(CONV_TAPS, 3, nh, LANES).transpose(2, 1, 0, 3).reshape(nh, 3 * CONV_TAPS, LANES)
    lane = jnp.arange(LANES)[None, None, :]
    head = jnp.arange(nh)[:, None, None]
    sel = jnp.concatenate([(lane == head), (lane == head + nh)], axis=1).astype(F32)
    hp = jnp.broadcast_to(jnp.stack([a_log, dt_bias], axis=1)[:, :, None], (nh, 2, LANES)).astype(F32)

    def main(sec):
        return pl.BlockSpec((1, tc, wd), lambda i, h, j: (i, j, sec * ng + h))

    def halo(sec):
        return pl.BlockSpec((1, SUBLANES, wd), lambda i, h, j: (i, jnp.maximum(j * hb - 1, 0), sec * ng + h))

    per_head = lambda rows: pl.BlockSpec((hg, rows, LANES), lambda i, h, j: (h, 0, 0))
    return pl.pallas_call(
        _dn_kernel,
        grid=(b, ng, t // tc),
        in_specs=[main(0), main(1), main(2), halo(0), halo(1), halo(2), main(3),
                  pl.BlockSpec((1, tc, LANES), lambda i, h, j: (i, j, 0)),
                  per_head(2), per_head(2), per_head(3 * CONV_TAPS),
                  pl.BlockSpec((1, LANES), lambda i, h, j: (0, 0))],
        out_specs=pl.BlockSpec((1, tc, wd), lambda i, h, j: (i, j, h)),
        out_shape=jax.ShapeDtypeStruct((b, t, nh * LANES), BF16),
        scratch_shapes=[pltpu.VMEM((hg, LANES, LANES), F32),
                        pltpu.VMEM((hg, LANES, LANES), BF16),
                        pltpu.VMEM((nbody, len(_DN_F128), c, LANES), F32),
                        pltpu.VMEM((nbody, 2 * c, LANES), BF16),
                        pltpu.VMEM((nbody, c, LANES), BF16),
                        pltpu.VMEM((nbody, c, c), BF16),
                        pltpu.VMEM((nbody, len(_DN_F64), c, c), F32),
                        pltpu.VMEM((nbody, c, 2 * LANES), BF16),
                        pltpu.VMEM((nbody, c + LANES, c), BF16),
                        pltpu.VMEM((nbody, SUBLANES, LANES), F32)],
        compiler_params=_params("parallel", "parallel", "arbitrary"),
        name="deltanet",
    )(p, p, p, p, p, p, p, ba, sel, hp, cw, out_norm.reshape(1, LANES))


HG_LEVELS = 6
HG_BLOCK = 64


def _hg_tables():
    import numpy as np
    c = HG_BLOCK
    t = np.arange(c)[:, None]
    i = np.arange(c)[None, :]
    rows = [i <= t, i > t]
    q_rows, k_rows = [], []
    level = np.full((c, c), -1, np.int32)
    level[np.arange(c), np.arange(c)] = HG_LEVELS
    for li in range(HG_LEVELS):
        m = c >> (li + 1)
        pos = t % (2 * m)
        upper = pos >= m
        ref = t - pos + m - 1
        q_rows.append(upper & (i > ref) & (i <= t))
        k_rows.append((~upper) & (i > t) & (i <= ref))
        same = (t // (2 * m)) == (i // (2 * m))
        level[same & upper & ((i % (2 * m)) < m)] = li
    table = np.concatenate(rows + q_rows + k_rows, axis=0).astype(np.float32)
    return table, level


def _hg_kernel(q_ref, f_ref, i_ref, z_ref, lb_ref, on_ref, tab_ref, lvl_ref, o_ref,
               s_ref, sb_s, qt_s, kt_s, qb_s, vb_s, klt_s, att_s, ecol_s, inc_s, o_s, *, layer):
    t = pl.program_id(2)
    tc = q_ref.shape[1]
    heads = s_ref.shape[0]
    c = HG_BLOCK
    nl = HG_LEVELS

    @pl.when(t == 0)
    def _():
        s_ref[...] = jnp.zeros_like(s_ref)
        sb_s[...] = jnp.zeros_like(sb_s)

    table = tab_ref[...]
    level = lvl_ref[...]
    onorm = on_ref[...]
    bodies = [(ci, hh) for ci in range(tc // c) for hh in range(heads)]

    def where(ci, hh):
        return ci * heads + hh, slice(ci * c, (ci + 1) * c), slice(hh * LANES, (hh + 1) * LANES)

    for ci, hh in bodies:
        i, rs, ls = where(ci, hh)
        logits = lb_ref[:, ls]
        e = jnp.exp(logits - jnp.max(logits, axis=0, keepdims=True))
        share = e / jnp.sum(e, axis=0, keepdims=True)
        lbound = jnp.sum(share[0:layer + 1], axis=0, keepdims=True) - share[0:1]
        q = _silu(q_ref[0, rs, ls]) * (LANES ** -0.5)
        forget = lbound + (1.0 - lbound) * jax.nn.sigmoid(f_ref[0, rs, ls])
        k = 1.0 - forget
        d = _dot01(table, jnp.log(forget))
        b = d[0:c]
        qb_s[i] = (q * jnp.exp(b)).astype(BF16)
        klt_s[i] = (k * jnp.exp(d[c:2 * c])).T.astype(BF16)
        ecol_s[i] = jnp.broadcast_to(jnp.exp(b[c - 1:c]), (LANES, LANES)).T
        for li in range(nl):
            qt_s[i, li] = (q * jnp.exp(d[(2 + li) * c:(3 + li) * c])).astype(BF16)
            kt_s[i, li] = (k * jnp.exp(d[(2 + nl + li) * c:(3 + nl + li) * c])).astype(BF16)
        qt_s[i, nl] = q.astype(BF16)
        kt_s[i, nl] = k.astype(BF16)
        vb_s[i] = i_ref[0, rs, ls].astype(BF16)

    for ci, hh in bodies:
        i, _, _ = where(ci, hh)
        att = jnp.zeros((c, c), F32)
        for li in range(nl + 1):
            att = att + jnp.where(level == li, _dot_nt(qt_s[i, li], kt_s[i, li]), 0.0)
        att_s[i] = att.astype(BF16)
        inc_s[i] = _dot(klt_s[i], vb_s[i])

    for ci, hh in bodies:
        i, _, _ = where(ci, hh)
        o_s[i] = _dot(att_s[i], vb_s[i])

    for ci, hh in bodies:
        i, _, _ = where(ci, hh)
        o_s[i] = o_s[i] + _dot(qb_s[i], sb_s[hh])
        s_new = ecol_s[i] * s_ref[hh] + inc_s[i]
        s_ref[hh] = s_new
        sb_s[hh] = s_new.astype(BF16)

    for ci, hh in bodies:
        i, rs, ls = where(ci, hh)
        o = o_s[i]
        on = o * lax.rsqrt(jnp.mean(o * o, axis=-1, keepdims=True) + NORM_EPS) * onorm
        o_ref[0, rs, ls] = (on * _silu(z_ref[0, rs, ls])).astype(o_ref.dtype)


def _hgrn2(p, col0, lb_logits, out_norm, nheads, layer):
    b, t, _ = p.shape
    tc = _tile(t, 256)
    c = HG_BLOCK
    assert tc % c == 0
    nh = nheads
    hg = _tile(nh, 8)
    ng = nh // hg
    wd = hg * LANES
    assert col0 % hg == 0
    cg = col0 // hg
    nbody = (tc // c) * hg
    table, level = _hg_tables()
    nrow = table.shape[0]

    def main(sec):
        return pl.BlockSpec((1, tc, wd), lambda i, h, j: (i, j, cg + sec * ng + h))

    nl = lb_logits.shape[0]
    const = lambda shape: pl.BlockSpec(shape, lambda i, h, j: (0, 0))
    return pl.pallas_call(
        functools.partial(_hg_kernel, layer=layer),
        grid=(b, ng, t // tc),
        in_specs=[main(0), main(1), main(2), main(3),
                  pl.BlockSpec((nl, wd), lambda i, h, j: (0, h)),
                  const((1, LANES)), const((nrow, c)), const((c, c))],
        out_specs=pl.BlockSpec((1, tc, wd), lambda i, h, j: (i, j, h)),
        out_shape=jax.ShapeDtypeStruct((b, t, nh * LANES), BF16),
        scratch_shapes=[pltpu.VMEM((hg, LANES, LANES), F32),
                        pltpu.VMEM((hg, LANES, LANES), BF16),
                        pltpu.VMEM((nbody, HG_LEVELS + 1, c, LANES), BF16),
                        pltpu.VMEM((nbody, HG_LEVELS + 1, c, LANES), BF16),
                        pltpu.VMEM((nbody, c, LANES), BF16),
                        pltpu.VMEM((nbody, c, LANES), BF16),
                        pltpu.VMEM((nbody, LANES, c), BF16),
                        pltpu.VMEM((nbody, c, c), BF16),
                        pltpu.VMEM((nbody, LANES, LANES), F32),
                        pltpu.VMEM((nbody, LANES, LANES), F32),
                        pltpu.VMEM((nbody, c, LANES), F32)],
        compiler_params=_params("parallel", "parallel", "arbitrary"),
        name="hgrn2",
    )(p, p, p, p, lb_logits, out_norm.reshape(1, LANES), jnp.asarray(table, BF16), jnp.asarray(level))


_RW_B64 = dict(aak=0, vb=1, arb=2, ark=3, sab=4)
_RW_B128 = dict(nb=0, vbd=1, bkt=2, atbd=3, avbd=4)
_RW_F64 = dict(p=0, uc=1, g=2, bv=3, lw=4, kkr=5, k2=6, ag=7, v=8, ssq=9, y=10, d=11)


def _rw_kernel(*refs, has_vres):
    if has_vres:
        (r_ref, k_ref, v_ref, z_ref, wl_ref, al_ref, vl_ref, vf_ref, prm_ref,
         o_ref, h_ref, hb_s, s64, s128, wr_s, np_s, f64, f128) = refs
    else:
        (r_ref, k_ref, v_ref, z_ref, wl_ref, al_ref, prm_ref,
         o_ref, h_ref, hb_s, s64, s128, wr_s, np_s, f64, f128) = refs
    t = pl.program_id(2)
    tc = o_ref.shape[1]
    pairs = h_ref.shape[0]
    c = RW_CHUNK
    nchunk = tc // c
    b64, b128, q64 = _RW_B64, _RW_B128, _RW_F64

    @pl.when(t == 0)
    def _():
        h_ref[...] = jnp.zeros_like(h_ref)
        hb_s[...] = jnp.zeros_like(hb_s)

    lane1 = _iota2((1, LANES), 1)
    m0 = (lane1 < RW_HEAD).astype(F32)
    m1 = 1.0 - m0
    same_head = (_iota2((LANES, LANES), 0) < RW_HEAD) == (_iota2((LANES, LANES), 1) < RW_HEAD)
    bones = same_head.astype(BF16)
    bdmask = same_head.astype(F32)
    row = _iota2((c, LANES), 0)
    scol = jnp.bitwise_and(_iota2((c, LANES), 1), RW_HEAD - 1)
    incl = row >= scol
    strict = row > scol
    eye_cat = (row == scol).astype(F32)
    tril = (_iota2((c, c), 0) >= _iota2((c, c), 1)).astype(BF16)

    def bd(x):
        return jnp.concatenate([x * m0, x * m1], axis=0)

    def gsum(x):
        return _dot(x.astype(BF16), bones)

    bodies = [(ci, pp) for ci in range(nchunk) for pp in range(pairs)]

    def where(ci, pp):
        return ci * pairs + pp, slice(ci * c, (ci + 1) * c), slice(pp * LANES, (pp + 1) * LANES)

    for ci, pp in bodies:
        i, rs, ls = where(ci, pp)
        prm = prm_ref[:, ls]
        w0, a0, k_k, k_a, r_k, v0 = prm[0:1], prm[1:2], prm[2:3], prm[3:4], prm[4:5], prm[7:8]
        r = r_ref[0, rs, ls]
        k = k_ref[0, rs, ls]
        v = v_ref[0, rs, ls]
        lw = -EXP_NEG_HALF * jax.nn.sigmoid(w0 + wl_ref[0, rs, ls])
        ag = jax.nn.sigmoid(a0 + al_ref[0, rs, ls])
        if has_vres:
            v = v + (vf_ref[0, rs, ls] - v) * jax.nn.sigmoid(v0 + vl_ref[0, rs, ls])
        kkr = k * k_k
        k2 = k * (1.0 + (ag - 1.0) * k_a)
        sums = gsum(jnp.concatenate([kkr * kkr, r * k2 * r_k], axis=0))
        f64[i, q64["g"]] = _dot01(tril, lw)
        f64[i, q64["lw"]] = lw
        f64[i, q64["ag"]] = ag
        f64[i, q64["v"]] = v
        f64[i, q64["kkr"]] = kkr
        f64[i, q64["k2"]] = k2
        f64[i, q64["ssq"]] = sums[0:c]
        f64[i, q64["bv"]] = sums[c:] * v

    for ci, pp in bodies:
        i, rs, ls = where(ci, pp)
        g = f64[i, q64["g"]]
        lw = f64[i, q64["lw"]]
        v = f64[i, q64["v"]]
        kk = f64[i, q64["kkr"]] * lax.rsqrt(f64[i, q64["ssq"]] + L2_EPS)
        e_g = jnp.exp(g)
        e_gn = jnp.exp(-g)
        rt = r_ref[0, rs, ls] * e_g
        at = -kk * jnp.exp(g - lw)
        kt = f64[i, q64["k2"]] * e_gn
        bt = kk * f64[i, q64["ag"]] * e_gn
        x = _bdot_nt(jnp.concatenate([at, rt], axis=0), jnp.concatenate([bd(kt), bd(bt)], axis=0))
        a_ab = jnp.where(strict, x[0:c, LANES:], 0.0)
        e_last = e_g[c - 1:c, :]
        s64[i, b64["aak"]] = jnp.where(strict, x[0:c, 0:LANES], 0.0).astype(BF16)
        s64[i, b64["ark"]] = jnp.where(incl, x[c:, 0:LANES], 0.0).astype(BF16)
        s64[i, b64["arb"]] = jnp.where(incl, x[c:, LANES:], 0.0).astype(BF16)
        np_s[i, 0:c] = a_ab.astype(BF16)
        np_s[i, c:] = (eye_cat + a_ab).astype(BF16)
        s128[i, b128["nb"]] = bd(a_ab).astype(BF16)
        f64[i, q64["p"]] = eye_cat + a_ab
        wr_s[i, c:] = rt.astype(BF16)
        s64[i, b64["vb"]] = v.astype(BF16)
        s128[i, b128["atbd"]] = bd(at).astype(BF16)
        s128[i, b128["vbd"]] = bd(v).astype(BF16)
        s128[i, b128["bkt"]] = jnp.concatenate([bt * e_last, kt * e_last], axis=0).T.astype(BF16)
        f128[i] = jnp.broadcast_to(e_last, (LANES, LANES)).T

    for ci, pp in bodies:
        i, _, _ = where(ci, pp)
        s128[i, b128["avbd"]] = bd(_dot(s64[i, b64["aak"]], s128[i, b128["vbd"]])).astype(BF16)

    for lvl in range(6):
        for ci, pp in bodies:
            i, _, _ = where(ci, pp)
            lhs = np_s[i, 0:c] if lvl == 0 else (np_s[i, c:] if lvl == 5 else np_s[i])
            out = _dot(lhs, s128[i, b128["nb"]])
            if lvl > 0:
                p = f64[i, q64["p"]] + out[-c:]
                f64[i, q64["p"]] = p
                np_s[i, c:] = p.astype(BF16)
            if lvl < 5:
                n2 = out[0:c]
                np_s[i, 0:c] = n2.astype(BF16)
                s128[i, b128["nb"]] = bd(n2).astype(BF16)

    for ci, pp in bodies:
        i, _, _ = where(ci, pp)
        rhs = jnp.concatenate([s128[i, b128["atbd"]], s128[i, b128["avbd"]]], axis=1)
        wu = _dot(f64[i, q64["p"]].astype(BF16), rhs)
        wr_s[i, 0:c] = wu[:, 0:LANES].astype(BF16)
        f64[i, q64["uc"]] = wu[:, LANES:]

    for ci in range(nchunk):
        for pp in range(pairs):
            i, _, _ = where(ci, pp)
            wr = _dot(wr_s[i], hb_s[pp])
            sa = wr[0:c] + f64[i, q64["uc"]]
            s64[i, b64["sab"]] = sa.astype(BF16)
            s128[i, b128["nb"]] = bd(sa).astype(BF16)
            f64[i, q64["y"]] = wr[c:]
        for pp in range(pairs):
            i, _, _ = where(ci, pp)
            f64[i, q64["y"]] = f64[i, q64["y"]] + _dot(
                jnp.concatenate([s64[i, b64["arb"]], s64[i, b64["ark"]]], axis=1),
                jnp.concatenate([s128[i, b128["nb"]], s128[i, b128["vbd"]]], axis=0))
            h_new = f128[i] * h_ref[pp] + bdmask * _dot(
                s128[i, b128["bkt"]], jnp.concatenate([s64[i, b64["sab"]], s64[i, b64["vb"]]], axis=0))
            h_ref[pp] = h_new
            hb_s[pp] = h_new.astype(BF16)

    for ci, pp in bodies:
        i, _, _ = where(ci, pp)
        y = f64[i, q64["y"]]
        f64[i, q64["d"]] = y - gsum(y) * (1.0 / RW_HEAD)
    for ci, pp in bodies:
        i, rs, ls = where(ci, pp)
        prm = prm_ref[:, ls]
        d = f64[i, q64["d"]]
        var = gsum(d * d) * (1.0 / RW_HEAD)
        yn = d * lax.rsqrt(var + GN_EPS) * prm[5:6] + prm[6:7]
        o_ref[0, rs, ls] = ((yn + f64[i, q64["bv"]]) * _silu(z_ref[0, rs, ls])).astype(o_ref.dtype)


def _rwkv7(rkvz, wl, al, vl, v_first, prm):
    b, t, d4 = rkvz.shape
    d = d4 // 4
    tc = _tile(t, 256)
    has_vres = vl is not None
    pg = _tile(d // LANES, 8)
    wd = pg * LANES
    c = RW_CHUNK
    nbody = (tc // c) * pg
    ngrp = d // wd

    def sec(s):
        return pl.BlockSpec((1, tc, wd), lambda i, p, j: (i, j, s * ngrp + p))

    flat = pl.BlockSpec((1, tc, wd), lambda i, p, j: (i, j, p))
    in_specs = [sec(0), sec(1), sec(2), sec(3), flat, flat]
    args = [rkvz, rkvz, rkvz, rkvz, wl, al]
    if has_vres:
        in_specs += [flat, sec(2)]
        args += [vl, v_first]
    in_specs.append(pl.BlockSpec((8, wd), lambda i, p, j: (0, p)))
    args.append(prm)
    return pl.pallas_call(
        functools.partial(_rw_kernel, has_vres=has_vres),
        grid=(b, d // wd, t // tc),
        in_specs=in_specs,
        out_specs=flat,
        out_shape=jax.ShapeDtypeStruct((b, t, d), BF16),
        scratch_shapes=[pltpu.VMEM((pg, LANES, LANES), F32),
                        pltpu.VMEM((pg, LANES, LANES), BF16),
                        pltpu.VMEM((nbody, len(_RW_B64), c, LANES), BF16),
                        pltpu.VMEM((nbody, len(_RW_B128), 2 * c, LANES), BF16),
                        pltpu.VMEM((nbody, 2 * c, LANES), BF16),
                        pltpu.VMEM((nbody, 2 * c, LANES), BF16),
                        pltpu.VMEM((nbody, len(_RW_F64), c, LANES), F32),
                        pltpu.VMEM((nbody, LANES, LANES), F32)],
        compiler_params=_params("parallel", "parallel", "arbitrary"),
        name="rwkv7",
    )(*args)


def kernel(x, norm_gains, mix_w_in, dn_conv, dn_a_log, dn_dt_bias, dn_out_norm, hg_lb_logits, hg_out_norm,
           mix_w_out, rw_mu, rw_w_rkvz, rw_w0, rw_w1, rw_w2, rw_a0, rw_a1, rw_a2, rw_v0, rw_v1, rw_v2,
           rw_k_k, rw_k_a, rw_r_k, rw_ln_w, rw_ln_b, rw_w_out, final_norm):
    b, t, d = x.shape
    m = b * t
    depth = norm_gains.shape[0]
    dn_heads = dn_a_log.shape[1]
    dn_width = dn_conv.shape[2] // 3
    hg_width = hg_lb_logits.shape[1]
    hg_heads = hg_width // LANES
    assert dn_width == dn_heads * LANES and hg_out_norm.shape[1] == LANES and rw_r_k.shape[2] == RW_HEAD
    dn_ba = 4 * dn_width
    hg_q = dn_ba + 2 * dn_heads
    assert 2 * dn_heads <= LANES

    h = x.reshape(m, d)
    v_first = None
    for layer in range(depth):
        gain = norm_gains[layer]
        if layer % 2 == 0:
            e = layer // 2
            w_in = mix_w_in[e]
            w_main = jnp.concatenate([w_in[:, :dn_ba], w_in[:, hg_q:]], axis=1).astype(BF16)
            w_ba = jnp.pad(w_in[:, dn_ba:hg_q], ((0, 0), (0, LANES - 2 * dn_heads))).astype(BF16)
            p, ba = _norm_matmul(h, gain, w_main, w_ba)
            p = p.reshape(b, t, -1)
            ba = ba.reshape(b, t, LANES)
            o_a = _deltanet(p, ba, dn_conv[e], dn_a_log[e], dn_dt_bias[e], dn_out_norm[e], dn_heads)
            o_b = _hgrn2(p, dn_ba // LANES, hg_lb_logits, hg_out_norm[e], hg_heads, e)
            h = _out_matmul([o_a.reshape(m, -1), o_b.reshape(m, -1)], mix_w_out[e].astype(BF16), h)
        else:
            o = layer // 2
            has_vres = v_first is not None
            rows = [0, 2, 3, 5, 1, 4] + ([3] if has_vres else [])
            downs = [rw_w1[o], rw_a1[o]] + ([rw_v1[o - 1]] if has_vres else [])
            a_down = jnp.stack([jnp.pad(a, ((0, 0), (0, LANES - a.shape[1]))) for a in downs]).astype(BF16)
            rkvz, mids = _mix_matmul(h.reshape(b, t, d), gain, rw_mu[o][jnp.array(rows)],
                                     rw_w_rkvz[o].astype(BF16), a_down)
            rkvz = rkvz.reshape(b, t, 4 * d)
            wl = _lora_up(mids, 0, rw_w2[o], True).reshape(b, t, d)
            al = _lora_up(mids, 1, rw_a2[o], False).reshape(b, t, d)
            vl = None
            v0 = jnp.zeros((d,), F32)
            if has_vres:
                vl = _lora_up(mids, 2, rw_v2[o - 1], False).reshape(b, t, d)
                v0 = rw_v0[o - 1]
            else:
                v_first = rkvz
            prm = jnp.stack([rw_w0[o], rw_a0[o], rw_k_k[o], rw_k_a[o], rw_r_k[o].reshape(d),
                             rw_ln_w[o], rw_ln_b[o], v0]).astype(F32)
            y = _rwkv7(rkvz, wl, al, vl, v_first, prm)
            h = _out_matmul([y.reshape(m, d)], rw_w_out[o].astype(BF16), h)
    return _rmsnorm(h, final_norm, x.dtype).reshape(b, t, d)
```

```python
import functools
import math

import jax
import jax.numpy as jnp
from jax import lax
from jax.experimental import pallas as pl
from jax.experimental.pallas import tpu as pltpu

F32 = jnp.float32
BF16 = jnp.bfloat16

NORM_EPS = 1e-6
GN_EPS = 64e-5
L2_EPS = 1e-6
CONV_TAPS = 4

LANES = 128
SUBLANES = 8
V7X_VMEM_LIMIT_BYTES = 56 * 1024 * 1024

DN_CHUNK = 64
RW_CHUNK = 64
RW_HEAD = 64
EXP_NEG_HALF = math.exp(-0.5)
MIX_PROLOGUE_ROWS = 256
MIX_DOT_ROWS = 512


def _params(*semantics):
    return pltpu.CompilerParams(dimension_semantics=semantics, vmem_limit_bytes=V7X_VMEM_LIMIT_BYTES)


def _tile(dim, pref):
    t = min(dim, pref)
    assert dim % t == 0, (dim, pref)
    return t


def _dot(a, b):
    return jnp.dot(a, b, preferred_element_type=F32)


def _dot_nt(a, b):
    return lax.dot_general(a, b, (((1,), (1,)), ((), ())), preferred_element_type=F32)


def _bdot_nt(a, b):
    return _dot_nt(a.astype(BF16), b.astype(BF16))


def _dot01(m01, x):
    x1 = x.astype(BF16)
    x2 = (x - x1.astype(F32)).astype(BF16)
    return _dot(m01, x1) + _dot(m01, x2)


def _silu(x):
    return x * jax.nn.sigmoid(x)


def _softplus(x):
    return jnp.maximum(x, 0.0) + jnp.log1p(jnp.exp(-jnp.abs(x)))


def _iota2(shape, dim):
    return lax.broadcasted_iota(jnp.int32, shape, dim)


def _shift_rows(x, halo, j):
    full = pltpu.roll(x, j, axis=0)
    hfix = pltpu.roll(halo, j, axis=0)
    top = jnp.where(_iota2(hfix.shape, 0) < j, hfix, full[0:SUBLANES])
    return jnp.concatenate([top, full[SUBLANES:]], axis=0)


def _rms(x, g):
    return x * lax.rsqrt(jnp.mean(x * x, axis=-1, keepdims=True) + NORM_EPS) * g


def _norm_mm_kernel(x_ref, g_ref, w_ref, wn_ref, o_ref, on_ref, hn_s):
    j = pl.program_id(1)

    @pl.when(j == 0)
    def _():
        hn_s[...] = _rms(x_ref[...], g_ref[...]).astype(BF16)
        on_ref[...] = _dot(hn_s[...], wn_ref[...])

    @pl.when(j > 0)
    def _():
        o_ref[...] = _dot(hn_s[...], w_ref[...])


def _norm_matmul(h, gain, w, w_narrow):
    m, d = h.shape
    n = w.shape[1]
    tm = _tile(m, 1024)
    tn = _tile(n, 1024)
    n_main = n // tn

    def col(j):
        return jnp.maximum(j - 1, 0)

    return pl.pallas_call(
        _norm_mm_kernel,
        grid=(m // tm, n_main + 1),
        in_specs=[pl.BlockSpec((tm, d), lambda i, j: (i, 0)),
                  pl.BlockSpec((1, d), lambda i, j: (0, 0)),
                  pl.BlockSpec((d, tn), lambda i, j: (0, col(j))),
                  pl.BlockSpec((d, LANES), lambda i, j: (0, 0))],
        out_specs=[pl.BlockSpec((tm, tn), lambda i, j: (i, col(j))),
                   pl.BlockSpec((tm, LANES), lambda i, j: (i, 0))],
        out_shape=[jax.ShapeDtypeStruct((m, n), F32), jax.ShapeDtypeStruct((m, LANES), F32)],
        scratch_shapes=[pltpu.VMEM((tm, d), BF16)],
        compiler_params=_params("parallel", "arbitrary"),
        name="norm_matmul",
    )(h, gain.reshape(1, d), w, w_narrow)


def _mix_mm_kernel(x_ref, halo_ref, g_ref, mu_ref, w_ref, a_ref, o_ref, mid_ref, hn_s, xx_s, *,
                   tiles_per_group, tiles_per_seq, n_groups):
    i = pl.program_id(0)
    j = pl.program_id(1)
    tm = hn_s.shape[0]
    rc = min(tm, MIX_PROLOGUE_ROWS)
    mc = min(tm, MIX_DOT_ROWS)

    def mix(rows, mu_row):
        return (hn_s[rows, :].astype(F32) + xx_s[rows, :].astype(F32) * mu_row).astype(BF16)

    @pl.when(j == 0)
    def _():
        g = g_ref[...]
        first = lax.rem(i, tiles_per_seq) == 0
        for r in range(tm // rc):
            rows = slice(r * rc, (r + 1) * rc)
            hn = _rms(x_ref[0, rows, :], g)
            if r == 0:
                hp = jnp.where(first, 0.0, _rms(halo_ref[0], g))
            else:
                hp = _rms(x_ref[0, r * rc - SUBLANES:r * rc, :], g)
            hn_s[rows, :] = hn.astype(BF16)
            xx_s[rows, :] = (_shift_rows(hn, hp, 1) - hn).astype(BF16)

        for l in range(a_ref.shape[0]):
            mu_row = mu_ref[n_groups + l:n_groups + l + 1, :]
            for r in range(tm // mc):
                rows = slice(r * mc, (r + 1) * mc)
                mid_ref[rows, l * LANES:(l + 1) * LANES] = _dot(mix(rows, mu_row), a_ref[l])

    @pl.when(j > 0)
    def _():
        mu_row = mu_ref[pl.ds(lax.div(j - 1, tiles_per_group), 1), :]
        for r in range(tm // mc):
            rows = slice(r * mc, (r + 1) * mc)
            o_ref[rows, :] = _dot(mix(rows, mu_row), w_ref[0])


def _mix_matmul(h, gain, mu, w, a_down):
    b, t, d = h.shape
    m = b * t
    g, _, n = w.shape
    nl = a_down.shape[0]
    tm = _tile(t, 1024)
    tn = _tile(n, 1024)
    tpg = n // tn
    n_main = g * tpg
    tps = t // tm
    hb = tm // SUBLANES

    def col(j):
        return jnp.maximum(j - 1, 0)

    return pl.pallas_call(
        functools.partial(_mix_mm_kernel, tiles_per_group=tpg, tiles_per_seq=tps, n_groups=g),
        grid=(m // tm, n_main + 1),
        in_specs=[pl.BlockSpec((1, tm, d), lambda i, j: (i // tps, i % tps, 0)),
                  pl.BlockSpec((1, SUBLANES, d), lambda i, j: (i // tps, jnp.maximum((i % tps) * hb - 1, 0), 0)),
                  pl.BlockSpec((1, d), lambda i, j: (0, 0)),
                  pl.BlockSpec((g + nl, d), lambda i, j: (0, 0)),
                  pl.BlockSpec((1, d, tn), lambda i, j: (col(j) // tpg, 0, col(j) % tpg)),
                  pl.BlockSpec((nl, d, LANES), lambda i, j: (0, 0, 0))],
        out_specs=[pl.BlockSpec((tm, tn), lambda i, j: (i, col(j))),
                   pl.BlockSpec((tm, nl * LANES), lambda i, j: (i, 0))],
        out_shape=[jax.ShapeDtypeStruct((m, g * n), F32), jax.ShapeDtypeStruct((m, nl * LANES), F32)],
        scratch_shapes=[pltpu.VMEM((tm, d), BF16), pltpu.VMEM((tm, d), BF16)],
        compiler_params=_params("parallel", "arbitrary"),
        name="mix_matmul",
    )(h, h, gain.reshape(1, d), mu, w, a_down)


def _lora_up_kernel(mid_ref, b_ref, o_ref, *, use_tanh):
    mid = mid_ref[...]
    if use_tanh:
        mid = jnp.tanh(mid)
    o_ref[...] = _dot(mid.astype(BF16), b_ref[...])


def _lora_up(mids, sel, b_up, use_tanh):
    m = mids.shape[0]
    rank, n = b_up.shape
    b_p = jnp.pad(b_up, ((0, LANES - rank), (0, 0))).astype(BF16)
    tm = _tile(m, 1024)
    return pl.pallas_call(
        functools.partial(_lora_up_kernel, use_tanh=use_tanh),
        grid=(m // tm,),
        in_specs=[pl.BlockSpec((tm, LANES), lambda i: (i, sel)),
                  pl.BlockSpec((LANES, n), lambda i: (0, 0))],
        out_specs=pl.BlockSpec((tm, n), lambda i: (i, 0)),
        out_shape=jax.ShapeDtypeStruct((m, n), F32),
        compiler_params=_params("parallel"),
        name="lora_up",
    )(mids, b_p)


def _out_mm_kernel(*refs, n_x):
    x_refs, w_ref, r_ref, o_ref = refs[:n_x], refs[n_x], refs[n_x + 1], refs[n_x + 2]
    acc = r_ref[...]
    k0 = 0
    for x_ref in x_refs:
        kx = x_ref.shape[1]
        acc = acc + _dot(x_ref[...], w_ref[k0:k0 + kx, :])
        k0 += kx
    o_ref[...] = acc


def _out_matmul(xs, w, residual):
    m = xs[0].shape[0]
    k, n = w.shape
    tm = _tile(m, 512)
    tn = _tile(n, 2048)
    in_specs = [pl.BlockSpec((tm, x.shape[1]), lambda i, j: (i, 0)) for x in xs]
    in_specs += [pl.BlockSpec((k, tn), lambda i, j: (0, j)), pl.BlockSpec((tm, tn), lambda i, j: (i, j))]
    return pl.pallas_call(
        functools.partial(_out_mm_kernel, n_x=len(xs)),
        grid=(m // tm, n // tn),
        in_specs=in_specs,
        out_specs=pl.BlockSpec((tm, tn), lambda i, j: (i, j)),
        out_shape=jax.ShapeDtypeStruct((m, n), F32),
        compiler_params=_params("parallel", "parallel"),
        name="out_matmul",
    )(*xs, w, residual)


def _rms_kernel(x_ref, g_ref, o_ref):
    o_ref[...] = _rms(x_ref[...], g_ref[...]).astype(o_ref.dtype)


def _rmsnorm(x, gain, out_dtype):
    m, d = x.shape
    tm = _tile(m, 512)
    return pl.pallas_call(
        _rms_kernel,
        grid=(m // tm,),
        in_specs=[pl.BlockSpec((tm, d), lambda i: (i, 0)), pl.BlockSpec((1, d), lambda i: (0, 0))],
        out_specs=pl.BlockSpec((tm, d), lambda i: (i, 0)),
        out_shape=jax.ShapeDtypeStruct((m, d), out_dtype),
        compiler_params=_params("parallel"),
        name="rmsnorm",
    )(x, gain.reshape(1, d))


_DN_F128 = dict(q=0, k=1, vb=2, bc=3, g=4, u0=5, o=6)
_DN_F64 = dict(p=0, dmat=1)


def _dn_kernel(q_ref, k_ref, v_ref, qh_ref, kh_ref, vh_ref, z_ref, ba_ref, sel_ref, hp_ref, cw_ref, on_ref,
               o_ref, s_ref, sb_s, f128, wq_s, ub_s, nc_s, f64, rhs_s, akd_s, cd_s):
    t = pl.program_id(2)
    tc = q_ref.shape[1]
    heads = s_ref.shape[0]
    c = DN_CHUNK
    qf, qp = _DN_F128, _DN_F64

    @pl.when(t == 0)
    def _():
        s_ref[...] = jnp.zeros_like(s_ref)
        sb_s[...] = jnp.zeros_like(sb_s)

    row = _iota2((c, c), 0)
    col = _iota2((c, c), 1)
    causal = row >= col
    strict = row > col
    strict_f = strict.astype(F32)
    eye = (row == col).astype(F32)
    tril = causal.astype(BF16)
    onorm = on_ref[...]

    bodies = [(ci, hh) for ci in range(tc // c) for hh in range(heads)]

    def where(ci, hh):
        return ci * heads + hh, slice(ci * c, (ci + 1) * c), slice(hh * LANES, (hh + 1) * LANES)

    def conv_silu(x_ref, h_ref, ci, rs, ls, w):
        x = x_ref[0, rs, ls]
        if ci == 0:
            halo = jnp.where(t > 0, h_ref[0, :, ls], 0.0)
        else:
            halo = x_ref[0, ci * c - SUBLANES:ci * c, ls]
        y = x * w[CONV_TAPS - 1:CONV_TAPS, :]
        for j in range(1, CONV_TAPS):
            y = y + _shift_rows(x, halo, j) * w[CONV_TAPS - 1 - j:CONV_TAPS - j, :]
        return _silu(y)

    def l2n(x):
        return x * lax.rsqrt(jnp.sum(x * x, axis=-1, keepdims=True) + L2_EPS)

    for ci, hh in bodies:
        i, rs, ls = where(ci, hh)
        cw = cw_ref[hh]
        sel = sel_ref[hh]
        hp = hp_ref[hh]
        ba = ba_ref[0, rs, :]
        qc = l2n(conv_silu(q_ref, qh_ref, ci, rs, ls, cw[0:CONV_TAPS])) * (LANES ** -0.5)
        kc = l2n(conv_silu(k_ref, kh_ref, ci, rs, ls, cw[CONV_TAPS:2 * CONV_TAPS]))
        vc = conv_silu(v_ref, vh_ref, ci, rs, ls, cw[2 * CONV_TAPS:3 * CONV_TAPS])
        beta = jax.nn.sigmoid(jnp.sum(ba * sel[0:1], axis=-1, keepdims=True))
        alpha_pre = jnp.sum(ba * sel[1:2], axis=-1, keepdims=True)
        bc = jnp.broadcast_to(beta, (c, LANES))
        lac = -jnp.exp(hp[0:1]) * _softplus(jnp.broadcast_to(alpha_pre, (c, LANES)) + hp[1:2])
        f128[i, qf["q"]] = qc
        f128[i, qf["k"]] = kc
        f128[i, qf["vb"]] = vc * bc
        f128[i, qf["bc"]] = bc
        f128[i, qf["g"]] = _dot01(tril, lac)
        f64[i, qp["dmat"]] = _dot01(tril, lac[:, 0:c] * strict_f)

    for ci, hh in bodies:
        i, rs, ls = where(ci, hh)
        qc = f128[i, qf["q"]]
        kc = f128[i, qf["k"]]
        bc = f128[i, qf["bc"]]
        g = f128[i, qf["g"]]
        decay = jnp.where(causal, jnp.exp(jnp.where(causal, f64[i, qp["dmat"]], 0.0)), 0.0)
        kb = kc.astype(BF16)
        qkk = _dot_nt(jnp.concatenate([qc.astype(BF16), kb], axis=0), kb)
        lower = jnp.where(strict, bc[:, 0:c] * qkk[c:] * decay, 0.0)
        gamma = jnp.exp(g)
        glast = g[c - 1:c, :]
        akd_s[i, 0:c] = jnp.where(causal, qkk[0:c] * decay, 0.0).astype(BF16)
        nc_s[i] = (-lower).astype(BF16)
        f64[i, qp["p"]] = eye - lower
        rhs_s[i] = jnp.concatenate([f128[i, qf["vb"]], kc * (bc * gamma)], axis=1).astype(BF16)
        wq_s[i, c:] = (qc * gamma).astype(BF16)
        akd_s[i, c:] = (kc * jnp.exp(glast - g)).T.astype(BF16)
        cd_s[i] = jnp.broadcast_to(jnp.exp(glast), (SUBLANES, LANES))

    for _ in range(5):
        for ci, hh in bodies:
            i, _, _ = where(ci, hh)
            nc = nc_s[i]
            nc_s[i] = _dot(nc, nc).astype(BF16)
        for ci, hh in bodies:
            i, _, _ = where(ci, hh)
            p = f64[i, qp["p"]]
            f64[i, qp["p"]] = p + _dot(p.astype(BF16), nc_s[i])

    for ci, hh in bodies:
        i, _, _ = where(ci, hh)
        sol = _dot(f64[i, qp["p"]].astype(BF16), rhs_s[i])
        f128[i, qf["u0"]] = sol[:, 0:LANES]
        wq_s[i, 0:c] = sol[:, LANES:].astype(BF16)

    for ci in range(tc // c):
        for hh in range(heads):
            i, _, _ = where(ci, hh)
            wq = _dot(wq_s[i], sb_s[hh])
            ub_s[i] = (f128[i, qf["u0"]] - wq[0:c]).astype(BF16)
            f128[i, qf["o"]] = wq[c:]
        for hh in range(heads):
            i, _, _ = where(ci, hh)
            ou = _dot(akd_s[i], ub_s[i])
            f128[i, qf["o"]] = f128[i, qf["o"]] + ou[0:c]
            s_new = cd_s[i][0:1] * s_ref[hh] + ou[c:]
            s_ref[hh] = s_new
            sb_s[hh] = s_new.astype(BF16)

    for ci, hh in bodies:
        i, rs, ls = where(ci, hh)
        o = f128[i, qf["o"]]
        on = o * lax.rsqrt(jnp.mean(o * o, axis=-1, keepdims=True) + NORM_EPS) * onorm
        o_ref[0, rs, ls] = (on * _silu(z_ref[0, rs, ls])).astype(o_ref.dtype)


def _deltanet(p, ba, conv_w, a_log, dt_bias, out_norm, nheads):
    b, t, _ = p.shape
    tc = _tile(t, 256)
    c = DN_CHUNK
    assert tc % c == 0
    hb = tc // SUBLANES
    nh = nheads
    hg = _tile(nh, 8)
    ng = nh // hg
    wd = hg * LANES
    nbody = (tc // c) * hg
    cw = conv_w.reshape(CONV_TAPS, 3, nh, LANES).transpose(2, 1, 0, 3).reshape(nh, 3 * CONV_TAPS, LANES)
    lane = jnp.arange(LANES)[None, None, :]
    head = jnp.arange(nh)[:, None, None]
    sel = jnp.concatenate([(lane == head), (lane == head + nh)], axis=1).astype(F32)
    hp = jnp.broadcast_to(jnp.stack([a_log, dt_bias], axis=1)[:, :, None], (nh, 2, LANES)).astype(F32)

    def main(sec):
        return pl.BlockSpec((1, tc, wd), lambda i, h, j: (i, j, sec * ng + h))

    def halo(sec):
        return pl.BlockSpec((1, SUBLANES, wd), lambda i, h, j: (i, jnp.maximum(j * hb - 1, 0), sec * ng + h))

    per_head = lambda rows: pl.BlockSpec((hg, rows, LANES), lambda i, h, j: (h, 0, 0))
    return pl.pallas_call(
        _dn_kernel,
        grid=(b, ng, t // tc),
        in_specs=[main(0), main(1), main(2), halo(0), halo(1), halo(2), main(3),
                  pl.BlockSpec((1, tc, LANES), lambda i, h, j: (i, j, 0)),
                  per_head(2), per_head(2), per_head(3 * CONV_TAPS),
                  pl.BlockSpec((1, LANES), lambda i, h, j: (0, 0))],
        out_specs=pl.BlockSpec((1, tc, wd), lambda i, h, j: (i, j, h)),
        out_shape=jax.ShapeDtypeStruct((b, t, nh * LANES), BF16),
        scratch_shapes=[pltpu.VMEM((hg, LANES, LANES), F32),
                        pltpu.VMEM((hg, LANES, LANES), BF16),
                        pltpu.VMEM((nbody, len(_DN_F128), c, LANES), F32),
                        pltpu.VMEM((nbody, 2 * c, LANES), BF16),
                        pltpu.VMEM((nbody, c, LANES), BF16),
                        pltpu.VMEM((nbody, c, c), BF16),
                        pltpu.VMEM((nbody, len(_DN_F64), c, c), F32),
                        pltpu.VMEM((nbody, c, 2 * LANES), BF16),
                        pltpu.VMEM((nbody, c + LANES, c), BF16),
                        pltpu.VMEM((nbody, SUBLANES, LANES), F32)],
        compiler_params=_params("parallel", "parallel", "arbitrary"),
        name="deltanet",
    )(p, p, p, p, p, p, p, ba, sel, hp, cw, out_norm.reshape(1, LANES))


HG_LEVELS = 6
HG_BLOCK = 64


def _hg_tables():
    import numpy as np
    c = HG_BLOCK
    t = np.arange(c)[:, None]
    i = np.arange(c)[None, :]
    rows = [i <= t, i > t]
    q_rows, k_rows = [], []
    level = np.full((c, c), -1, np.int32)
    level[np.arange(c), np.arange(c)] = HG_LEVELS
    for li in range(HG_LEVELS):
        m = c >> (li + 1)
        pos = t % (2 * m)
        upper = pos >= m
        ref = t - pos + m - 1
        q_rows.append(upper & (i > ref) & (i <= t))
        k_rows.append((~upper) & (i > t) & (i <= ref))
        same = (t // (2 * m)) == (i // (2 * m))
        level[same & upper & ((i % (2 * m)) < m)] = li
    table = np.concatenate(rows + q_rows + k_rows, axis=0).astype(np.float32)
    return table, level


def _hg_kernel(q_ref, f_ref, i_ref, z_ref, lb_ref, on_ref, tab_ref, lvl_ref, o_ref,
               s_ref, sb_s, qt_s, kt_s, qb_s, vb_s, klt_s, att_s, ecol_s, inc_s, o_s, *, layer):
    t = pl.program_id(2)
    tc = q_ref.shape[1]
    heads = s_ref.shape[0]
    c = HG_BLOCK
    nl = HG_LEVELS

    @pl.when(t == 0)
    def _():
        s_ref[...] = jnp.zeros_like(s_ref)
        sb_s[...] = jnp.zeros_like(sb_s)

    table = tab_ref[...]
    level = lvl_ref[...]
    onorm = on_ref[...]
    bodies = [(ci, hh) for ci in range(tc // c) for hh in range(heads)]

    def where(ci, hh):
        return ci * heads + hh, slice(ci * c, (ci + 1) * c), slice(hh * LANES, (hh + 1) * LANES)

    for ci, hh in bodies:
        i, rs, ls = where(ci, hh)
        logits = lb_ref[:, ls]
        e = jnp.exp(logits - jnp.max(logits, axis=0, keepdims=True))
        share = e / jnp.sum(e, axis=0, keepdims=True)
        lbound = jnp.sum(share[0:layer + 1], axis=0, keepdims=True) - share[0:1]
        q = _silu(q_ref[0, rs, ls]) * (LANES ** -0.5)
        forget = lbound + (1.0 - lbound) * jax.nn.sigmoid(f_ref[0, rs, ls])
        k = 1.0 - forget
        d = _dot(table, jnp.log(forget).astype(BF16))
        b = d[0:c]
        qb_s[i] = (q * jnp.exp(b)).astype(BF16)
        klt_s[i] = (k * jnp.exp(d[c:2 * c])).T.astype(BF16)
        ecol_s[i] = jnp.broadcast_to(jnp.exp(b[c - 1:c]), (LANES, LANES)).T
        for li in range(nl):
            qt_s[i, li] = (q * jnp.exp(d[(2 + li) * c:(3 + li) * c])).astype(BF16)
            kt_s[i, li] = (k * jnp.exp(d[(2 + nl + li) * c:(3 + nl + li) * c])).astype(BF16)
        qt_s[i, nl] = q.astype(BF16)
        kt_s[i, nl] = k.astype(BF16)
        vb_s[i] = i_ref[0, rs, ls].astype(BF16)

    for ci, hh in bodies:
        i, _, _ = where(ci, hh)
        att = jnp.zeros((c, c), F32)
        for li in range(nl + 1):
            att = att + jnp.where(level == li, _dot_nt(qt_s[i, li], kt_s[i, li]), 0.0)
        att_s[i] = att.astype(BF16)
        inc_s[i] = _dot(klt_s[i], vb_s[i])

    for ci, hh in bodies:
        i, _, _ = where(ci, hh)
        o_s[i] = _dot(att_s[i], vb_s[i])

    for ci, hh in bodies:
        i, _, _ = where(ci, hh)
        o_s[i] = o_s[i] + _dot(qb_s[i], sb_s[hh])
        s_new = ecol_s[i] * s_ref[hh] + inc_s[i]
        s_ref[hh] = s_new
        sb_s[hh] = s_new.astype(BF16)

    for ci, hh in bodies:
        i, rs, ls = where(ci, hh)
        o = o_s[i]
        on = o * lax.rsqrt(jnp.mean(o * o, axis=-1, keepdims=True) + NORM_EPS) * onorm
        o_ref[0, rs, ls] = (on * _silu(z_ref[0, rs, ls])).astype(o_ref.dtype)


def _hgrn2(p, col0, lb_logits, out_norm, nheads, layer):
    b, t, _ = p.shape
    tc = _tile(t, 256)
    c = HG_BLOCK
    assert tc % c == 0
    nh = nheads
    hg = _tile(nh, 8)
    ng = nh // hg
    wd = hg * LANES
    assert col0 % hg == 0
    cg = col0 // hg
    nbody = (tc // c) * hg
    table, level = _hg_tables()
    nrow = table.shape[0]

    def main(sec):
        return pl.BlockSpec((1, tc, wd), lambda i, h, j: (i, j, cg + sec * ng + h))

    nl = lb_logits.shape[0]
    const = lambda shape: pl.BlockSpec(shape, lambda i, h, j: (0, 0))
    return pl.pallas_call(
        functools.partial(_hg_kernel, layer=layer),
        grid=(b, ng, t // tc),
        in_specs=[main(0), main(1), main(2), main(3),
                  pl.BlockSpec((nl, wd), lambda i, h, j: (0, h)),
                  const((1, LANES)), const((nrow, c)), const((c, c))],
        out_specs=pl.BlockSpec((1, tc, wd), lambda i, h, j: (i, j, h)),
        out_shape=jax.ShapeDtypeStruct((b, t, nh * LANES), BF16),
        scratch_shapes=[pltpu.VMEM((hg, LANES, LANES), F32),
                        pltpu.VMEM((hg, LANES, LANES), BF16),
                        pltpu.VMEM((nbody, HG_LEVELS + 1, c, LANES), BF16),
                        pltpu.VMEM((nbody, HG_LEVELS + 1, c, LANES), BF16),
                        pltpu.VMEM((nbody, c, LANES), BF16),
                        pltpu.VMEM((nbody, c, LANES), BF16),
                        pltpu.VMEM((nbody, LANES, c), BF16),
                        pltpu.VMEM((nbody, c, c), BF16),
                        pltpu.VMEM((nbody, LANES, LANES), F32),
                        pltpu.VMEM((nbody, LANES, LANES), F32),
                        pltpu.VMEM((nbody, c, LANES), F32)],
        compiler_params=_params("parallel", "parallel", "arbitrary"),
        name="hgrn2",
    )(p, p, p, p, lb_logits, out_norm.reshape(1, LANES), jnp.asarray(table, BF16), jnp.asarray(level))


_RW_B64 = dict(aak=0, vb=1, arb=2, ark=3, sab=4)
_RW_B128 = dict(nb=0, vbd=1, bkt=2, atbd=3, avbd=4)
_RW_F64 = dict(p=0, uc=1, g=2, bv=3, lw=4, kkr=5, k2=6, ag=7, v=8, ssq=9, y=10, d=11)


def _rw_kernel(*refs, has_vres):
    if has_vres:
        (r_ref, k_ref, v_ref, z_ref, wl_ref, al_ref, vl_ref, vf_ref, prm_ref,
         o_ref, h_ref, hb_s, s64, s128, wr_s, np_s, f64, f128) = refs
    else:
        (r_ref, k_ref, v_ref, z_ref, wl_ref, al_ref, prm_ref,
         o_ref, h_ref, hb_s, s64, s128, wr_s, np_s, f64, f128) = refs
    t = pl.program_id(2)
    tc = o_ref.shape[1]
    pairs = h_ref.shape[0]
    c = RW_CHUNK
    nchunk = tc // c
    b64, b128, q64 = _RW_B64, _RW_B128, _RW_F64

    @pl.when(t == 0)
    def _():
        h_ref[...] = jnp.zeros_like(h_ref)
        hb_s[...] = jnp.zeros_like(hb_s)

    lane1 = _iota2((1, LANES), 1)
    head0 = lane1 < RW_HEAD
    m0 = head0.astype(F32)
    m1 = 1.0 - m0
    same_head = (_iota2((LANES, LANES), 0) < RW_HEAD) == (_iota2((LANES, LANES), 1) < RW_HEAD)
    bdmask = same_head.astype(F32)
    row = _iota2((c, LANES), 0)
    scol = jnp.bitwise_and(_iota2((c, LANES), 1), RW_HEAD - 1)
    incl = row >= scol
    strict = row > scol
    eye_cat = (row == scol).astype(F32)
    tril = (_iota2((c, c), 0) >= _iota2((c, c), 1)).astype(BF16)

    def bd(x):
        return jnp.concatenate([x * m0, x * m1], axis=0)

    def gsum(x):
        s0 = jnp.sum(jnp.where(head0, x, 0.0), axis=-1, keepdims=True)
        s1 = jnp.sum(x, axis=-1, keepdims=True) - s0
        return jnp.where(head0, s0, s1)

    bodies = [(ci, pp) for ci in range(nchunk) for pp in range(pairs)]

    def where(ci, pp):
        return ci * pairs + pp, slice(ci * c, (ci + 1) * c), slice(pp * LANES, (pp + 1) * LANES)

    for ci, pp in bodies:
        i, rs, ls = where(ci, pp)
        prm = prm_ref[:, ls]
        w0, a0, k_k, k_a, r_k, v0 = prm[0:1], prm[1:2], prm[2:3], prm[3:4], prm[4:5], prm[7:8]
        r = r_ref[0, rs, ls]
        k = k_ref[0, rs, ls]
        v = v_ref[0, rs, ls]
        lw = -EXP_NEG_HALF * jax.nn.sigmoid(w0 + wl_ref[0, rs, ls])
        ag = jax.nn.sigmoid(a0 + al_ref[0, rs, ls])
        if has_vres:
            v = v + (vf_ref[0, rs, ls] - v) * jax.nn.sigmoid(v0 + vl_ref[0, rs, ls])
        kkr = k * k_k
        k2 = k * (1.0 + (ag - 1.0) * k_a)
        sums = gsum(jnp.concatenate([kkr * kkr, r * k2 * r_k], axis=0))
        f64[i, q64["g"]] = _dot01(tril, lw)
        f64[i, q64["lw"]] = lw
        f64[i, q64["ag"]] = ag
        f64[i, q64["v"]] = v
        f64[i, q64["kkr"]] = kkr
        f64[i, q64["k2"]] = k2
        f64[i, q64["ssq"]] = sums[0:c]
        f64[i, q64["bv"]] = sums[c:] * v

    for ci, pp in bodies:
        i, rs, ls = where(ci, pp)
        g = f64[i, q64["g"]]
        lw = f64[i, q64["lw"]]
        v = f64[i, q64["v"]]
        kk = f64[i, q64["kkr"]] * lax.rsqrt(f64[i, q64["ssq"]] + L2_EPS)
        e_g = jnp.exp(g)
        e_gn = jnp.exp(-g)
        rt = r_ref[0, rs, ls] * e_g
        at = -kk * jnp.exp(g - lw)
        kt = f64[i, q64["k2"]] * e_gn
        bt = kk * f64[i, q64["ag"]] * e_gn
        x = _bdot_nt(jnp.concatenate([at, rt], axis=0), jnp.concatenate([bd(kt), bd(bt)], axis=0))
        a_ab = jnp.where(strict, x[0:c, LANES:], 0.0)
        e_last = e_g[c - 1:c, :]
        s64[i, b64["aak"]] = jnp.where(strict, x[0:c, 0:LANES], 0.0).astype(BF16)
        s64[i, b64["ark"]] = jnp.where(incl, x[c:, 0:LANES], 0.0).astype(BF16)
        s64[i, b64["arb"]] = jnp.where(incl, x[c:, LANES:], 0.0).astype(BF16)
        np_s[i, 0:c] = a_ab.astype(BF16)
        np_s[i, c:] = (eye_cat + a_ab).astype(BF16)
        s128[i, b128["nb"]] = bd(a_ab).astype(BF16)
        f64[i, q64["p"]] = eye_cat + a_ab
        wr_s[i, c:] = rt.astype(BF16)
        s64[i, b64["vb"]] = v.astype(BF16)
        s128[i, b128["atbd"]] = bd(at).astype(BF16)
        s128[i, b128["vbd"]] = bd(v).astype(BF16)
        s128[i, b128["bkt"]] = jnp.concatenate([bt * e_last, kt * e_last], axis=0).T.astype(BF16)
        f128[i] = jnp.broadcast_to(e_last, (LANES, LANES)).T

    for ci, pp in bodies:
        i, _, _ = where(ci, pp)
        s128[i, b128["avbd"]] = bd(_dot(s64[i, b64["aak"]], s128[i, b128["vbd"]])).astype(BF16)

    for lvl in range(6):
        for ci, pp in bodies:
            i, _, _ = where(ci, pp)
            lhs = np_s[i, 0:c] if lvl == 0 else (np_s[i, c:] if lvl == 5 else np_s[i])
            out = _dot(lhs, s128[i, b128["nb"]])
            if lvl > 0:
                p = f64[i, q64["p"]] + out[-c:]
                f64[i, q64["p"]] = p
                np_s[i, c:] = p.astype(BF16)
            if lvl < 5:
                n2 = out[0:c]
                np_s[i, 0:c] = n2.astype(BF16)
                s128[i, b128["nb"]] = bd(n2).astype(BF16)

    for ci, pp in bodies:
        i, _, _ = where(ci, pp)
        rhs = jnp.concatenate([s128[i, b128["atbd"]], s128[i, b128["avbd"]]], axis=1)
        wu = _dot(f64[i, q64["p"]].astype(BF16), rhs)
        wr_s[i, 0:c] = wu[:, 0:LANES].astype(BF16)
        f64[i, q64["uc"]] = wu[:, LANES:]

    for ci in range(nchunk):
        for pp in range(pairs):
            i, _, _ = where(ci, pp)
            wr = _dot(wr_s[i], hb_s[pp])
            sa = wr[0:c] + f64[i, q64["uc"]]
            s64[i, b64["sab"]] = sa.astype(BF16)
            s128[i, b128["nb"]] = bd(sa).astype(BF16)
            f64[i, q64["y"]] = wr[c:]
        for pp in range(pairs):
            i, _, _ = where(ci, pp)
            f64[i, q64["y"]] = f64[i, q64["y"]] + _dot(
                jnp.concatenate([s64[i, b64["arb"]], s64[i, b64["ark"]]], axis=1),
                jnp.concatenate([s128[i, b128["nb"]], s128[i, b128["vbd"]]], axis=0))
            h_new = f128[i] * h_ref[pp] + bdmask * _dot(
                s128[i, b128["bkt"]], jnp.concatenate([s64[i, b64["sab"]], s64[i, b64["vb"]]], axis=0))
            h_ref[pp] = h_new
            hb_s[pp] = h_new.astype(BF16)

    for ci, pp in bodies:
        i, _, _ = where(ci, pp)
        y = f64[i, q64["y"]]
        f64[i, q64["d"]] = y - gsum(y) * (1.0 / RW_HEAD)
    for ci, pp in bodies:
        i, rs, ls = where(ci, pp)
        prm = prm_ref[:, ls]
        d = f64[i, q64["d"]]
        var = gsum(d * d) * (1.0 / RW_HEAD)
        yn = d * lax.rsqrt(var + GN_EPS) * prm[5:6] + prm[6:7]
        o_ref[0, rs, ls] = ((yn + f64[i, q64["bv"]]) * _silu(z_ref[0, rs, ls])).astype(o_ref.dtype)


def _rwkv7(rkvz, wl, al, vl, v_first, prm):
    b, t, d4 = rkvz.shape
    d = d4 // 4
    tc = _tile(t, 256)
    has_vres = vl is not None
    pg = _tile(d // LANES, 8)
    wd = pg * LANES
    c = RW_CHUNK
    nbody = (tc // c) * pg
    ngrp = d // wd

    def sec(s):
        return pl.BlockSpec((1, tc, wd), lambda i, p, j: (i, j, s * ngrp + p))

    flat = pl.BlockSpec((1, tc, wd), lambda i, p, j: (i, j, p))
    in_specs = [sec(0), sec(1), sec(2), sec(3), flat, flat]
    args = [rkvz, rkvz, rkvz, rkvz, wl, al]
    if has_vres:
        in_specs += [flat, sec(2)]
        args += [vl, v_first]
    in_specs.append(pl.BlockSpec((8, wd), lambda i, p, j: (0, p)))
    args.append(prm)
    return pl.pallas_call(
        functools.partial(_rw_kernel, has_vres=has_vres),
        grid=(b, d // wd, t // tc),
        in_specs=in_specs,
        out_specs=flat,
        out_shape=jax.ShapeDtypeStruct((b, t, d), BF16),
        scratch_shapes=[pltpu.VMEM((pg, LANES, LANES), F32),
                        pltpu.VMEM((pg, LANES, LANES), BF16),
                        pltpu.VMEM((nbody, len(_RW_B64), c, LANES), BF16),
                        pltpu.VMEM((nbody, len(_RW_B128), 2 * c, LANES), BF16),
                        pltpu.VMEM((nbody, 2 * c, LANES), BF16),
                        pltpu.VMEM((nbody, 2 * c, LANES), BF16),
                        pltpu.VMEM((nbody, len(_RW_F64), c, LANES), F32),
                        pltpu.VMEM((nbody, LANES, LANES), F32)],
        compiler_params=_params("parallel", "parallel", "arbitrary"),
        name="rwkv7",
    )(*args)


def kernel(x, norm_gains, mix_w_in, dn_conv, dn_a_log, dn_dt_bias, dn_out_norm, hg_lb_logits, hg_out_norm,
           mix_w_out, rw_mu, rw_w_rkvz, rw_w0, rw_w1, rw_w2, rw_a0, rw_a1, rw_a2, rw_v0, rw_v1, rw_v2,
           rw_k_k, rw_k_a, rw_r_k, rw_ln_w, rw_ln_b, rw_w_out, final_norm):
    b, t, d = x.shape
    m = b * t
    depth = norm_gains.shape[0]
    dn_heads = dn_a_log.shape[1]
    dn_width = dn_conv.shape[2] // 3
    hg_width = hg_lb_logits.shape[1]
    hg_heads = hg_width // LANES
    assert dn_width == dn_heads * LANES and hg_out_norm.shape[1] == LANES and rw_r_k.shape[2] == RW_HEAD
    dn_ba = 4 * dn_width
    hg_q = dn_ba + 2 * dn_heads
    assert 2 * dn_heads <= LANES

    h = x.reshape(m, d)
    v_first = None
    for layer in range(depth):
        gain = norm_gains[layer]
        if layer % 2 == 0:
            e = layer // 2
            w_in = mix_w_in[e]
            w_main = jnp.concatenate([w_in[:, :dn_ba], w_in[:, hg_q:]], axis=1).astype(BF16)
            w_ba = jnp.pad(w_in[:, dn_ba:hg_q], ((0, 0), (0, LANES - 2 * dn_heads))).astype(BF16)
            p, ba = _norm_matmul(h, gain, w_main, w_ba)
            p = p.reshape(b, t, -1)
            ba = ba.reshape(b, t, LANES)
            o_a = _deltanet(p, ba, dn_conv[e], dn_a_log[e], dn_dt_bias[e], dn_out_norm[e], dn_heads)
            o_b = _hgrn2(p, dn_ba // LANES, hg_lb_logits, hg_out_norm[e], hg_heads, e)
            h = _out_matmul([o_a.reshape(m, -1), o_b.reshape(m, -1)], mix_w_out[e].astype(BF16), h)
        else:
            o = layer // 2
            has_vres = v_first is not None
            rows = [0, 2, 3, 5, 1, 4] + ([3] if has_vres else [])
            downs = [rw_w1[o], rw_a1[o]] + ([rw_v1[o - 1]] if has_vres else [])
            a_down = jnp.stack([jnp.pad(a, ((0, 0), (0, LANES - a.shape[1]))) for a in downs]).astype(BF16)
            rkvz, mids = _mix_matmul(h.reshape(b, t, d), gain, rw_mu[o][jnp.array(rows)],
                                     rw_w_rkvz[o].astype(BF16), a_down)
            rkvz = rkvz.reshape(b, t, 4 * d)
            wl = _lora_up(mids, 0, rw_w2[o], True).reshape(b, t, d)
            al = _lora_up(mids, 1, rw_a2[o], False).reshape(b, t, d)
            vl = None
            v0 = jnp.zeros((d,), F32)
            if has_vres:
                vl = _lora_up(mids, 2, rw_v2[o - 1], False).reshape(b, t, d)
                v0 = rw_v0[o - 1]
            else:
                v_first = rkvz
            prm = jnp.stack([rw_w0[o], rw_a0[o], rw_k_k[o], rw_k_a[o], rw_r_k[o].reshape(d),
                             rw_ln_w[o], rw_ln_b[o], v0]).astype(F32)
            y = _rwkv7(rkvz, wl, al, vl, v_first, prm)
            h = _out_matmul([y.reshape(m, d)], rw_w_out[o].astype(BF16), h)
    return _rmsnorm(h, final_norm, x.dtype).reshape(b, t, d)
```

```python
import functools
import math

import jax
import jax.numpy as jnp
from jax import lax
from jax.experimental import pallas as pl
from jax.experimental.pallas import tpu as pltpu

F32 = jnp.float32
BF16 = jnp.bfloat16

NORM_EPS = 1e-6
GN_EPS = 64e-5
L2_EPS = 1e-6
CONV_TAPS = 4

LANES = 128
SUBLANES = 8
V7X_VMEM_LIMIT_BYTES = 56 * 1024 * 1024

DN_CHUNK = 64
RW_CHUNK = 64
RW_HEAD = 64
EXP_NEG_HALF = math.exp(-0.5)
MIX_PROLOGUE_ROWS = 256
MIX_DOT_ROWS = 512


def _params(*semantics):
    return pltpu.CompilerParams(dimension_semantics=semantics, vmem_limit_bytes=V7X_VMEM_LIMIT_BYTES)


def _tile(dim, pref):
    t = min(dim, pref)
    assert dim % t == 0, (dim, pref)
    return t


def _dot(a, b):
    return jnp.dot(a, b, preferred_element_type=F32)


def _dot_nt(a, b):
    return lax.dot_general(a, b, (((1,), (1,)), ((), ())), preferred_element_type=F32)


def _bdot_nt(a, b):
    return _dot_nt(a.astype(BF16), b.astype(BF16))


def _dot01(m01, x):
    x1 = x.astype(BF16)
    x2 = (x - x1.astype(F32)).astype(BF16)
    return _dot(m01, x1) + _dot(m01, x2)


def _silu(x):
    return x * jax.nn.sigmoid(x)


def _softplus(x):
    return jnp.maximum(x, 0.0) + jnp.log1p(jnp.exp(-jnp.abs(x)))


def _iota2(shape, dim):
    return lax.broadcasted_iota(jnp.int32, shape, dim)


def _shift_rows(x, halo, j):
    full = pltpu.roll(x, j, axis=0)
    hfix = pltpu.roll(halo, j, axis=0)
    top = jnp.where(_iota2(hfix.shape, 0) < j, hfix, full[0:SUBLANES])
    return jnp.concatenate([top, full[SUBLANES:]], axis=0)


def _rms(x, g):
    return x * lax.rsqrt(jnp.mean(x * x, axis=-1, keepdims=True) + NORM_EPS) * g


def _norm_mm_kernel(x_ref, g_ref, w_ref, wn_ref, o_ref, on_ref, hn_s):
    j = pl.program_id(1)

    @pl.when(j == 0)
    def _():
        hn_s[...] = _rms(x_ref[...], g_ref[...]).astype(BF16)
        on_ref[...] = _dot(hn_s[...], wn_ref[...])

    @pl.when(j > 0)
    def _():
        o_ref[...] = _dot(hn_s[...], w_ref[...])


def _norm_matmul(h, gain, w, w_narrow):
    m, d = h.shape
    n = w.shape[1]
    tm = _tile(m, 1024)
    tn = _tile(n, 1024)
    n_main = n // tn

    def col(j):
        return jnp.maximum(j - 1, 0)

    return pl.pallas_call(
        _norm_mm_kernel,
        grid=(m // tm, n_main + 1),
        in_specs=[pl.BlockSpec((tm, d), lambda i, j: (i, 0)),
                  pl.BlockSpec((1, d), lambda i, j: (0, 0)),
                  pl.BlockSpec((d, tn), lambda i, j: (0, col(j))),
                  pl.BlockSpec((d, LANES), lambda i, j: (0, 0))],
        out_specs=[pl.BlockSpec((tm, tn), lambda i, j: (i, col(j))),
                   pl.BlockSpec((tm, LANES), lambda i, j: (i, 0))],
        out_shape=[jax.ShapeDtypeStruct((m, n), F32), jax.ShapeDtypeStruct((m, LANES), F32)],
        scratch_shapes=[pltpu.VMEM((tm, d), BF16)],
        compiler_params=_params("parallel", "arbitrary"),
        name="norm_matmul",
    )(h, gain.reshape(1, d), w, w_narrow)


def _mix_mm_kernel(x_ref, halo_ref, g_ref, mu_ref, w_ref, a_ref, o_ref, mid_ref, hn_s, xx_s, *,
                   tiles_per_group, tiles_per_seq, n_groups):
    i = pl.program_id(0)
    j = pl.program_id(1)
    tm = hn_s.shape[0]
    rc = min(tm, MIX_PROLOGUE_ROWS)
    mc = min(tm, MIX_DOT_ROWS)

    def mix(rows, mu_row):
        return hn_s[rows, :] + xx_s[rows, :] * mu_row.astype(BF16)

    @pl.when(j == 0)
    def _():
        g = g_ref[...]
        first = lax.rem(i, tiles_per_seq) == 0
        for r in range(tm // rc):
            rows = slice(r * rc, (r + 1) * rc)
            hn = _rms(x_ref[0, rows, :], g)
            if r == 0:
                hp = jnp.where(first, 0.0, _rms(halo_ref[0], g))
            else:
                hp = _rms(x_ref[0, r * rc - SUBLANES:r * rc, :], g)
            hn_s[rows, :] = hn.astype(BF16)
            xx_s[rows, :] = (_shift_rows(hn, hp, 1) - hn).astype(BF16)

        for l in range(a_ref.shape[0]):
            mu_row = mu_ref[n_groups + l:n_groups + l + 1, :]
            for r in range(tm // mc):
                rows = slice(r * mc, (r + 1) * mc)
                mid_ref[rows, l * LANES:(l + 1) * LANES] = _dot(mix(rows, mu_row), a_ref[l])

    @pl.when(j > 0)
    def _():
        mu_row = mu_ref[pl.ds(lax.div(j - 1, tiles_per_group), 1), :]
        for r in range(tm // mc):
            rows = slice(r * mc, (r + 1) * mc)
            o_ref[rows, :] = _dot(mix(rows, mu_row), w_ref[0])


def _mix_matmul(h, gain, mu, w, a_down):
    b, t, d = h.shape
    m = b * t
    g, _, n = w.shape
    nl = a_down.shape[0]
    tm = _tile(t, 1024)
    tn = _tile(n, 1024)
    tpg = n // tn
    n_main = g * tpg
    tps = t // tm
    hb = tm // SUBLANES

    def col(j):
        return jnp.maximum(j - 1, 0)

    return pl.pallas_call(
        functools.partial(_mix_mm_kernel, tiles_per_group=tpg, tiles_per_seq=tps, n_groups=g),
        grid=(m // tm, n_main + 1),
        in_specs=[pl.BlockSpec((1, tm, d), lambda i, j: (i // tps, i % tps, 0)),
                  pl.BlockSpec((1, SUBLANES, d), lambda i, j: (i // tps, jnp.maximum((i % tps) * hb - 1, 0), 0)),
                  pl.BlockSpec((1, d), lambda i, j: (0, 0)),
                  pl.BlockSpec((g + nl, d), lambda i, j: (0, 0)),
                  pl.BlockSpec((1, d, tn), lambda i, j: (col(j) // tpg, 0, col(j) % tpg)),
                  pl.BlockSpec((nl, d, LANES), lambda i, j: (0, 0, 0))],
        out_specs=[pl.BlockSpec((tm, tn), lambda i, j: (i, col(j))),
                   pl.BlockSpec((tm, nl * LANES), lambda i, j: (i, 0))],
        out_shape=[jax.ShapeDtypeStruct((m, g * n), F32), jax.ShapeDtypeStruct((m, nl * LANES), F32)],
        scratch_shapes=[pltpu.VMEM((tm, d), BF16), pltpu.VMEM((tm, d), BF16)],
        compiler_params=_params("parallel", "arbitrary"),
        name="mix_matmul",
    )(h, h, gain.reshape(1, d), mu, w, a_down)


def _lora_up_kernel(mid_ref, b_ref, o_ref, *, use_tanh):
    mid = mid_ref[...]
    if use_tanh:
        mid = jnp.tanh(mid)
    o_ref[...] = _dot(mid.astype(BF16), b_ref[...])


def _lora_up(mids, sel, b_up, use_tanh):
    m = mids.shape[0]
    rank, n = b_up.shape
    b_p = jnp.pad(b_up, ((0, LANES - rank), (0, 0))).astype(BF16)
    tm = _tile(m, 1024)
    return pl.pallas_call(
        functools.partial(_lora_up_kernel, use_tanh=use_tanh),
        grid=(m // tm,),
        in_specs=[pl.BlockSpec((tm, LANES), lambda i: (i, sel)),
                  pl.BlockSpec((LANES, n), lambda i: (0, 0))],
        out_specs=pl.BlockSpec((tm, n), lambda i: (i, 0)),
        out_shape=jax.ShapeDtypeStruct((m, n), F32),
        compiler_params=_params("parallel"),
        name="lora_up",
    )(mids, b_p)


def _out_mm_kernel(*refs, n_x):
    x_refs, w_ref, r_ref, o_ref = refs[:n_x], refs[n_x], refs[n_x + 1], refs[n_x + 2]
    acc = r_ref[...]
    k0 = 0
    for x_ref in x_refs:
        kx = x_ref.shape[1]
        acc = acc + _dot(x_ref[...], w_ref[k0:k0 + kx, :])
        k0 += kx
    o_ref[...] = acc


def _out_matmul(xs, w, residual):
    m = xs[0].shape[0]
    k, n = w.shape
    tm = _tile(m, 512)
    tn = _tile(n, 2048)
    in_specs = [pl.BlockSpec((tm, x.shape[1]), lambda i, j: (i, 0)) for x in xs]
    in_specs += [pl.BlockSpec((k, tn), lambda i, j: (0, j)), pl.BlockSpec((tm, tn), lambda i, j: (i, j))]
    return pl.pallas_call(
        functools.partial(_out_mm_kernel, n_x=len(xs)),
        grid=(m // tm, n // tn),
        in_specs=in_specs,
        out_specs=pl.BlockSpec((tm, tn), lambda i, j: (i, j)),
        out_shape=jax.ShapeDtypeStruct((m, n), F32),
        compiler_params=_params("parallel", "parallel"),
        name="out_matmul",
    )(*xs, w, residual)


def _rms_kernel(x_ref, g_ref, o_ref):
    o_ref[...] = _rms(x_ref[...], g_ref[...]).astype(o_ref.dtype)


def _rmsnorm(x, gain, out_dtype):
    m, d = x.shape
    tm = _tile(m, 512)
    return pl.pallas_call(
        _rms_kernel,
        grid=(m // tm,),
        in_specs=[pl.BlockSpec((tm, d), lambda i: (i, 0)), pl.BlockSpec((1, d), lambda i: (0, 0))],
        out_specs=pl.BlockSpec((tm, d), lambda i: (i, 0)),
        out_shape=jax.ShapeDtypeStruct((m, d), out_dtype),
        compiler_params=_params("parallel"),
        name="rmsnorm",
    )(x, gain.reshape(1, d))


_DN_F128 = dict(q=0, k=1, vb=2, bc=3, g=4, u0=5, o=6)
_DN_F64 = dict(p=0, dmat=1)


def _dn_kernel(q_ref, k_ref, v_ref, qh_ref, kh_ref, vh_ref, z_ref, ba_ref, sel_ref, hp_ref, cw_ref, on_ref,
               o_ref, s_ref, sb_s, f128, wq_s, ub_s, nc_s, f64, rhs_s, akd_s, cd_s):
    t = pl.program_id(2)
    tc = q_ref.shape[1]
    heads = s_ref.shape[0]
    c = DN_CHUNK
    qf, qp = _DN_F128, _DN_F64

    @pl.when(t == 0)
    def _():
        s_ref[...] = jnp.zeros_like(s_ref)
        sb_s[...] = jnp.zeros_like(sb_s)

    row = _iota2((c, c), 0)
    col = _iota2((c, c), 1)
    causal = row >= col
    strict = row > col
    strict_f = strict.astype(F32)
    eye = (row == col).astype(F32)
    tril = causal.astype(BF16)
    onorm = on_ref[...]

    bodies = [(ci, hh) for ci in range(tc // c) for hh in range(heads)]

    def where(ci, hh):
        return ci * heads + hh, slice(ci * c, (ci + 1) * c), slice(hh * LANES, (hh + 1) * LANES)

    def conv_silu(x_ref, h_ref, ci, rs, ls, w):
        x = x_ref[0, rs, ls]
        if ci == 0:
            halo = jnp.where(t > 0, h_ref[0, :, ls], 0.0)
        else:
            halo = x_ref[0, ci * c - SUBLANES:ci * c, ls]
        y = x * w[CONV_TAPS - 1:CONV_TAPS, :]
        for j in range(1, CONV_TAPS):
            y = y + _shift_rows(x, halo, j) * w[CONV_TAPS - 1 - j:CONV_TAPS - j, :]
        return _silu(y)

    def l2n(x):
        return x * lax.rsqrt(jnp.sum(x * x, axis=-1, keepdims=True) + L2_EPS)

    for ci, hh in bodies:
        i, rs, ls = where(ci, hh)
        cw = cw_ref[hh]
        sel = sel_ref[hh]
        hp = hp_ref[hh]
        ba = ba_ref[0, rs, :]
        qc = l2n(conv_silu(q_ref, qh_ref, ci, rs, ls, cw[0:CONV_TAPS])) * (LANES ** -0.5)
        kc = l2n(conv_silu(k_ref, kh_ref, ci, rs, ls, cw[CONV_TAPS:2 * CONV_TAPS]))
        vc = conv_silu(v_ref, vh_ref, ci, rs, ls, cw[2 * CONV_TAPS:3 * CONV_TAPS])
        beta = jax.nn.sigmoid(jnp.sum(ba * sel[0:1], axis=-1, keepdims=True))
        alpha_pre = jnp.sum(ba * sel[1:2], axis=-1, keepdims=True)
        bc = jnp.broadcast_to(beta, (c, LANES))
        lac = -jnp.exp(hp[0:1]) * _softplus(jnp.broadcast_to(alpha_pre, (c, LANES)) + hp[1:2])
        f128[i, qf["q"]] = qc
        f128[i, qf["k"]] = kc
        f128[i, qf["vb"]] = vc * bc
        f128[i, qf["bc"]] = bc
        f128[i, qf["g"]] = _dot01(tril, lac)
        f64[i, qp["dmat"]] = _dot01(tril, lac[:, 0:c] * strict_f)

    for ci, hh in bodies:
        i, rs, ls = where(ci, hh)
        qc = f128[i, qf["q"]]
        kc = f128[i, qf["k"]]
        bc = f128[i, qf["bc"]]
        g = f128[i, qf["g"]]
        decay = jnp.where(causal, jnp.exp(jnp.where(causal, f64[i, qp["dmat"]], 0.0)), 0.0)
        kb = kc.astype(BF16)
        qkk = _dot_nt(jnp.concatenate([qc.astype(BF16), kb], axis=0), kb)
        lower = jnp.where(strict, bc[:, 0:c] * qkk[c:] * decay, 0.0)
        gamma = jnp.exp(g)
        glast = g[c - 1:c, :]
        akd_s[i, 0:c] = jnp.where(causal, qkk[0:c] * decay, 0.0).astype(BF16)
        nc_s[i] = (-lower).astype(BF16)
        f64[i, qp["p"]] = eye - lower
        rhs_s[i] = jnp.concatenate([f128[i, qf["vb"]], kc * (bc * gamma)], axis=1).astype(BF16)
        wq_s[i, c:] = (qc * gamma).astype(BF16)
        akd_s[i, c:] = (kc * jnp.exp(glast - g)).T.astype(BF16)
        cd_s[i] = jnp.broadcast_to(jnp.exp(glast), (SUBLANES, LANES))

    for _ in range(5):
        for ci, hh in bodies:
            i, _, _ = where(ci, hh)
            nc = nc_s[i]
            nc_s[i] = _dot(nc, nc).astype(BF16)
        for ci, hh in bodies:
            i, _, _ = where(ci, hh)
            p = f64[i, qp["p"]]
            f64[i, qp["p"]] = p + _dot(p.astype(BF16), nc_s[i])

    for ci, hh in bodies:
        i, _, _ = where(ci, hh)
        sol = _dot(f64[i, qp["p"]].astype(BF16), rhs_s[i])
        f128[i, qf["u0"]] = sol[:, 0:LANES]
        wq_s[i, 0:c] = sol[:, LANES:].astype(BF16)

    for ci in range(tc // c):
        for hh in range(heads):
            i, _, _ = where(ci, hh)
            wq = _dot(wq_s[i], sb_s[hh])
            ub_s[i] = (f128[i, qf["u0"]] - wq[0:c]).astype(BF16)
            f128[i, qf["o"]] = wq[c:]
        for hh in range(heads):
            i, _, _ = where(ci, hh)
            ou = _dot(akd_s[i], ub_s[i])
            f128[i, qf["o"]] = f128[i, qf["o"]] + ou[0:c]
            s_new = cd_s[i][0:1] * s_ref[hh] + ou[c:]
            s_ref[hh] = s_new
            sb_s[hh] = s_new.astype(BF16)

    for ci, hh in bodies:
        i, rs, ls = where(ci, hh)
        o = f128[i, qf["o"]]
        on = o * lax.rsqrt(jnp.mean(o * o, axis=-1, keepdims=True) + NORM_EPS) * onorm
        o_ref[0, rs, ls] = (on * _silu(z_ref[0, rs, ls])).astype(o_ref.dtype)


def _deltanet(p, ba, conv_w, a_log, dt_bias, out_norm, nheads):
    b, t, _ = p.shape
    tc = _tile(t, 256)
    c = DN_CHUNK
    assert tc % c == 0
    hb = tc // SUBLANES
    nh = nheads
    hg = _tile(nh, 8)
    ng = nh // hg
    wd = hg * LANES
    nbody = (tc // c) * hg
    cw = conv_w.reshape(CONV_TAPS, 3, nh, LANES).transpose(2, 1, 0, 3).reshape(nh, 3 * CONV_TAPS, LANES)
    lane = jnp.arange(LANES)[None, None, :]
    head = jnp.arange(nh)[:, None, None]
    sel = jnp.concatenate([(lane == head), (lane == head + nh)], axis=1).astype(F32)
    hp = jnp.broadcast_to(jnp.stack([a_log, dt_bias], axis=1)[:, :, None], (nh, 2, LANES)).astype(F32)

    def main(sec):
        return pl.BlockSpec((1, tc, wd), lambda i, h, j: (i, j, sec * ng + h))

    def halo(sec):
        return pl.BlockSpec((1, SUBLANES, wd), lambda i, h, j: (i, jnp.maximum(j * hb - 1, 0), sec * ng + h))

    per_head = lambda rows: pl.BlockSpec((hg, rows, LANES), lambda i, h, j: (h, 0, 0))
    return pl.pallas_call(
        _dn_kernel,
        grid=(b, ng, t // tc),
        in_specs=[main(0), main(1), main(2), halo(0), halo(1), halo(2), main(3),
                  pl.BlockSpec((1, tc, LANES), lambda i, h, j: (i, j, 0)),
                  per_head(2), per_head(2), per_head(3 * CONV_TAPS),
                  pl.BlockSpec((1, LANES), lambda i, h, j: (0, 0))],
        out_specs=pl.BlockSpec((1, tc, wd), lambda i, h, j: (i, j, h)),
        out_shape=jax.ShapeDtypeStruct((b, t, nh * LANES), BF16),
        scratch_shapes=[pltpu.VMEM((hg, LANES, LANES), F32),
                        pltpu.VMEM((hg, LANES, LANES), BF16),
                        pltpu.VMEM((nbody, len(_DN_F128), c, LANES), F32),
                        pltpu.VMEM((nbody, 2 * c, LANES), BF16),
                        pltpu.VMEM((nbody, c, LANES), BF16),
                        pltpu.VMEM((nbody, c, c), BF16),
                        pltpu.VMEM((nbody, len(_DN_F64), c, c), F32),
                        pltpu.VMEM((nbody, c, 2 * LANES), BF16),
                        pltpu.VMEM((nbody, c + LANES, c), BF16),
                        pltpu.VMEM((nbody, SUBLANES, LANES), F32)],
        compiler_params=_params("parallel", "parallel", "arbitrary"),
        name="deltanet",
    )(p, p, p, p, p, p, p, ba, sel, hp, cw, out_norm.reshape(1, LANES))


HG_LEVELS = 6
HG_BLOCK = 64


def _hg_tables():
    import numpy as np
    c = HG_BLOCK
    t = np.arange(c)[:, None]
    i = np.arange(c)[None, :]
    rows = [i <= t, i > t]
    q_rows, k_rows = [], []
    level = np.full((c, c), -1, np.int32)
    level[np.arange(c), np.arange(c)] = HG_LEVELS
    for li in range(HG_LEVELS):
        m = c >> (li + 1)
        pos = t % (2 * m)
        upper = pos >= m
        ref = t - pos + m - 1
        q_rows.append(upper & (i > ref) & (i <= t))
        k_rows.append((~upper) & (i > t) & (i <= ref))
        same = (t // (2 * m)) == (i // (2 * m))
        level[same & upper & ((i % (2 * m)) < m)] = li
    table = np.concatenate(rows + q_rows + k_rows, axis=0).astype(np.float32)
    return table, level


def _hg_kernel(q_ref, f_ref, i_ref, z_ref, lb_ref, on_ref, tab_ref, lvl_ref, o_ref,
               s_ref, sb_s, qt_s, kt_s, qb_s, vb_s, klt_s, att_s, ecol_s, inc_s, o_s, *, layer):
    t = pl.program_id(2)
    tc = q_ref.shape[1]
    heads = s_ref.shape[0]
    c = HG_BLOCK
    nl = HG_LEVELS

    @pl.when(t == 0)
    def _():
        s_ref[...] = jnp.zeros_like(s_ref)
        sb_s[...] = jnp.zeros_like(sb_s)

    table = tab_ref[...]
    level = lvl_ref[...]
    onorm = on_ref[...]
    bodies = [(ci, hh) for ci in range(tc // c) for hh in range(heads)]

    def where(ci, hh):
        return ci * heads + hh, slice(ci * c, (ci + 1) * c), slice(hh * LANES, (hh + 1) * LANES)

    for ci, hh in bodies:
        i, rs, ls = where(ci, hh)
        logits = lb_ref[:, ls]
        e = jnp.exp(logits - jnp.max(logits, axis=0, keepdims=True))
        share = e / jnp.sum(e, axis=0, keepdims=True)
        lbound = jnp.sum(share[0:layer + 1], axis=0, keepdims=True) - share[0:1]
        q = _silu(q_ref[0, rs, ls]) * (LANES ** -0.5)
        forget = lbound + (1.0 - lbound) * jax.nn.sigmoid(f_ref[0, rs, ls])
        k = 1.0 - forget
        d = _dot(table, jnp.log(forget).astype(BF16))
        b = d[0:c]
        qb_s[i] = (q * jnp.exp(b)).astype(BF16)
        klt_s[i] = (k * jnp.exp(d[c:2 * c])).T.astype(BF16)
        ecol_s[i] = jnp.broadcast_to(jnp.exp(b[c - 1:c]), (LANES, LANES)).T
        for li in range(nl):
            qt_s[i, li] = (q * jnp.exp(d[(2 + li) * c:(3 + li) * c])).astype(BF16)
            kt_s[i, li] = (k * jnp.exp(d[(2 + nl + li) * c:(3 + nl + li) * c])).astype(BF16)
        qt_s[i, nl] = q.astype(BF16)
        kt_s[i, nl] = k.astype(BF16)
        vb_s[i] = i_ref[0, rs, ls].astype(BF16)

    for ci, hh in bodies:
        i, _, _ = where(ci, hh)
        att = jnp.zeros((c, c), F32)
        for li in range(nl + 1):
            att = att + jnp.where(level == li, _dot_nt(qt_s[i, li], kt_s[i, li]), 0.0)
        att_s[i] = att.astype(BF16)
        inc_s[i] = _dot(klt_s[i], vb_s[i])

    for ci, hh in bodies:
        i, _, _ = where(ci, hh)
        o_s[i] = _dot(att_s[i], vb_s[i])

    for ci, hh in bodies:
        i, _, _ = where(ci, hh)
        o_s[i] = o_s[i] + _dot(qb_s[i], sb_s[hh])
        s_new = ecol_s[i] * s_ref[hh] + inc_s[i]
        s_ref[hh] = s_new
        sb_s[hh] = s_new.astype(BF16)

    for ci, hh in bodies:
        i, rs, ls = where(ci, hh)
        o = o_s[i]
        on = o * lax.rsqrt(jnp.mean(o * o, axis=-1, keepdims=True) + NORM_EPS) * onorm
        o_ref[0, rs, ls] = (on * _silu(z_ref[0, rs, ls])).astype(o_ref.dtype)


def _hgrn2(p, col0, lb_logits, out_norm, nheads, layer):
    b, t, _ = p.shape
    tc = _tile(t, 256)
    c = HG_BLOCK
    assert tc % c == 0
    nh = nheads
    hg = _tile(nh, 8)
    ng = nh // hg
    wd = hg * LANES
    assert col0 % hg == 0
    cg = col0 // hg
    nbody = (tc // c) * hg
    table, level = _hg_tables()
    nrow = table.shape[0]

    def main(sec):
        return pl.BlockSpec((1, tc, wd), lambda i, h, j: (i, j, cg + sec * ng + h))

    nl = lb_logits.shape[0]
    const = lambda shape: pl.BlockSpec(shape, lambda i, h, j: (0, 0))
    return pl.pallas_call(
        functools.partial(_hg_kernel, layer=layer),
        grid=(b, ng, t // tc),
        in_specs=[main(0), main(1), main(2), main(3),
                  pl.BlockSpec((nl, wd), lambda i, h, j: (0, h)),
                  const((1, LANES)), const((nrow, c)), const((c, c))],
        out_specs=pl.BlockSpec((1, tc, wd), lambda i, h, j: (i, j, h)),
        out_shape=jax.ShapeDtypeStruct((b, t, nh * LANES), BF16),
        scratch_shapes=[pltpu.VMEM((hg, LANES, LANES), F32),
                        pltpu.VMEM((hg, LANES, LANES), BF16),
                        pltpu.VMEM((nbody, HG_LEVELS + 1, c, LANES), BF16),
                        pltpu.VMEM((nbody, HG_LEVELS + 1, c, LANES), BF16),
                        pltpu.VMEM((nbody, c, LANES), BF16),
                        pltpu.VMEM((nbody, c, LANES), BF16),
                        pltpu.VMEM((nbody, LANES, c), BF16),
                        pltpu.VMEM((nbody, c, c), BF16),
                        pltpu.VMEM((nbody, LANES, LANES), F32),
                        pltpu.VMEM((nbody, LANES, LANES), F32),
                        pltpu.VMEM((nbody, c, LANES), F32)],
        compiler_params=_params("parallel", "parallel", "arbitrary"),
        name="hgrn2",
    )(p, p, p, p, lb_logits, out_norm.reshape(1, LANES), jnp.asarray(table, BF16), jnp.asarray(level))


_RW_B64 = dict(aak=0, vb=1, arb=2, ark=3, sab=4)
_RW_B128 = dict(nb=0, vbd=1, bkt=2, atbd=3, avbd=4)
_RW_F64 = dict(p=0, uc=1, g=2, bv=3, lw=4, kkr=5, k2=6, ag=7, v=8, ssq=9, y=10, d=11)


def _rw_kernel(*refs, has_vres):
    if has_vres:
        (r_ref, k_ref, v_ref, z_ref, wl_ref, al_ref, vl_ref, vf_ref, prm_ref,
         o_ref, h_ref, hb_s, s64, s128, wr_s, np_s, f64, f128) = refs
    else:
        (r_ref, k_ref, v_ref, z_ref, wl_ref, al_ref, prm_ref,
         o_ref, h_ref, hb_s, s64, s128, wr_s, np_s, f64, f128) = refs
    t = pl.program_id(2)
    tc = o_ref.shape[1]
    pairs = h_ref.shape[0]
    c = RW_CHUNK
    nchunk = tc // c
    b64, b128, q64 = _RW_B64, _RW_B128, _RW_F64

    @pl.when(t == 0)
    def _():
        h_ref[...] = jnp.zeros_like(h_ref)
        hb_s[...] = jnp.zeros_like(hb_s)

    lane1 = _iota2((1, LANES), 1)
    head0 = lane1 < RW_HEAD
    m0b = head0.astype(BF16)
    m1b = 1 - m0b
    same_head = (_iota2((LANES, LANES), 0) < RW_HEAD) == (_iota2((LANES, LANES), 1) < RW_HEAD)
    bdmask = same_head.astype(F32)
    row = _iota2((c, LANES), 0)
    scol = jnp.bitwise_and(_iota2((c, LANES), 1), RW_HEAD - 1)
    incl = row >= scol
    strict = row > scol
    eye_cat = (row == scol).astype(F32)
    tril = (_iota2((c, c), 0) >= _iota2((c, c), 1)).astype(BF16)

    def bd(x):
        xb = x.astype(BF16)
        return jnp.concatenate([xb * m0b, xb * m1b], axis=0)

    def gsum(x):
        s0 = jnp.sum(jnp.where(head0, x, 0.0), axis=-1, keepdims=True)
        s1 = jnp.sum(x, axis=-1, keepdims=True) - s0
        return jnp.where(head0, s0, s1)

    bodies = [(ci, pp) for ci in range(nchunk) for pp in range(pairs)]

    def where(ci, pp):
        return ci * pairs + pp, slice(ci * c, (ci + 1) * c), slice(pp * LANES, (pp + 1) * LANES)

    for ci, pp in bodies:
        i, rs, ls = where(ci, pp)
        prm = prm_ref[:, ls]
        w0, a0, k_k, k_a, r_k, v0 = prm[0:1], prm[1:2], prm[2:3], prm[3:4], prm[4:5], prm[7:8]
        r = r_ref[0, rs, ls]
        k = k_ref[0, rs, ls]
        v = v_ref[0, rs, ls]
        lw = -EXP_NEG_HALF * jax.nn.sigmoid(w0 + wl_ref[0, rs, ls])
        ag = jax.nn.sigmoid(a0 + al_ref[0, rs, ls])
        if has_vres:
            v = v + (vf_ref[0, rs, ls] - v) * jax.nn.sigmoid(v0 + vl_ref[0, rs, ls])
        kkr = k * k_k
        k2 = k * (1.0 + (ag - 1.0) * k_a)
        sums = gsum(jnp.concatenate([kkr * kkr, r * k2 * r_k], axis=0))
        f64[i, q64["g"]] = _dot01(tril, lw)
        f64[i, q64["lw"]] = lw
        f64[i, q64["ag"]] = ag
        f64[i, q64["v"]] = v
        f64[i, q64["kkr"]] = kkr
        f64[i, q64["k2"]] = k2
        f64[i, q64["ssq"]] = sums[0:c]
        f64[i, q64["bv"]] = sums[c:] * v

    for ci, pp in bodies:
        i, rs, ls = where(ci, pp)
        g = f64[i, q64["g"]]
        lw = f64[i, q64["lw"]]
        v = f64[i, q64["v"]]
        kk = f64[i, q64["kkr"]] * lax.rsqrt(f64[i, q64["ssq"]] + L2_EPS)
        e_g = jnp.exp(g)
        e_gn = jnp.exp(-g)
        rt = r_ref[0, rs, ls] * e_g
        at = -kk * jnp.exp(g - lw)
        kt = f64[i, q64["k2"]] * e_gn
        bt = kk * f64[i, q64["ag"]] * e_gn
        x = _bdot_nt(jnp.concatenate([at, rt], axis=0), jnp.concatenate([bd(kt), bd(bt)], axis=0))
        a_ab = jnp.where(strict, x[0:c, LANES:], 0.0)
        e_last = e_g[c - 1:c, :]
        s64[i, b64["aak"]] = jnp.where(strict, x[0:c, 0:LANES], 0.0).astype(BF16)
        s64[i, b64["ark"]] = jnp.where(incl, x[c:, 0:LANES], 0.0).astype(BF16)
        s64[i, b64["arb"]] = jnp.where(incl, x[c:, LANES:], 0.0).astype(BF16)
        np_s[i, 0:c] = a_ab.astype(BF16)
        np_s[i, c:] = (eye_cat + a_ab).astype(BF16)
        s128[i, b128["nb"]] = bd(a_ab)
        f64[i, q64["p"]] = eye_cat + a_ab
        wr_s[i, c:] = rt.astype(BF16)
        s64[i, b64["vb"]] = v.astype(BF16)
        s128[i, b128["atbd"]] = bd(at)
        s128[i, b128["vbd"]] = bd(v)
        s128[i, b128["bkt"]] = jnp.concatenate([bt * e_last, kt * e_last], axis=0).T.astype(BF16)
        f128[i] = jnp.broadcast_to(e_last, (LANES, LANES)).T

    for ci, pp in bodies:
        i, _, _ = where(ci, pp)
        s128[i, b128["avbd"]] = bd(_dot(s64[i, b64["aak"]], s128[i, b128["vbd"]]))

    for lvl in range(6):
        for ci, pp in bodies:
            i, _, _ = where(ci, pp)
            lhs = np_s[i, 0:c] if lvl == 0 else (np_s[i, c:] if lvl == 5 else np_s[i])
            out = _dot(lhs, s128[i, b128["nb"]])
            if lvl > 0:
                p = f64[i, q64["p"]] + out[-c:]
                f64[i, q64["p"]] = p
                np_s[i, c:] = p.astype(BF16)
            if lvl < 5:
                n2 = out[0:c]
                np_s[i, 0:c] = n2.astype(BF16)
                s128[i, b128["nb"]] = bd(n2)

    for ci, pp in bodies:
        i, _, _ = where(ci, pp)
        rhs = jnp.concatenate([s128[i, b128["atbd"]], s128[i, b128["avbd"]]], axis=1)
        wu = _dot(f64[i, q64["p"]].astype(BF16), rhs)
        wr_s[i, 0:c] = wu[:, 0:LANES].astype(BF16)
        f64[i, q64["uc"]] = wu[:, LANES:]

    for ci in range(nchunk):
        for pp in range(pairs):
            i, _, _ = where(ci, pp)
            wr = _dot(wr_s[i], hb_s[pp])
            sa = wr[0:c] + f64[i, q64["uc"]]
            s64[i, b64["sab"]] = sa.astype(BF16)
            s128[i, b128["nb"]] = bd(sa)
            f64[i, q64["y"]] = wr[c:]
        for pp in range(pairs):
            i, _, _ = where(ci, pp)
            f64[i, q64["y"]] = f64[i, q64["y"]] + _dot(
                jnp.concatenate([s64[i, b64["arb"]], s64[i, b64["ark"]]], axis=1),
                jnp.concatenate([s128[i, b128["nb"]], s128[i, b128["vbd"]]], axis=0))
            h_new = f128[i] * h_ref[pp] + bdmask * _dot(
                s128[i, b128["bkt"]], jnp.concatenate([s64[i, b64["sab"]], s64[i, b64["vb"]]], axis=0))
            h_ref[pp] = h_new
            hb_s[pp] = h_new.astype(BF16)

    for ci, pp in bodies:
        i, _, _ = where(ci, pp)
        y = f64[i, q64["y"]]
        f64[i, q64["d"]] = y - gsum(y) * (1.0 / RW_HEAD)
    for ci, pp in bodies:
        i, rs, ls = where(ci, pp)
        prm = prm_ref[:, ls]
        d = f64[i, q64["d"]]
        var = gsum(d * d) * (1.0 / RW_HEAD)
        yn = d * lax.rsqrt(var + GN_EPS) * prm[5:6] + prm[6:7]
        o_ref[0, rs, ls] = ((yn + f64[i, q64["bv"]]) * _silu(z_ref[0, rs, ls])).astype(o_ref.dtype)


def _rwkv7(rkvz, wl, al, vl, v_first, prm):
    b, t, d4 = rkvz.shape
    d = d4 // 4
    tc = _tile(t, 256)
    has_vres = vl is not None
    pg = _tile(d // LANES, 8)
    wd = pg * LANES
    c = RW_CHUNK
    nbody = (tc // c) * pg
    ngrp = d // wd

    def sec(s):
        return pl.BlockSpec((1, tc, wd), lambda i, p, j: (i, j, s * ngrp + p))

    flat = pl.BlockSpec((1, tc, wd), lambda i, p, j: (i, j, p))
    in_specs = [sec(0), sec(1), sec(2), sec(3), flat, flat]
    args = [rkvz, rkvz, rkvz, rkvz, wl, al]
    if has_vres:
        in_specs += [flat, sec(2)]
        args += [vl, v_first]
    in_specs.append(pl.BlockSpec((8, wd), lambda i, p, j: (0, p)))
    args.append(prm)
    return pl.pallas_call(
        functools.partial(_rw_kernel, has_vres=has_vres),
        grid=(b, d // wd, t // tc),
        in_specs=in_specs,
        out_specs=flat,
        out_shape=jax.ShapeDtypeStruct((b, t, d), BF16),
        scratch_shapes=[pltpu.VMEM((pg, LANES, LANES), F32),
                        pltpu.VMEM((pg, LANES, LANES), BF16),
                        pltpu.VMEM((nbody, len(_RW_B64), c, LANES), BF16),
                        pltpu.VMEM((nbody, len(_RW_B128), 2 * c, LANES), BF16),
                        pltpu.VMEM((nbody, 2 * c, LANES), BF16),
                        pltpu.VMEM((nbody, 2 * c, LANES), BF16),
                        pltpu.VMEM((nbody, len(_RW_F64), c, LANES), F32),
                        pltpu.VMEM((nbody, LANES, LANES), F32)],
        compiler_params=_params("parallel", "parallel", "arbitrary"),
        name="rwkv7",
    )(*args)


def kernel(x, norm_gains, mix_w_in, dn_conv, dn_a_log, dn_dt_bias, dn_out_norm, hg_lb_logits, hg_out_norm,
           mix_w_out, rw_mu, rw_w_rkvz, rw_w0, rw_w1, rw_w2, rw_a0, rw_a1, rw_a2, rw_v0, rw_v1, rw_v2,
           rw_k_k, rw_k_a, rw_r_k, rw_ln_w, rw_ln_b, rw_w_out, final_norm):
    b, t, d = x.shape
    m = b * t
    depth = norm_gains.shape[0]
    dn_heads = dn_a_log.shape[1]
    dn_width = dn_conv.shape[2] // 3
    hg_width = hg_lb_logits.shape[1]
    hg_heads = hg_width // LANES
    assert dn_width == dn_heads * LANES and hg_out_norm.shape[1] == LANES and rw_r_k.shape[2] == RW_HEAD
    dn_ba = 4 * dn_width
    hg_q = dn_ba + 2 * dn_heads
    assert 2 * dn_heads <= LANES

    h = x.reshape(m, d)
    v_first = None
    for layer in range(depth):
        gain = norm_gains[layer]
        if layer % 2 == 0:
            e = layer // 2
            w_in = mix_w_in[e]
            w_main = jnp.concatenate([w_in[:, :dn_ba], w_in[:, hg_q:]], axis=1).astype(BF16)
            w_ba = jnp.pad(w_in[:, dn_ba:hg_q], ((0, 0), (0, LANES - 2 * dn_heads))).astype(BF16)
            p, ba = _norm_matmul(h, gain, w_main, w_ba)
            p = p.reshape(b, t, -1)
            ba = ba.reshape(b, t, LANES)
            o_a = _deltanet(p, ba, dn_conv[e], dn_a_log[e], dn_dt_bias[e], dn_out_norm[e], dn_heads)
            o_b = _hgrn2(p, dn_ba // LANES, hg_lb_logits, hg_out_norm[e], hg_heads, e)
            h = _out_matmul([o_a.reshape(m, -1), o_b.reshape(m, -1)], mix_w_out[e].astype(BF16), h)
        else:
            o = layer // 2
            has_vres = v_first is not None
            rows = [0, 2, 3, 5, 1, 4] + ([3] if has_vres else [])
            downs = [rw_w1[o], rw_a1[o]] + ([rw_v1[o - 1]] if has_vres else [])
            a_down = jnp.stack([jnp.pad(a, ((0, 0), (0, LANES - a.shape[1]))) for a in downs]).astype(BF16)
            rkvz, mids = _mix_matmul(h.reshape(b, t, d), gain, rw_mu[o][jnp.array(rows)],
                                     rw_w_rkvz[o].astype(BF16), a_down)
            rkvz = rkvz.reshape(b, t, 4 * d)
            wl = _lora_up(mids, 0, rw_w2[o], True).reshape(b, t, d)
            al = _lora_up(mids, 1, rw_a2[o], False).reshape(b, t, d)
            vl = None
            v0 = jnp.zeros((d,), F32)
            if has_vres:
                vl = _lora_up(mids, 2, rw_v2[o - 1], False).reshape(b, t, d)
                v0 = rw_v0[o - 1]
            else:
                v_first = rkvz
            prm = jnp.stack([rw_w0[o], rw_a0[o], rw_k_k[o], rw_k_a[o], rw_r_k[o].reshape(d),
                             rw_ln_w[o], rw_ln_b[o], v0]).astype(F32)
            y = _rwkv7(rkvz, wl, al, vl, v_first, prm)
            h = _out_matmul([y.reshape(m, d)], rw_w_out[o].astype(BF16), h)
    return _rmsnorm(h, final_norm, x.dtype).reshape(b, t, d)
```

```python
import functools
import math

import jax
import jax.numpy as jnp
from jax import lax
from jax.experimental import pallas as pl
from jax.experimental.pallas import tpu as pltpu

F32 = jnp.float32
BF16 = jnp.bfloat16

NORM_EPS = 1e-6
GN_EPS = 64e-5
L2_EPS = 1e-6
CONV_TAPS = 4

LANES = 128
SUBLANES = 8
V7X_VMEM_LIMIT_BYTES = 56 * 1024 * 1024

DN_CHUNK = 64
RW_CHUNK = 64
RW_HEAD = 64
EXP_NEG_HALF = math.exp(-0.5)
MIX_PROLOGUE_ROWS = 256
MIX_DOT_ROWS = 512


def _params(*semantics):
    return pltpu.CompilerParams(dimension_semantics=semantics, vmem_limit_bytes=V7X_VMEM_LIMIT_BYTES)


def _tile(dim, pref):
    t = min(dim, pref)
    assert dim % t == 0, (dim, pref)
    return t


def _dot(a, b):
    return jnp.dot(a, b, preferred_element_type=F32)


def _dot_nt(a, b):
    return lax.dot_general(a, b, (((1,), (1,)), ((), ())), preferred_element_type=F32)


def _bdot_nt(a, b):
    return _dot_nt(a.astype(BF16), b.astype(BF16))


def _dot01(m01, x):
    x1 = x.astype(BF16)
    x2 = (x - x1.astype(F32)).astype(BF16)
    return _dot(m01, x1) + _dot(m01, x2)


def _silu(x):
    return x * jax.nn.sigmoid(x)


def _softplus(x):
    return jnp.maximum(x, 0.0) + jnp.log1p(jnp.exp(-jnp.abs(x)))


def _iota2(shape, dim):
    return lax.broadcasted_iota(jnp.int32, shape, dim)


def _shift_rows(x, halo, j):
    full = pltpu.roll(x, j, axis=0)
    hfix = pltpu.roll(halo, j, axis=0)
    top = jnp.where(_iota2(hfix.shape, 0) < j, hfix, full[0:SUBLANES])
    return jnp.concatenate([top, full[SUBLANES:]], axis=0)


def _rms(x, g):
    return x * lax.rsqrt(jnp.mean(x * x, axis=-1, keepdims=True) + NORM_EPS) * g


def _norm_mm_kernel(x_ref, g_ref, w_ref, wn_ref, o_ref, on_ref, hn_s):
    j = pl.program_id(1)

    @pl.when(j == 0)
    def _():
        hn_s[...] = _rms(x_ref[...], g_ref[...]).astype(BF16)
        on_ref[...] = _dot(hn_s[...], wn_ref[...])

    @pl.when(j > 0)
    def _():
        o_ref[...] = _dot(hn_s[...], w_ref[...])


def _norm_matmul(h, gain, w, w_narrow):
    m, d = h.shape
    n = w.shape[1]
    tm = _tile(m, 1024)
    tn = _tile(n, 1024)
    n_main = n // tn

    def col(j):
        return jnp.maximum(j - 1, 0)

    return pl.pallas_call(
        _norm_mm_kernel,
        grid=(m // tm, n_main + 1),
        in_specs=[pl.BlockSpec((tm, d), lambda i, j: (i, 0)),
                  pl.BlockSpec((1, d), lambda i, j: (0, 0)),
                  pl.BlockSpec((d, tn), lambda i, j: (0, col(j))),
                  pl.BlockSpec((d, LANES), lambda i, j: (0, 0))],
        out_specs=[pl.BlockSpec((tm, tn), lambda i, j: (i, col(j))),
                   pl.BlockSpec((tm, LANES), lambda i, j: (i, 0))],
        out_shape=[jax.ShapeDtypeStruct((m, n), F32), jax.ShapeDtypeStruct((m, LANES), F32)],
        scratch_shapes=[pltpu.VMEM((tm, d), BF16)],
        compiler_params=_params("parallel", "arbitrary"),
        name="norm_matmul",
    )(h, gain.reshape(1, d), w, w_narrow)


def _mix_mm_kernel(x_ref, halo_ref, g_ref, mu_ref, w_ref, a_ref, o_ref, mid_ref, hn_s, xx_s, *,
                   tiles_per_group, tiles_per_seq, n_groups):
    i = pl.program_id(0)
    j = pl.program_id(1)
    tm = hn_s.shape[0]
    rc = min(tm, MIX_PROLOGUE_ROWS)
    mc = min(tm, MIX_DOT_ROWS)

    def mix(rows, mu_row):
        return hn_s[rows, :] + xx_s[rows, :] * mu_row.astype(BF16)

    @pl.when(j == 0)
    def _():
        g = g_ref[...]
        first = lax.rem(i, tiles_per_seq) == 0
        for r in range(tm // rc):
            rows = slice(r * rc, (r + 1) * rc)
            hn = _rms(x_ref[0, rows, :], g)
            if r == 0:
                hp = jnp.where(first, 0.0, _rms(halo_ref[0], g))
            else:
                hp = _rms(x_ref[0, r * rc - SUBLANES:r * rc, :], g)
            hn_s[rows, :] = hn.astype(BF16)
            xx_s[rows, :] = (_shift_rows(hn, hp, 1) - hn).astype(BF16)

        for l in range(a_ref.shape[0]):
            mu_row = mu_ref[n_groups + l:n_groups + l + 1, :]
            for r in range(tm // mc):
                rows = slice(r * mc, (r + 1) * mc)
                mid_ref[rows, l * LANES:(l + 1) * LANES] = _dot(mix(rows, mu_row), a_ref[l])

    @pl.when(j > 0)
    def _():
        mu_row = mu_ref[pl.ds(lax.div(j - 1, tiles_per_group), 1), :]
        for r in range(tm // mc):
            rows = slice(r * mc, (r + 1) * mc)
            o_ref[rows, :] = _dot(mix(rows, mu_row), w_ref[0])


def _mix_matmul(h, gain, mu, w, a_down):
    b, t, d = h.shape
    m = b * t
    g, _, n = w.shape
    nl = a_down.shape[0]
    tm = _tile(t, 1024)
    tn = _tile(n, 1024)
    tpg = n // tn
    n_main = g * tpg
    tps = t // tm
    hb = tm // SUBLANES

    def col(j):
        return jnp.maximum(j - 1, 0)

    return pl.pallas_call(
        functools.partial(_mix_mm_kernel, tiles_per_group=tpg, tiles_per_seq=tps, n_groups=g),
        grid=(m // tm, n_main + 1),
        in_specs=[pl.BlockSpec((1, tm, d), lambda i, j: (i // tps, i % tps, 0)),
                  pl.BlockSpec((1, SUBLANES, d), lambda i, j: (i // tps, jnp.maximum((i % tps) * hb - 1, 0), 0)),
                  pl.BlockSpec((1, d), lambda i, j: (0, 0)),
                  pl.BlockSpec((g + nl, d), lambda i, j: (0, 0)),
                  pl.BlockSpec((1, d, tn), lambda i, j: (col(j) // tpg, 0, col(j) % tpg)),
                  pl.BlockSpec((nl, d, LANES), lambda i, j: (0, 0, 0))],
        out_specs=[pl.BlockSpec((tm, tn), lambda i, j: (i, col(j))),
                   pl.BlockSpec((tm, nl * LANES), lambda i, j: (i, 0))],
        out_shape=[jax.ShapeDtypeStruct((m, g * n), F32), jax.ShapeDtypeStruct((m, nl * LANES), F32)],
        scratch_shapes=[pltpu.VMEM((tm, d), BF16), pltpu.VMEM((tm, d), BF16)],
        compiler_params=_params("parallel", "arbitrary"),
        name="mix_matmul",
    )(h, h, gain.reshape(1, d), mu, w, a_down)


def _lora_up_kernel(mid_ref, b_ref, o_ref, *, use_tanh):
    mid = mid_ref[...]
    if use_tanh:
        mid = jnp.tanh(mid)
    o_ref[...] = _dot(mid.astype(BF16), b_ref[...])


def _lora_up(mids, sel, b_up, use_tanh):
    m = mids.shape[0]
    rank, n = b_up.shape
    b_p = jnp.pad(b_up, ((0, LANES - rank), (0, 0))).astype(BF16)
    tm = _tile(m, 1024)
    return pl.pallas_call(
        functools.partial(_lora_up_kernel, use_tanh=use_tanh),
        grid=(m // tm,),
        in_specs=[pl.BlockSpec((tm, LANES), lambda i: (i, sel)),
                  pl.BlockSpec((LANES, n), lambda i: (0, 0))],
        out_specs=pl.BlockSpec((tm, n), lambda i: (i, 0)),
        out_shape=jax.ShapeDtypeStruct((m, n), F32),
        compiler_params=_params("parallel"),
        name="lora_up",
    )(mids, b_p)


def _out_mm_kernel(*refs, n_x, final_norm):
    x_refs, w_ref, r_ref = refs[:n_x], refs[n_x], refs[n_x + 1]
    acc = r_ref[...]
    k0 = 0
    for x_ref in x_refs:
        kx = x_ref.shape[1]
        acc = acc + _dot(x_ref[...], w_ref[k0:k0 + kx, :])
        k0 += kx
    if final_norm:
        g_ref, o_ref = refs[n_x + 2], refs[n_x + 3]
        o_ref[...] = _rms(acc, g_ref[...])
    else:
        refs[n_x + 2][...] = acc


def _out_matmul(xs, w, residual, final_gain=None):
    m = xs[0].shape[0]
    k, n = w.shape
    tm = _tile(m, 512)
    tn = _tile(n, 2048)
    in_specs = [pl.BlockSpec((tm, x.shape[1]), lambda i, j: (i, 0)) for x in xs]
    in_specs += [pl.BlockSpec((k, tn), lambda i, j: (0, j)), pl.BlockSpec((tm, tn), lambda i, j: (i, j))]
    args = [*xs, w, residual]
    if final_gain is not None:
        assert tn == n
        in_specs.append(pl.BlockSpec((1, n), lambda i, j: (0, 0)))
        args.append(final_gain.reshape(1, n))
    return pl.pallas_call(
        functools.partial(_out_mm_kernel, n_x=len(xs), final_norm=final_gain is not None),
        grid=(m // tm, n // tn),
        in_specs=in_specs,
        out_specs=pl.BlockSpec((tm, tn), lambda i, j: (i, j)),
        out_shape=jax.ShapeDtypeStruct((m, n), F32),
        compiler_params=_params("parallel", "parallel"),
        name="out_matmul",
    )(*args)


_DN_F128 = dict(q=0, k=1, vb=2, bc=3, g=4, u0=5, o=6)
_DN_F64 = dict(p=0, dmat=1)


def _dn_kernel(q_ref, k_ref, v_ref, qh_ref, kh_ref, vh_ref, z_ref, ba_ref, sel_ref, hp_ref, cw_ref, on_ref,
               o_ref, s_ref, sb_s, f128, wq_s, ub_s, nc_s, f64, rhs_s, akd_s, cd_s):
    t = pl.program_id(2)
    tc = q_ref.shape[1]
    heads = s_ref.shape[0]
    c = DN_CHUNK
    qf, qp = _DN_F128, _DN_F64

    @pl.when(t == 0)
    def _():
        s_ref[...] = jnp.zeros_like(s_ref)
        sb_s[...] = jnp.zeros_like(sb_s)

    row = _iota2((c, c), 0)
    col = _iota2((c, c), 1)
    causal = row >= col
    strict = row > col
    strict_f = strict.astype(F32)
    eye = (row == col).astype(F32)
    tril = causal.astype(BF16)
    onorm = on_ref[...]

    bodies = [(ci, hh) for ci in range(tc // c) for hh in range(heads)]

    def where(ci, hh):
        return ci * heads + hh, slice(ci * c, (ci + 1) * c), slice(hh * LANES, (hh + 1) * LANES)

    def conv_silu(x_ref, h_ref, ci, rs, ls, w):
        x = x_ref[0, rs, ls]
        if ci == 0:
            halo = jnp.where(t > 0, h_ref[0, :, ls], 0.0)
        else:
            halo = x_ref[0, ci * c - SUBLANES:ci * c, ls]
        y = x * w[CONV_TAPS - 1:CONV_TAPS, :]
        for j in range(1, CONV_TAPS):
            y = y + _shift_rows(x, halo, j) * w[CONV_TAPS - 1 - j:CONV_TAPS - j, :]
        return _silu(y)

    def l2n(x):
        return x * lax.rsqrt(jnp.sum(x * x, axis=-1, keepdims=True) + L2_EPS)

    for ci, hh in bodies:
        i, rs, ls = where(ci, hh)
        cw = cw_ref[hh]
        sel = sel_ref[hh]
        hp = hp_ref[hh]
        ba = ba_ref[0, rs, :]
        qc = l2n(conv_silu(q_ref, qh_ref, ci, rs, ls, cw[0:CONV_TAPS])) * (LANES ** -0.5)
        kc = l2n(conv_silu(k_ref, kh_ref, ci, rs, ls, cw[CONV_TAPS:2 * CONV_TAPS]))
        vc = conv_silu(v_ref, vh_ref, ci, rs, ls, cw[2 * CONV_TAPS:3 * CONV_TAPS])
        beta = jax.nn.sigmoid(jnp.sum(ba * sel[0:1], axis=-1, keepdims=True))
        alpha_pre = jnp.sum(ba * sel[1:2], axis=-1, keepdims=True)
        bc = jnp.broadcast_to(beta, (c, LANES))
        lac = -jnp.exp(hp[0:1]) * _softplus(jnp.broadcast_to(alpha_pre, (c, LANES)) + hp[1:2])
        f128[i, qf["q"]] = qc
        f128[i, qf["k"]] = kc
        f128[i, qf["vb"]] = vc * bc
        f128[i, qf["bc"]] = bc
        f128[i, qf["g"]] = _dot01(tril, lac)
        f64[i, qp["dmat"]] = _dot01(tril, lac[:, 0:c] * strict_f)

    for ci, hh in bodies:
        i, rs, ls = where(ci, hh)
        qc = f128[i, qf["q"]]
        kc = f128[i, qf["k"]]
        bc = f128[i, qf["bc"]]
        g = f128[i, qf["g"]]
        decay = jnp.where(causal, jnp.exp(jnp.where(causal, f64[i, qp["dmat"]], 0.0)), 0.0)
        kb = kc.astype(BF16)
        qkk = _dot_nt(jnp.concatenate([qc.astype(BF16), kb], axis=0), kb)
        lower = jnp.where(strict, bc[:, 0:c] * qkk[c:] * decay, 0.0)
        gamma = jnp.exp(g)
        glast = g[c - 1:c, :]
        akd_s[i, 0:c] = jnp.where(causal, qkk[0:c] * decay, 0.0).astype(BF16)
        nc_s[i] = (-lower).astype(BF16)
        f64[i, qp["p"]] = eye - lower
        rhs_s[i] = jnp.concatenate([f128[i, qf["vb"]], kc * (bc * gamma)], axis=1).astype(BF16)
        wq_s[i, c:] = (qc * gamma).astype(BF16)
        akd_s[i, c:] = (kc * jnp.exp(glast - g)).T.astype(BF16)
        cd_s[i] = jnp.broadcast_to(jnp.exp(glast), (SUBLANES, LANES))

    for _ in range(5):
        for ci, hh in bodies:
            i, _, _ = where(ci, hh)
            nc = nc_s[i]
            nc_s[i] = _dot(nc, nc).astype(BF16)
        for ci, hh in bodies:
            i, _, _ = where(ci, hh)
            p = f64[i, qp["p"]]
            f64[i, qp["p"]] = p + _dot(p.astype(BF16), nc_s[i])

    for ci, hh in bodies:
        i, _, _ = where(ci, hh)
        sol = _dot(f64[i, qp["p"]].astype(BF16), rhs_s[i])
        f128[i, qf["u0"]] = sol[:, 0:LANES]
        wq_s[i, 0:c] = sol[:, LANES:].astype(BF16)

    for ci in range(tc // c):
        for hh in range(heads):
            i, _, _ = where(ci, hh)
            wq = _dot(wq_s[i], sb_s[hh])
            ub_s[i] = (f128[i, qf["u0"]] - wq[0:c]).astype(BF16)
            f128[i, qf["o"]] = wq[c:]
        for hh in range(heads):
            i, _, _ = where(ci, hh)
            ou = _dot(akd_s[i], ub_s[i])
            f128[i, qf["o"]] = f128[i, qf["o"]] + ou[0:c]
            s_new = cd_s[i][0:1] * s_ref[hh] + ou[c:]
            s_ref[hh] = s_new
            sb_s[hh] = s_new.astype(BF16)

    for ci, hh in bodies:
        i, rs, ls = where(ci, hh)
        o = f128[i, qf["o"]]
        on = o * lax.rsqrt(jnp.mean(o * o, axis=-1, keepdims=True) + NORM_EPS) * onorm
        o_ref[0, rs, ls] = (on * _silu(z_ref[0, rs, ls])).astype(o_ref.dtype)


def _deltanet(p, ba, conv_w, a_log, dt_bias, out_norm, nheads):
    b, t, _ = p.shape
    tc = _tile(t, 256)
    c = DN_CHUNK
    assert tc % c == 0
    hb = tc // SUBLANES
    nh = nheads
    hg = _tile(nh, 8)
    ng = nh // hg
    wd = hg * LANES
    nbody = (tc // c) * hg
    cw = conv_w.reshape(CONV_TAPS, 3, nh, LANES).transpose(2, 1, 0, 3).reshape(nh, 3 * CONV_TAPS, LANES)
    lane = jnp.arange(LANES)[None, None, :]
    head = jnp.arange(nh)[:, None, None]
    sel = jnp.concatenate([(lane == head), (lane == head + nh)], axis=1).astype(F32)
    hp = jnp.broadcast_to(jnp.stack([a_log, dt_bias], axis=1)[:, :, None], (nh, 2, LANES)).astype(F32)

    def main(sec):
        return pl.BlockSpec((1, tc, wd), lambda i, h, j: (i, j, sec * ng + h))

    def halo(sec):
        return pl.BlockSpec((1, SUBLANES, wd), lambda i, h, j: (i, jnp.maximum(j * hb - 1, 0), sec * ng + h))

    per_head = lambda rows: pl.BlockSpec((hg, rows, LANES), lambda i, h, j: (h, 0, 0))
    return pl.pallas_call(
        _dn_kernel,
        grid=(b, ng, t // tc),
        in_specs=[main(0), main(1), main(2), halo(0), halo(1), halo(2), main(3),
                  pl.BlockSpec((1, tc, LANES), lambda i, h, j: (i, j, 0)),
                  per_head(2), per_head(2), per_head(3 * CONV_TAPS),
                  pl.BlockSpec((1, LANES), lambda i, h, j: (0, 0))],
        out_specs=pl.BlockSpec((1, tc, wd), lambda i, h, j: (i, j, h)),
        out_shape=jax.ShapeDtypeStruct((b, t, nh * LANES), BF16),
        scratch_shapes=[pltpu.VMEM((hg, LANES, LANES), F32),
                        pltpu.VMEM((hg, LANES, LANES), BF16),
                        pltpu.VMEM((nbody, len(_DN_F128), c, LANES), F32),
                        pltpu.VMEM((nbody, 2 * c, LANES), BF16),
                        pltpu.VMEM((nbody, c, LANES), BF16),
                        pltpu.VMEM((nbody, c, c), BF16),
                        pltpu.VMEM((nbody, len(_DN_F64), c, c), F32),
                        pltpu.VMEM((nbody, c, 2 * LANES), BF16),
                        pltpu.VMEM((nbody, c + LANES, c), BF16),
                        pltpu.VMEM((nbody, SUBLANES, LANES), F32)],
        compiler_params=_params("parallel", "parallel", "arbitrary"),
        name="deltanet",
    )(p, p, p, p, p, p, p, ba, sel, hp, cw, out_norm.reshape(1, LANES))


HG_LEVELS = 6
HG_BLOCK = 64


def _hg_tables():
    import numpy as np
    c = HG_BLOCK
    t = np.arange(c)[:, None]
    i = np.arange(c)[None, :]
    rows = [i <= t, i > t]
    q_rows, k_rows = [], []
    level = np.full((c, c), -1, np.int32)
    level[np.arange(c), np.arange(c)] = HG_LEVELS
    for li in range(HG_LEVELS):
        m = c >> (li + 1)
        pos = t % (2 * m)
        upper = pos >= m
        ref = t - pos + m - 1
        q_rows.append(upper & (i > ref) & (i <= t))
        k_rows.append((~upper) & (i > t) & (i <= ref))
        same = (t // (2 * m)) == (i // (2 * m))
        level[same & upper & ((i % (2 * m)) < m)] = li
    table = np.concatenate(rows + q_rows + k_rows, axis=0).astype(np.float32)
    return table, level


def _hg_kernel(q_ref, f_ref, i_ref, z_ref, lb_ref, on_ref, tab_ref, lvl_ref, o_ref,
               s_ref, sb_s, qt_s, kt_s, qb_s, vb_s, klt_s, att_s, ecol_s, inc_s, o_s, *, layer):
    t = pl.program_id(2)
    tc = q_ref.shape[1]
    heads = s_ref.shape[0]
    c = HG_BLOCK
    nl = HG_LEVELS

    @pl.when(t == 0)
    def _():
        s_ref[...] = jnp.zeros_like(s_ref)
        sb_s[...] = jnp.zeros_like(sb_s)

    table = tab_ref[...]
    level = lvl_ref[...]
    onorm = on_ref[...]
    bodies = [(ci, hh) for ci in range(tc // c) for hh in range(heads)]

    def where(ci, hh):
        return ci * heads + hh, slice(ci * c, (ci + 1) * c), slice(hh * LANES, (hh + 1) * LANES)

    for ci, hh in bodies:
        i, rs, ls = where(ci, hh)
        logits = lb_ref[:, ls]
        e = jnp.exp(logits - jnp.max(logits, axis=0, keepdims=True))
        share = e / jnp.sum(e, axis=0, keepdims=True)
        lbound = jnp.sum(share[0:layer + 1], axis=0, keepdims=True) - share[0:1]
        q = _silu(q_ref[0, rs, ls]) * (LANES ** -0.5)
        forget = lbound + (1.0 - lbound) * jax.nn.sigmoid(f_ref[0, rs, ls])
        k = 1.0 - forget
        d = _dot(table, jnp.log(forget).astype(BF16))
        b = d[0:c]
        qb_s[i] = (q * jnp.exp(b)).astype(BF16)
        klt_s[i] = (k * jnp.exp(d[c:2 * c])).T.astype(BF16)
        ecol_s[i] = jnp.broadcast_to(jnp.exp(b[c - 1:c]), (LANES, LANES)).T
        for li in range(nl):
            qt_s[i, li] = (q * jnp.exp(d[(2 + li) * c:(3 + li) * c])).astype(BF16)
            kt_s[i, li] = (k * jnp.exp(d[(2 + nl + li) * c:(3 + nl + li) * c])).astype(BF16)
        qt_s[i, nl] = q.astype(BF16)
        kt_s[i, nl] = k.astype(BF16)
        vb_s[i] = i_ref[0, rs, ls].astype(BF16)

    for ci, hh in bodies:
        i, _, _ = where(ci, hh)
        att = jnp.zeros((c, c), F32)
        for li in range(nl + 1):
            att = att + jnp.where(level == li, _dot_nt(qt_s[i, li], kt_s[i, li]), 0.0)
        att_s[i] = att.astype(BF16)
        inc_s[i] = _dot(klt_s[i], vb_s[i])

    for ci, hh in bodies:
        i, _, _ = where(ci, hh)
        o_s[i] = _dot(att_s[i], vb_s[i])

    for ci, hh in bodies:
        i, _, _ = where(ci, hh)
        o_s[i] = o_s[i] + _dot(qb_s[i], sb_s[hh])
        s_new = ecol_s[i] * s_ref[hh] + inc_s[i]
        s_ref[hh] = s_new
        sb_s[hh] = s_new.astype(BF16)

    for ci, hh in bodies:
        i, rs, ls = where(ci, hh)
        o = o_s[i]
        on = o * lax.rsqrt(jnp.mean(o * o, axis=-1, keepdims=True) + NORM_EPS) * onorm
        o_ref[0, rs, ls] = (on * _silu(z_ref[0, rs, ls])).astype(o_ref.dtype)


def _hgrn2(p, col0, lb_logits, out_norm, nheads, layer):
    b, t, _ = p.shape
    tc = _tile(t, 256)
    c = HG_BLOCK
    assert tc % c == 0
    nh = nheads
    hg = _tile(nh, 8)
    ng = nh // hg
    wd = hg * LANES
    assert col0 % hg == 0
    cg = col0 // hg
    nbody = (tc // c) * hg
    table, level = _hg_tables()
    nrow = table.shape[0]

    def main(sec):
        return pl.BlockSpec((1, tc, wd), lambda i, h, j: (i, j, cg + sec * ng + h))

    nl = lb_logits.shape[0]
    const = lambda shape: pl.BlockSpec(shape, lambda i, h, j: (0, 0))
    return pl.pallas_call(
        functools.partial(_hg_kernel, layer=layer),
        grid=(b, ng, t // tc),
        in_specs=[main(0), main(1), main(2), main(3),
                  pl.BlockSpec((nl, wd), lambda i, h, j: (0, h)),
                  const((1, LANES)), const((nrow, c)), const((c, c))],
        out_specs=pl.BlockSpec((1, tc, wd), lambda i, h, j: (i, j, h)),
        out_shape=jax.ShapeDtypeStruct((b, t, nh * LANES), BF16),
        scratch_shapes=[pltpu.VMEM((hg, LANES, LANES), F32),
                        pltpu.VMEM((hg, LANES, LANES), BF16),
                        pltpu.VMEM((nbody, HG_LEVELS + 1, c, LANES), BF16),
                        pltpu.VMEM((nbody, HG_LEVELS + 1, c, LANES), BF16),
                        pltpu.VMEM((nbody, c, LANES), BF16),
                        pltpu.VMEM((nbody, c, LANES), BF16),
                        pltpu.VMEM((nbody, LANES, c), BF16),
                        pltpu.VMEM((nbody, c, c), BF16),
                        pltpu.VMEM((nbody, LANES, LANES), F32),
                        pltpu.VMEM((nbody, LANES, LANES), F32),
                        pltpu.VMEM((nbody, c, LANES), F32)],
        compiler_params=_params("parallel", "parallel", "arbitrary"),
        name="hgrn2",
    )(p, p, p, p, lb_logits, out_norm.reshape(1, LANES), jnp.asarray(table, BF16), jnp.asarray(level))


_RW_B64 = dict(aak=0, vb=1, arb=2, ark=3, sab=4)
_RW_B128 = dict(nb=0, vbd=1, bkt=2, atbd=3, avbd=4)
_RW_F64 = dict(p=0, uc=1, g=2, bv=3, lw=4, kkr=5, k2=6, ag=7, v=8, ssq=9, y=10, d=11)


def _rw_kernel(*refs, has_vres):
    if has_vres:
        (r_ref, k_ref, v_ref, z_ref, wl_ref, al_ref, vl_ref, vf_ref, prm_ref,
         o_ref, h_ref, hb_s, s64, s128, wr_s, np_s, f64, f128) = refs
    else:
        (r_ref, k_ref, v_ref, z_ref, wl_ref, al_ref, prm_ref,
         o_ref, h_ref, hb_s, s64, s128, wr_s, np_s, f64, f128) = refs
    t = pl.program_id(2)
    tc = o_ref.shape[1]
    pairs = h_ref.shape[0]
    c = RW_CHUNK
    nchunk = tc // c
    b64, b128, q64 = _RW_B64, _RW_B128, _RW_F64

    @pl.when(t == 0)
    def _():
        h_ref[...] = jnp.zeros_like(h_ref)
        hb_s[...] = jnp.zeros_like(hb_s)

    lane1 = _iota2((1, LANES), 1)
    head0 = lane1 < RW_HEAD
    m0 = head0.astype(F32)
    m1 = 1.0 - m0
    same_head = (_iota2((LANES, LANES), 0) < RW_HEAD) == (_iota2((LANES, LANES), 1) < RW_HEAD)
    bdmask = same_head.astype(F32)
    row = _iota2((c, LANES), 0)
    scol = jnp.bitwise_and(_iota2((c, LANES), 1), RW_HEAD - 1)
    incl = row >= scol
    strict = row > scol
    eye_cat = (row == scol).astype(F32)
    tril = (_iota2((c, c), 0) >= _iota2((c, c), 1)).astype(BF16)

    def bd(x):
        return jnp.concatenate([x * m0, x * m1], axis=0).astype(BF16)

    def gsum(x):
        s0 = jnp.sum(jnp.where(head0, x, 0.0), axis=-1, keepdims=True)
        s1 = jnp.sum(x, axis=-1, keepdims=True) - s0
        return jnp.where(head0, s0, s1)

    bodies = [(ci, pp) for ci in range(nchunk) for pp in range(pairs)]

    def where(ci, pp):
        return ci * pairs + pp, slice(ci * c, (ci + 1) * c), slice(pp * LANES, (pp + 1) * LANES)

    for ci, pp in bodies:
        i, rs, ls = where(ci, pp)
        prm = prm_ref[:, ls]
        w0, a0, k_k, k_a, r_k, v0 = prm[0:1], prm[1:2], prm[2:3], prm[3:4], prm[4:5], prm[7:8]
        r = r_ref[0, rs, ls]
        k = k_ref[0, rs, ls]
        v = v_ref[0, rs, ls]
        lw = -EXP_NEG_HALF * jax.nn.sigmoid(w0 + wl_ref[0, rs, ls])
        ag = jax.nn.sigmoid(a0 + al_ref[0, rs, ls])
        if has_vres:
            v = v + (vf_ref[0, rs, ls] - v) * jax.nn.sigmoid(v0 + vl_ref[0, rs, ls])
        kkr = k * k_k
        k2 = k * (1.0 + (ag - 1.0) * k_a)
        sums = gsum(jnp.concatenate([kkr * kkr, r * k2 * r_k], axis=0))
        f64[i, q64["g"]] = _dot01(tril, lw)
        f64[i, q64["lw"]] = lw
        f64[i, q64["ag"]] = ag
        f64[i, q64["v"]] = v
        f64[i, q64["kkr"]] = kkr
        f64[i, q64["k2"]] = k2
        f64[i, q64["ssq"]] = sums[0:c]
        f64[i, q64["bv"]] = sums[c:] * v

    for ci, pp in bodies:
        i, rs, ls = where(ci, pp)
        g = f64[i, q64["g"]]
        lw = f64[i, q64["lw"]]
        v = f64[i, q64["v"]]
        kk = f64[i, q64["kkr"]] * lax.rsqrt(f64[i, q64["ssq"]] + L2_EPS)
        e_g = jnp.exp(g)
        e_gn = jnp.exp(-g)
        rt = r_ref[0, rs, ls] * e_g
        at = -kk * jnp.exp(g - lw)
        kt = f64[i, q64["k2"]] * e_gn
        bt = kk * f64[i, q64["ag"]] * e_gn
        x = _bdot_nt(jnp.concatenate([at, rt], axis=0), jnp.concatenate([bd(kt), bd(bt)], axis=0))
        a_ab = jnp.where(strict, x[0:c, LANES:], 0.0)
        e_last = e_g[c - 1:c, :]
        s64[i, b64["aak"]] = jnp.where(strict, x[0:c, 0:LANES], 0.0).astype(BF16)
        s64[i, b64["ark"]] = jnp.where(incl, x[c:, 0:LANES], 0.0).astype(BF16)
        s64[i, b64["arb"]] = jnp.where(incl, x[c:, LANES:], 0.0).astype(BF16)
        np_s[i, 0:c] = a_ab.astype(BF16)
        np_s[i, c:] = (eye_cat + a_ab).astype(BF16)
        s128[i, b128["nb"]] = bd(a_ab)
        f64[i, q64["p"]] = eye_cat + a_ab
        wr_s[i, c:] = rt.astype(BF16)
        s64[i, b64["vb"]] = v.astype(BF16)
        s128[i, b128["atbd"]] = bd(at)
        s128[i, b128["vbd"]] = bd(v)
        s128[i, b128["bkt"]] = jnp.concatenate([bt * e_last, kt * e_last], axis=0).T.astype(BF16)
        f128[i] = jnp.broadcast_to(e_last, (LANES, LANES)).T

    for ci, pp in bodies:
        i, _, _ = where(ci, pp)
        s128[i, b128["avbd"]] = bd(_dot(s64[i, b64["aak"]], s128[i, b128["vbd"]]))

    for lvl in range(6):
        for ci, pp in bodies:
            i, _, _ = where(ci, pp)
            lhs = np_s[i, 0:c] if lvl == 0 else (np_s[i, c:] if lvl == 5 else np_s[i])
            out = _dot(lhs, s128[i, b128["nb"]])
            if lvl > 0:
                p = f64[i, q64["p"]] + out[-c:]
                f64[i, q64["p"]] = p
                np_s[i, c:] = p.astype(BF16)
            if lvl < 5:
                n2 = out[0:c]
                np_s[i, 0:c] = n2.astype(BF16)
                s128[i, b128["nb"]] = bd(n2)

    for ci, pp in bodies:
        i, _, _ = where(ci, pp)
        rhs = jnp.concatenate([s128[i, b128["atbd"]], s128[i, b128["avbd"]]], axis=1)
        wu = _dot(f64[i, q64["p"]].astype(BF16), rhs)
        wr_s[i, 0:c] = wu[:, 0:LANES].astype(BF16)
        f64[i, q64["uc"]] = wu[:, LANES:]

    for ci in range(nchunk):
        for pp in range(pairs):
            i, _, _ = where(ci, pp)
            wr = _dot(wr_s[i], hb_s[pp])
            sa = wr[0:c] + f64[i, q64["uc"]]
            s64[i, b64["sab"]] = sa.astype(BF16)
            s128[i, b128["nb"]] = bd(sa)
            f64[i, q64["y"]] = wr[c:]
        for pp in range(pairs):
            i, _, _ = where(ci, pp)
            f64[i, q64["y"]] = f64[i, q64["y"]] + _dot(
                jnp.concatenate([s64[i, b64["arb"]], s64[i, b64["ark"]]], axis=1),
                jnp.concatenate([s128[i, b128["nb"]], s128[i, b128["vbd"]]], axis=0))
            h_new = f128[i] * h_ref[pp] + bdmask * _dot(
                s128[i, b128["bkt"]], jnp.concatenate([s64[i, b64["sab"]], s64[i, b64["vb"]]], axis=0))
            h_ref[pp] = h_new
            hb_s[pp] = h_new.astype(BF16)

    for ci, pp in bodies:
        i, _, _ = where(ci, pp)
        y = f64[i, q64["y"]]
        f64[i, q64["d"]] = y - gsum(y) * (1.0 / RW_HEAD)
    for ci, pp in bodies:
        i, rs, ls = where(ci, pp)
        prm = prm_ref[:, ls]
        d = f64[i, q64["d"]]
        var = gsum(d * d) * (1.0 / RW_HEAD)
        yn = d * lax.rsqrt(var + GN_EPS) * prm[5:6] + prm[6:7]
        o_ref[0, rs, ls] = ((yn + f64[i, q64["bv"]]) * _silu(z_ref[0, rs, ls])).astype(o_ref.dtype)


def _rwkv7(rkvz, wl, al, vl, v_first, prm):
    b, t, d4 = rkvz.shape
    d = d4 // 4
    tc = _tile(t, 256)
    has_vres = vl is not None
    pg = _tile(d // LANES, 8)
    wd = pg * LANES
    c = RW_CHUNK
    nbody = (tc // c) * pg
    ngrp = d // wd

    def sec(s):
        return pl.BlockSpec((1, tc, wd), lambda i, p, j: (i, j, s * ngrp + p))

    flat = pl.BlockSpec((1, tc, wd), lambda i, p, j: (i, j, p))
    in_specs = [sec(0), sec(1), sec(2), sec(3), flat, flat]
    args = [rkvz, rkvz, rkvz, rkvz, wl, al]
    if has_vres:
        in_specs += [flat, sec(2)]
        args += [vl, v_first]
    in_specs.append(pl.BlockSpec((8, wd), lambda i, p, j: (0, p)))
    args.append(prm)
    return pl.pallas_call(
        functools.partial(_rw_kernel, has_vres=has_vres),
        grid=(b, d // wd, t // tc),
        in_specs=in_specs,
        out_specs=flat,
        out_shape=jax.ShapeDtypeStruct((b, t, d), BF16),
        scratch_shapes=[pltpu.VMEM((pg, LANES, LANES), F32),
                        pltpu.VMEM((pg, LANES, LANES), BF16),
                        pltpu.VMEM((nbody, len(_RW_B64), c, LANES), BF16),
                        pltpu.VMEM((nbody, len(_RW_B128), 2 * c, LANES), BF16),
                        pltpu.VMEM((nbody, 2 * c, LANES), BF16),
                        pltpu.VMEM((nbody, 2 * c, LANES), BF16),
                        pltpu.VMEM((nbody, len(_RW_F64), c, LANES), F32),
                        pltpu.VMEM((nbody, LANES, LANES), F32)],
        compiler_params=_params("parallel", "parallel", "arbitrary"),
        name="rwkv7",
    )(*args)


def kernel(x, norm_gains, mix_w_in, dn_conv, dn_a_log, dn_dt_bias, dn_out_norm, hg_lb_logits, hg_out_norm,
           mix_w_out, rw_mu, rw_w_rkvz, rw_w0, rw_w1, rw_w2, rw_a0, rw_a1, rw_a2, rw_v0, rw_v1, rw_v2,
           rw_k_k, rw_k_a, rw_r_k, rw_ln_w, rw_ln_b, rw_w_out, final_norm):
    b, t, d = x.shape
    m = b * t
    depth = norm_gains.shape[0]
    dn_heads = dn_a_log.shape[1]
    dn_width = dn_conv.shape[2] // 3
    hg_width = hg_lb_logits.shape[1]
    hg_heads = hg_width // LANES
    assert dn_width == dn_heads * LANES and hg_out_norm.shape[1] == LANES and rw_r_k.shape[2] == RW_HEAD
    dn_ba = 4 * dn_width
    hg_q = dn_ba + 2 * dn_heads
    assert 2 * dn_heads <= LANES

    h = x.reshape(m, d)
    v_first = None
    for layer in range(depth):
        gain = norm_gains[layer]
        if layer % 2 == 0:
            e = layer // 2
            w_in = mix_w_in[e]
            w_main = jnp.concatenate([w_in[:, :dn_ba], w_in[:, hg_q:]], axis=1).astype(BF16)
            w_ba = jnp.pad(w_in[:, dn_ba:hg_q], ((0, 0), (0, LANES - 2 * dn_heads))).astype(BF16)
            p, ba = _norm_matmul(h, gain, w_main, w_ba)
            p = p.reshape(b, t, -1)
            ba = ba.reshape(b, t, LANES)
            o_a = _deltanet(p, ba, dn_conv[e], dn_a_log[e], dn_dt_bias[e], dn_out_norm[e], dn_heads)
            o_b = _hgrn2(p, dn_ba // LANES, hg_lb_logits, hg_out_norm[e], hg_heads, e)
            h = _out_matmul([o_a.reshape(m, -1), o_b.reshape(m, -1)], mix_w_out[e].astype(BF16), h,
                            final_norm if layer == depth - 1 else None)
        else:
            o = layer // 2
            has_vres = v_first is not None
            rows = [0, 2, 3, 5, 1, 4] + ([3] if has_vres else [])
            downs = [rw_w1[o], rw_a1[o]] + ([rw_v1[o - 1]] if has_vres else [])
            a_down = jnp.stack([jnp.pad(a, ((0, 0), (0, LANES - a.shape[1]))) for a in downs]).astype(BF16)
            rkvz, mids = _mix_matmul(h.reshape(b, t, d), gain, rw_mu[o][jnp.array(rows)],
                                     rw_w_rkvz[o].astype(BF16), a_down)
            rkvz = rkvz.reshape(b, t, 4 * d)
            wl = _lora_up(mids, 0, rw_w2[o], True).reshape(b, t, d)
            al = _lora_up(mids, 1, rw_a2[o], False).reshape(b, t, d)
            vl = None
            v0 = jnp.zeros((d,), F32)
            if has_vres:
                vl = _lora_up(mids, 2, rw_v2[o - 1], False).reshape(b, t, d)
                v0 = rw_v0[o - 1]
            else:
                v_first = rkvz
            prm = jnp.stack([rw_w0[o], rw_a0[o], rw_k_k[o], rw_k_a[o], rw_r_k[o].reshape(d),
                             rw_ln_w[o], rw_ln_b[o], v0]).astype(F32)
            y = _rwkv7(rkvz, wl, al, vl, v_first, prm)
            h = _out_matmul([y.reshape(m, d)], rw_w_out[o].astype(BF16), h,
                            final_norm if layer == depth - 1 else None)
    return h.reshape(b, t, d)
```

```python
import functools
import math

import jax
import jax.numpy as jnp
from jax import lax
from jax.experimental import pallas as pl
from jax.experimental.pallas import tpu as pltpu

F32 = jnp.float32
BF16 = jnp.bfloat16

NORM_EPS = 1e-6
GN_EPS = 64e-5
L2_EPS = 1e-6
CONV_TAPS = 4

LANES = 128
SUBLANES = 8
V7X_VMEM_LIMIT_BYTES = 56 * 1024 * 1024

DN_CHUNK = 64
RW_CHUNK = 64
RW_HEAD = 64
EXP_NEG_HALF = math.exp(-0.5)
MIX_PROLOGUE_ROWS = 256
MIX_DOT_ROWS = 512
STAGE_SKEW = 3


def _params(*semantics):
    return pltpu.CompilerParams(dimension_semantics=semantics, vmem_limit_bytes=V7X_VMEM_LIMIT_BYTES)


def _tile(dim, pref):
    t = min(dim, pref)
    assert dim % t == 0, (dim, pref)
    return t


def _dot(a, b):
    return jnp.dot(a, b, preferred_element_type=F32)


def _dot_nt(a, b):
    return lax.dot_general(a, b, (((1,), (1,)), ((), ())), preferred_element_type=F32)


def _bdot_nt(a, b):
    return _dot_nt(a.astype(BF16), b.astype(BF16))


def _dot01(m01, x):
    x1 = x.astype(BF16)
    x2 = (x - x1.astype(F32)).astype(BF16)
    return _dot(m01, x1) + _dot(m01, x2)


def _silu(x):
    return x * jax.nn.sigmoid(x)


def _softplus(x):
    return jnp.maximum(x, 0.0) + jnp.log1p(jnp.exp(-jnp.abs(x)))


def _iota2(shape, dim):
    return lax.broadcasted_iota(jnp.int32, shape, dim)


def _shift_rows(x, halo, j):
    full = pltpu.roll(x, j, axis=0)
    hfix = pltpu.roll(halo, j, axis=0)
    top = jnp.where(_iota2(hfix.shape, 0) < j, hfix, full[0:SUBLANES])
    return jnp.concatenate([top, full[SUBLANES:]], axis=0)


def _emit_skewed(stages, nchunk, width, skew=STAGE_SKEW):
    n = len(stages)
    for tau in range(n + skew * (nchunk - 1)):
        for w in range(width):
            for ci in range(nchunk):
                st = tau - skew * ci
                if 0 <= st < n:
                    stages[st](ci, w)


def _rms(x, g):
    return x * lax.rsqrt(jnp.mean(x * x, axis=-1, keepdims=True) + NORM_EPS) * g


def _norm_mm_kernel(x_ref, g_ref, w_ref, wn_ref, o_ref, on_ref, hn_s):
    j = pl.program_id(1)

    @pl.when(j == 0)
    def _():
        hn_s[...] = _rms(x_ref[...], g_ref[...]).astype(BF16)
        on_ref[...] = _dot(hn_s[...], wn_ref[...])

    @pl.when(j > 0)
    def _():
        o_ref[...] = _dot(hn_s[...], w_ref[...])


def _norm_matmul(h, gain, w, w_narrow):
    m, d = h.shape
    n = w.shape[1]
    tm = _tile(m, 1024)
    tn = _tile(n, 1024)
    n_main = n // tn

    def col(j):
        return jnp.maximum(j - 1, 0)

    return pl.pallas_call(
        _norm_mm_kernel,
        grid=(m // tm, n_main + 1),
        in_specs=[pl.BlockSpec((tm, d), lambda i, j: (i, 0)),
                  pl.BlockSpec((1, d), lambda i, j: (0, 0)),
                  pl.BlockSpec((d, tn), lambda i, j: (0, col(j))),
                  pl.BlockSpec((d, LANES), lambda i, j: (0, 0))],
        out_specs=[pl.BlockSpec((tm, tn), lambda i, j: (i, col(j))),
                   pl.BlockSpec((tm, LANES), lambda i, j: (i, 0))],
        out_shape=[jax.ShapeDtypeStruct((m, n), F32), jax.ShapeDtypeStruct((m, LANES), F32)],
        scratch_shapes=[pltpu.VMEM((tm, d), BF16)],
        compiler_params=_params("parallel", "arbitrary"),
        name="norm_matmul",
    )(h, gain.reshape(1, d), w, w_narrow)


def _mix_mm_kernel(x_ref, halo_ref, g_ref, mu_ref, w_ref, a_ref, o_ref, mid_ref, hn_s, xx_s, *,
                   tiles_per_group, tiles_per_seq, n_groups):
    i = pl.program_id(0)
    j = pl.program_id(1)
    tm = hn_s.shape[0]
    rc = min(tm, MIX_PROLOGUE_ROWS)
    mc = min(tm, MIX_DOT_ROWS)

    def mix(rows, mu_row):
        return hn_s[rows, :] + xx_s[rows, :] * mu_row.astype(BF16)

    @pl.when(j == 0)
    def _():
        g = g_ref[...]
        first = lax.rem(i, tiles_per_seq) == 0
        for r in range(tm // rc):
            rows = slice(r * rc, (r + 1) * rc)
            hn = _rms(x_ref[0, rows, :], g)
            if r == 0:
                hp = jnp.where(first, 0.0, _rms(halo_ref[0], g))
            else:
                hp = _rms(x_ref[0, r * rc - SUBLANES:r * rc, :], g)
            hn_s[rows, :] = hn.astype(BF16)
            xx_s[rows, :] = (_shift_rows(hn, hp, 1) - hn).astype(BF16)

        for l in range(a_ref.shape[0]):
            mu_row = mu_ref[n_groups + l:n_groups + l + 1, :]
            for r in range(tm // mc):
                rows = slice(r * mc, (r + 1) * mc)
                mid_ref[rows, l * LANES:(l + 1) * LANES] = _dot(mix(rows, mu_row), a_ref[l])

    @pl.when(j > 0)
    def _():
        mu_row = mu_ref[pl.ds(lax.div(j - 1, tiles_per_group), 1), :]
        for r in range(tm // mc):
            rows = slice(r * mc, (r + 1) * mc)
            o_ref[rows, :] = _dot(mix(rows, mu_row), w_ref[0])


def _mix_matmul(h, gain, mu, w, a_down):
    b, t, d = h.shape
    m = b * t
    g, _, n = w.shape
    nl = a_down.shape[0]
    tm = _tile(t, 1024)
    tn = _tile(n, 1024)
    tpg = n // tn
    n_main = g * tpg
    tps = t // tm
    hb = tm // SUBLANES

    def col(j):
        return jnp.maximum(j - 1, 0)

    return pl.pallas_call(
        functools.partial(_mix_mm_kernel, tiles_per_group=tpg, tiles_per_seq=tps, n_groups=g),
        grid=(m // tm, n_main + 1),
        in_specs=[pl.BlockSpec((1, tm, d), lambda i, j: (i // tps, i % tps, 0)),
                  pl.BlockSpec((1, SUBLANES, d), lambda i, j: (i // tps, jnp.maximum((i % tps) * hb - 1, 0), 0)),
                  pl.BlockSpec((1, d), lambda i, j: (0, 0)),
                  pl.BlockSpec((g + nl, d), lambda i, j: (0, 0)),
                  pl.BlockSpec((1, d, tn), lambda i, j: (col(j) // tpg, 0, col(j) % tpg)),
                  pl.BlockSpec((nl, d, LANES), lambda i, j: (0, 0, 0))],
        out_specs=[pl.BlockSpec((tm, tn), lambda i, j: (i, col(j))),
                   pl.BlockSpec((tm, nl * LANES), lambda i, j: (i, 0))],
        out_shape=[jax.ShapeDtypeStruct((m, g * n), F32), jax.ShapeDtypeStruct((m, nl * LANES), F32)],
        scratch_shapes=[pltpu.VMEM((tm, d), BF16), pltpu.VMEM((tm, d), BF16)],
        compiler_params=_params("parallel", "arbitrary"),
        name="mix_matmul",
    )(h, h, gain.reshape(1, d), mu, w, a_down)


def _lora_up_kernel(mid_ref, b_ref, o_ref, *, use_tanh):
    mid = mid_ref[...]
    if use_tanh:
        mid = jnp.tanh(mid)
    o_ref[...] = _dot(mid.astype(BF16), b_ref[...])


def _lora_up(mids, sel, b_up, use_tanh):
    m = mids.shape[0]
    rank, n = b_up.shape
    b_p = jnp.pad(b_up, ((0, LANES - rank), (0, 0))).astype(BF16)
    tm = _tile(m, 1024)
    return pl.pallas_call(
        functools.partial(_lora_up_kernel, use_tanh=use_tanh),
        grid=(m // tm,),
        in_specs=[pl.BlockSpec((tm, LANES), lambda i: (i, sel)),
                  pl.BlockSpec((LANES, n), lambda i: (0, 0))],
        out_specs=pl.BlockSpec((tm, n), lambda i: (i, 0)),
        out_shape=jax.ShapeDtypeStruct((m, n), F32),
        compiler_params=_params("parallel"),
        name="lora_up",
    )(mids, b_p)


def _out_mm_kernel(*refs, n_x, final_norm):
    x_refs, w_ref, r_ref = refs[:n_x], refs[n_x], refs[n_x + 1]
    acc = r_ref[...]
    k0 = 0
    for x_ref in x_refs:
        kx = x_ref.shape[1]
        acc = acc + _dot(x_ref[...], w_ref[k0:k0 + kx, :])
        k0 += kx
    if final_norm:
        g_ref, o_ref = refs[n_x + 2], refs[n_x + 3]
        o_ref[...] = _rms(acc, g_ref[...])
    else:
        refs[n_x + 2][...] = acc


def _out_matmul(xs, w, residual, final_gain=None):
    m = xs[0].shape[0]
    k, n = w.shape
    tm = _tile(m, 512)
    tn = _tile(n, 2048)
    in_specs = [pl.BlockSpec((tm, x.shape[1]), lambda i, j: (i, 0)) for x in xs]
    in_specs += [pl.BlockSpec((k, tn), lambda i, j: (0, j)), pl.BlockSpec((tm, tn), lambda i, j: (i, j))]
    args = [*xs, w, residual]
    if final_gain is not None:
        assert tn == n
        in_specs.append(pl.BlockSpec((1, n), lambda i, j: (0, 0)))
        args.append(final_gain.reshape(1, n))
    return pl.pallas_call(
        functools.partial(_out_mm_kernel, n_x=len(xs), final_norm=final_gain is not None),
        grid=(m // tm, n // tn),
        in_specs=in_specs,
        out_specs=pl.BlockSpec((tm, tn), lambda i, j: (i, j)),
        out_shape=jax.ShapeDtypeStruct((m, n), F32),
        compiler_params=_params("parallel", "parallel"),
        name="out_matmul",
    )(*args)


_DN_F128 = dict(q=0, k=1, vb=2, bc=3, g=4, u0=5, o=6)
_DN_F64 = dict(p=0, dmat=1)


def _dn_kernel(q_ref, k_ref, v_ref, qh_ref, kh_ref, vh_ref, z_ref, ba_ref, sel_ref, hp_ref, cw_ref, on_ref,
               o_ref, s_ref, sb_s, f128, wq_s, ub_s, nc_s, f64, rhs_s, akd_s, cd_s):
    t = pl.program_id(2)
    tc = q_ref.shape[1]
    heads = s_ref.shape[0]
    c = DN_CHUNK
    qf, qp = _DN_F128, _DN_F64

    @pl.when(t == 0)
    def _():
        s_ref[...] = jnp.zeros_like(s_ref)
        sb_s[...] = jnp.zeros_like(sb_s)

    row = _iota2((c, c), 0)
    col = _iota2((c, c), 1)
    causal = row >= col
    strict = row > col
    strict_f = strict.astype(F32)
    eye = (row == col).astype(F32)
    tril = causal.astype(BF16)
    onorm = on_ref[...]

    def where(ci, hh):
        return ci * heads + hh, slice(ci * c, (ci + 1) * c), slice(hh * LANES, (hh + 1) * LANES)

    def conv_silu(x_ref, h_ref, ci, rs, ls, w):
        x = x_ref[0, rs, ls]
        if ci == 0:
            halo = jnp.where(t > 0, h_ref[0, :, ls], 0.0)
        else:
            halo = x_ref[0, ci * c - SUBLANES:ci * c, ls]
        y = x * w[CONV_TAPS - 1:CONV_TAPS, :]
        for j in range(1, CONV_TAPS):
            y = y + _shift_rows(x, halo, j) * w[CONV_TAPS - 1 - j:CONV_TAPS - j, :]
        return _silu(y)

    def l2n(x):
        return x * lax.rsqrt(jnp.sum(x * x, axis=-1, keepdims=True) + L2_EPS)

    def stage_gates(ci, hh):
        i, rs, ls = where(ci, hh)
        cw = cw_ref[hh]
        sel = sel_ref[hh]
        hp = hp_ref[hh]
        ba = ba_ref[0, rs, :]
        qc = l2n(conv_silu(q_ref, qh_ref, ci, rs, ls, cw[0:CONV_TAPS])) * (LANES ** -0.5)
        kc = l2n(conv_silu(k_ref, kh_ref, ci, rs, ls, cw[CONV_TAPS:2 * CONV_TAPS]))
        vc = conv_silu(v_ref, vh_ref, ci, rs, ls, cw[2 * CONV_TAPS:3 * CONV_TAPS])
        beta = jax.nn.sigmoid(jnp.sum(ba * sel[0:1], axis=-1, keepdims=True))
        alpha_pre = jnp.sum(ba * sel[1:2], axis=-1, keepdims=True)
        bc = jnp.broadcast_to(beta, (c, LANES))
        lac = -jnp.exp(hp[0:1]) * _softplus(jnp.broadcast_to(alpha_pre, (c, LANES)) + hp[1:2])
        f128[i, qf["q"]] = qc
        f128[i, qf["k"]] = kc
        f128[i, qf["vb"]] = vc * bc
        f128[i, qf["bc"]] = bc
        f128[i, qf["g"]] = _dot01(tril, lac)
        f64[i, qp["dmat"]] = _dot01(tril, lac[:, 0:c] * strict_f)

    def stage_intra(ci, hh):
        i, rs, ls = where(ci, hh)
        qc = f128[i, qf["q"]]
        kc = f128[i, qf["k"]]
        bc = f128[i, qf["bc"]]
        g = f128[i, qf["g"]]
        decay = jnp.where(causal, jnp.exp(jnp.where(causal, f64[i, qp["dmat"]], 0.0)), 0.0)
        kb = kc.astype(BF16)
        qkk = _dot_nt(jnp.concatenate([qc.astype(BF16), kb], axis=0), kb)
        lower = jnp.where(strict, bc[:, 0:c] * qkk[c:] * decay, 0.0)
        gamma = jnp.exp(g)
        glast = g[c - 1:c, :]
        akd_s[i, 0:c] = jnp.where(causal, qkk[0:c] * decay, 0.0).astype(BF16)
        nc_s[i] = (-lower).astype(BF16)
        f64[i, qp["p"]] = eye - lower
        rhs_s[i] = jnp.concatenate([f128[i, qf["vb"]], kc * (bc * gamma)], axis=1).astype(BF16)
        wq_s[i, c:] = (qc * gamma).astype(BF16)
        akd_s[i, c:] = (kc * jnp.exp(glast - g)).T.astype(BF16)
        cd_s[i] = jnp.broadcast_to(jnp.exp(glast), (SUBLANES, LANES))

    def stage_square(ci, hh):
        i, _, _ = where(ci, hh)
        nc = nc_s[i]
        nc_s[i] = _dot(nc, nc).astype(BF16)

    def stage_accumulate(ci, hh):
        i, _, _ = where(ci, hh)
        p = f64[i, qp["p"]]
        f64[i, qp["p"]] = p + _dot(p.astype(BF16), nc_s[i])

    def stage_solve(ci, hh):
        i, _, _ = where(ci, hh)
        sol = _dot(f64[i, qp["p"]].astype(BF16), rhs_s[i])
        f128[i, qf["u0"]] = sol[:, 0:LANES]
        wq_s[i, 0:c] = sol[:, LANES:].astype(BF16)

    def stage_state_a(ci, hh):
        i, _, _ = where(ci, hh)
        wq = _dot(wq_s[i], sb_s[hh])
        ub_s[i] = (f128[i, qf["u0"]] - wq[0:c]).astype(BF16)
        f128[i, qf["o"]] = wq[c:]

    def stage_state_b(ci, hh):
        i, _, _ = where(ci, hh)
        ou = _dot(akd_s[i], ub_s[i])
        f128[i, qf["o"]] = f128[i, qf["o"]] + ou[0:c]
        s_new = cd_s[i][0:1] * s_ref[hh] + ou[c:]
        s_ref[hh] = s_new
        sb_s[hh] = s_new.astype(BF16)

    def stage_out(ci, hh):
        i, rs, ls = where(ci, hh)
        o = f128[i, qf["o"]]
        on = o * lax.rsqrt(jnp.mean(o * o, axis=-1, keepdims=True) + NORM_EPS) * onorm
        o_ref[0, rs, ls] = (on * _silu(z_ref[0, rs, ls])).astype(o_ref.dtype)

    stages = ([stage_gates, stage_intra] + [stage_square, stage_accumulate] * 5
              + [stage_solve, stage_state_a, stage_state_b, stage_out])
    _emit_skewed(stages, tc // c, heads)


def _deltanet(p, ba, conv_w, a_log, dt_bias, out_norm, nheads):
    b, t, _ = p.shape
    tc = _tile(t, 256)
    c = DN_CHUNK
    assert tc % c == 0
    hb = tc // SUBLANES
    nh = nheads
    hg = _tile(nh, 8)
    ng = nh // hg
    wd = hg * LANES
    nbody = (tc // c) * hg
    cw = conv_w.reshape(CONV_TAPS, 3, nh, LANES).transpose(2, 1, 0, 3).reshape(nh, 3 * CONV_TAPS, LANES)
    lane = jnp.arange(LANES)[None, None, :]
    head = jnp.arange(nh)[:, None, None]
    sel = jnp.concatenate([(lane == head), (lane == head + nh)], axis=1).astype(F32)
    hp = jnp.broadcast_to(jnp.stack([a_log, dt_bias], axis=1)[:, :, None], (nh, 2, LANES)).astype(F32)

    def main(sec):
        return pl.BlockSpec((1, tc, wd), lambda i, h, j: (i, j, sec * ng + h))

    def halo(sec):
        return pl.BlockSpec((1, SUBLANES, wd), lambda i, h, j: (i, jnp.maximum(j * hb - 1, 0), sec * ng + h))

    per_head = lambda rows: pl.BlockSpec((hg, rows, LANES), lambda i, h, j: (h, 0, 0))
    return pl.pallas_call(
        _dn_kernel,
        grid=(b, ng, t // tc),
        in_specs=[main(0), main(1), main(2), halo(0), halo(1), halo(2), main(3),
                  pl.BlockSpec((1, tc, LANES), lambda i, h, j: (i, j, 0)),
                  per_head(2), per_head(2), per_head(3 * CONV_TAPS),
                  pl.BlockSpec((1, LANES), lambda i, h, j: (0, 0))],
        out_specs=pl.BlockSpec((1, tc, wd), lambda i, h, j: (i, j, h)),
        out_shape=jax.ShapeDtypeStruct((b, t, nh * LANES), BF16),
        scratch_shapes=[pltpu.VMEM((hg, LANES, LANES), F32),
                        pltpu.VMEM((hg, LANES, LANES), BF16),
                        pltpu.VMEM((nbody, len(_DN_F128), c, LANES), F32),
                        pltpu.VMEM((nbody, 2 * c, LANES), BF16),
                        pltpu.VMEM((nbody, c, LANES), BF16),
                        pltpu.VMEM((nbody, c, c), BF16),
                        pltpu.VMEM((nbody, len(_DN_F64), c, c), F32),
                        pltpu.VMEM((nbody, c, 2 * LANES), BF16),
                        pltpu.VMEM((nbody, c + LANES, c), BF16),
                        pltpu.VMEM((nbody, SUBLANES, LANES), F32)],
        compiler_params=_params("parallel", "parallel", "arbitrary"),
        name="deltanet",
    )(p, p, p, p, p, p, p, ba, sel, hp, cw, out_norm.reshape(1, LANES))


HG_LEVELS = 6
HG_BLOCK = 64


def _hg_tables():
    import numpy as np
    c = HG_BLOCK
    t = np.arange(c)[:, None]
    i = np.arange(c)[None, :]
    rows = [i <= t, i > t]
    q_rows, k_rows = [], []
    level = np.full((c, c), -1, np.int32)
    level[np.arange(c), np.arange(c)] = HG_LEVELS
    for li in range(HG_LEVELS):
        m = c >> (li + 1)
        pos = t % (2 * m)
        upper = pos >= m
        ref = t - pos + m - 1
        q_rows.append(upper & (i > ref) & (i <= t))
        k_rows.append((~upper) & (i > t) & (i <= ref))
        same = (t // (2 * m)) == (i // (2 * m))
        level[same & upper & ((i % (2 * m)) < m)] = li
    table = np.concatenate(rows + q_rows + k_rows, axis=0).astype(np.float32)
    return table, level


def _hg_kernel(q_ref, f_ref, i_ref, z_ref, lb_ref, on_ref, tab_ref, lvl_ref, o_ref,
               s_ref, sb_s, qt_s, kt_s, qb_s, vb_s, klt_s, att_s, ecol_s, inc_s, o_s, *, layer):
    t = pl.program_id(2)
    tc = q_ref.shape[1]
    heads = s_ref.shape[0]
    c = HG_BLOCK
    nl = HG_LEVELS

    @pl.when(t == 0)
    def _():
        s_ref[...] = jnp.zeros_like(s_ref)
        sb_s[...] = jnp.zeros_like(sb_s)

    table = tab_ref[...]
    level = lvl_ref[...]
    onorm = on_ref[...]
    bodies = [(ci, hh) for ci in range(tc // c) for hh in range(heads)]

    def where(ci, hh):
        return ci * heads + hh, slice(ci * c, (ci + 1) * c), slice(hh * LANES, (hh + 1) * LANES)

    for ci, hh in bodies:
        i, rs, ls = where(ci, hh)
        logits = lb_ref[:, ls]
        e = jnp.exp(logits - jnp.max(logits, axis=0, keepdims=True))
        share = e / jnp.sum(e, axis=0, keepdims=True)
        lbound = jnp.sum(share[0:layer + 1], axis=0, keepdims=True) - share[0:1]
        q = _silu(q_ref[0, rs, ls]) * (LANES ** -0.5)
        forget = lbound + (1.0 - lbound) * jax.nn.sigmoid(f_ref[0, rs, ls])
        k = 1.0 - forget
        d = _dot(table, jnp.log(forget).astype(BF16))
        b = d[0:c]
        qb_s[i] = (q * jnp.exp(b)).astype(BF16)
        klt_s[i] = (k * jnp.exp(d[c:2 * c])).T.astype(BF16)
        ecol_s[i] = jnp.broadcast_to(jnp.exp(b[c - 1:c]), (LANES, LANES)).T
        for li in range(nl):
            qt_s[i, li] = (q * jnp.exp(d[(2 + li) * c:(3 + li) * c])).astype(BF16)
            kt_s[i, li] = (k * jnp.exp(d[(2 + nl + li) * c:(3 + nl + li) * c])).astype(BF16)
        qt_s[i, nl] = q.astype(BF16)
        kt_s[i, nl] = k.astype(BF16)
        vb_s[i] = i_ref[0, rs, ls].astype(BF16)

    for ci, hh in bodies:
        i, _, _ = where(ci, hh)
        att = jnp.zeros((c, c), F32)
        for li in range(nl + 1):
            att = att + jnp.where(level == li, _dot_nt(qt_s[i, li], kt_s[i, li]), 0.0)
        att_s[i] = att.astype(BF16)
        inc_s[i] = _dot(klt_s[i], vb_s[i])

    for ci, hh in bodies:
        i, _, _ = where(ci, hh)
        o_s[i] = _dot(att_s[i], vb_s[i])

    for ci, hh in bodies:
        i, _, _ = where(ci, hh)
        o_s[i] = o_s[i] + _dot(qb_s[i], sb_s[hh])
        s_new = ecol_s[i] * s_ref[hh] + inc_s[i]
        s_ref[hh] = s_new
        sb_s[hh] = s_new.astype(BF16)

    for ci, hh in bodies:
        i, rs, ls = where(ci, hh)
        o = o_s[i]
        on = o * lax.rsqrt(jnp.mean(o * o, axis=-1, keepdims=True) + NORM_EPS) * onorm
        o_ref[0, rs, ls] = (on * _silu(z_ref[0, rs, ls])).astype(o_ref.dtype)


def _hgrn2(p, col0, lb_logits, out_norm, nheads, layer):
    b, t, _ = p.shape
    tc = _tile(t, 256)
    c = HG_BLOCK
    assert tc % c == 0
    nh = nheads
    hg = _tile(nh, 8)
    ng = nh // hg
    wd = hg * LANES
    assert col0 % hg == 0
    cg = col0 // hg
    nbody = (tc // c) * hg
    table, level = _hg_tables()
    nrow = table.shape[0]

    def main(sec):
        return pl.BlockSpec((1, tc, wd), lambda i, h, j: (i, j, cg + sec * ng + h))

    nl = lb_logits.shape[0]
    const = lambda shape: pl.BlockSpec(shape, lambda i, h, j: (0, 0))
    return pl.pallas_call(
        functools.partial(_hg_kernel, layer=layer),
        grid=(b, ng, t // tc),
        in_specs=[main(0), main(1), main(2), main(3),
                  pl.BlockSpec((nl, wd), lambda i, h, j: (0, h)),
                  const((1, LANES)), const((nrow, c)), const((c, c))],
        out_specs=pl.BlockSpec((1, tc, wd), lambda i, h, j: (i, j, h)),
        out_shape=jax.ShapeDtypeStruct((b, t, nh * LANES), BF16),
        scratch_shapes=[pltpu.VMEM((hg, LANES, LANES), F32),
                        pltpu.VMEM((hg, LANES, LANES), BF16),
                        pltpu.VMEM((nbody, HG_LEVELS + 1, c, LANES), BF16),
                        pltpu.VMEM((nbody, HG_LEVELS + 1, c, LANES), BF16),
                        pltpu.VMEM((nbody, c, LANES), BF16),
                        pltpu.VMEM((nbody, c, LANES), BF16),
                        pltpu.VMEM((nbody, LANES, c), BF16),
                        pltpu.VMEM((nbody, c, c), BF16),
                        pltpu.VMEM((nbody, LANES, LANES), F32),
                        pltpu.VMEM((nbody, LANES, LANES), F32),
                        pltpu.VMEM((nbody, c, LANES), F32)],
        compiler_params=_params("parallel", "parallel", "arbitrary"),
        name="hgrn2",
    )(p, p, p, p, lb_logits, out_norm.reshape(1, LANES), jnp.asarray(table, BF16), jnp.asarray(level))


_RW_B64 = dict(aak=0, vb=1, arb=2, ark=3, sab=4)
_RW_B128 = dict(nb=0, vbd=1, bkt=2, atbd=3, avbd=4)
_RW_F64 = dict(p=0, uc=1, g=2, bv=3, lw=4, kkr=5, k2=6, ag=7, v=8, ssq=9, y=10, d=11)


def _rw_kernel(*refs, has_vres):
    if has_vres:
        (r_ref, k_ref, v_ref, z_ref, wl_ref, al_ref, vl_ref, vf_ref, prm_ref,
         o_ref, h_ref, hb_s, s64, s128, wr_s, np_s, f64, f128) = refs
    else:
        (r_ref, k_ref, v_ref, z_ref, wl_ref, al_ref, prm_ref,
         o_ref, h_ref, hb_s, s64, s128, wr_s, np_s, f64, f128) = refs
    t = pl.program_id(2)
    tc = o_ref.shape[1]
    pairs = h_ref.shape[0]
    c = RW_CHUNK
    nchunk = tc // c
    b64, b128, q64 = _RW_B64, _RW_B128, _RW_F64

    @pl.when(t == 0)
    def _():
        h_ref[...] = jnp.zeros_like(h_ref)
        hb_s[...] = jnp.zeros_like(hb_s)

    lane1 = _iota2((1, LANES), 1)
    head0 = lane1 < RW_HEAD
    m0 = head0.astype(F32)
    m1 = 1.0 - m0
    same_head = (_iota2((LANES, LANES), 0) < RW_HEAD) == (_iota2((LANES, LANES), 1) < RW_HEAD)
    bdmask = same_head.astype(F32)
    row = _iota2((c, LANES), 0)
    scol = jnp.bitwise_and(_iota2((c, LANES), 1), RW_HEAD - 1)
    incl = row >= scol
    strict = row > scol
    eye_cat = (row == scol).astype(F32)
    tril = (_iota2((c, c), 0) >= _iota2((c, c), 1)).astype(BF16)

    def bd(x):
        return jnp.concatenate([x * m0, x * m1], axis=0).astype(BF16)

    def gsum(x):
        s0 = jnp.sum(jnp.where(head0, x, 0.0), axis=-1, keepdims=True)
        s1 = jnp.sum(x, axis=-1, keepdims=True) - s0
        return jnp.where(head0, s0, s1)

    def where(ci, pp):
        return ci * pairs + pp, slice(ci * c, (ci + 1) * c), slice(pp * LANES, (pp + 1) * LANES)

    def stage_gates(ci, pp):
        i, rs, ls = where(ci, pp)
        prm = prm_ref[:, ls]
        w0, a0, k_k, k_a, r_k, v0 = prm[0:1], prm[1:2], prm[2:3], prm[3:4], prm[4:5], prm[7:8]
        r = r_ref[0, rs, ls]
        k = k_ref[0, rs, ls]
        v = v_ref[0, rs, ls]
        lw = -EXP_NEG_HALF * jax.nn.sigmoid(w0 + wl_ref[0, rs, ls])
        ag = jax.nn.sigmoid(a0 + al_ref[0, rs, ls])
        if has_vres:
            v = v + (vf_ref[0, rs, ls] - v) * jax.nn.sigmoid(v0 + vl_ref[0, rs, ls])
        kkr = k * k_k
        k2 = k * (1.0 + (ag - 1.0) * k_a)
        sums = gsum(jnp.concatenate([kkr * kkr, r * k2 * r_k], axis=0))
        f64[i, q64["g"]] = _dot01(tril, lw)
        f64[i, q64["lw"]] = lw
        f64[i, q64["ag"]] = ag
        f64[i, q64["v"]] = v
        f64[i, q64["kkr"]] = kkr
        f64[i, q64["k2"]] = k2
        f64[i, q64["ssq"]] = sums[0:c]
        f64[i, q64["bv"]] = sums[c:] * v

    def stage_intra(ci, pp):
        i, rs, ls = where(ci, pp)
        g = f64[i, q64["g"]]
        lw = f64[i, q64["lw"]]
        v = f64[i, q64["v"]]
        kk = f64[i, q64["kkr"]] * lax.rsqrt(f64[i, q64["ssq"]] + L2_EPS)
        e_g = jnp.exp(g)
        e_gn = jnp.exp(-g)
        rt = r_ref[0, rs, ls] * e_g
        at = -kk * jnp.exp(g - lw)
        kt = f64[i, q64["k2"]] * e_gn
        bt = kk * f64[i, q64["ag"]] * e_gn
        x = _bdot_nt(jnp.concatenate([at, rt], axis=0), jnp.concatenate([bd(kt), bd(bt)], axis=0))
        a_ab = jnp.where(strict, x[0:c, LANES:], 0.0)
        e_last = e_g[c - 1:c, :]
        s64[i, b64["aak"]] = jnp.where(strict, x[0:c, 0:LANES], 0.0).astype(BF16)
        s64[i, b64["ark"]] = jnp.where(incl, x[c:, 0:LANES], 0.0).astype(BF16)
        s64[i, b64["arb"]] = jnp.where(incl, x[c:, LANES:], 0.0).astype(BF16)
        np_s[i, 0:c] = a_ab.astype(BF16)
        np_s[i, c:] = (eye_cat + a_ab).astype(BF16)
        s128[i, b128["nb"]] = bd(a_ab)
        f64[i, q64["p"]] = eye_cat + a_ab
        wr_s[i, c:] = rt.astype(BF16)
        s64[i, b64["vb"]] = v.astype(BF16)
        s128[i, b128["atbd"]] = bd(at)
        s128[i, b128["vbd"]] = bd(v)
        s128[i, b128["bkt"]] = jnp.concatenate([bt * e_last, kt * e_last], axis=0).T.astype(BF16)
        f128[i] = jnp.broadcast_to(e_last, (LANES, LANES)).T

    def stage_av(ci, pp):
        i, _, _ = where(ci, pp)
        s128[i, b128["avbd"]] = bd(_dot(s64[i, b64["aak"]], s128[i, b128["vbd"]]))

    def stage_level(lvl):
        def stage(ci, pp):
            i, _, _ = where(ci, pp)
            lhs = np_s[i, 0:c] if lvl == 0 else (np_s[i, c:] if lvl == 5 else np_s[i])
            out = _dot(lhs, s128[i, b128["nb"]])
            if lvl > 0:
                p = f64[i, q64["p"]] + out[-c:]
                f64[i, q64["p"]] = p
                np_s[i, c:] = p.astype(BF16)
            if lvl < 5:
                n2 = out[0:c]
                np_s[i, 0:c] = n2.astype(BF16)
                s128[i, b128["nb"]] = bd(n2)
        return stage

    def stage_wu(ci, pp):
        i, _, _ = where(ci, pp)
        rhs = jnp.concatenate([s128[i, b128["atbd"]], s128[i, b128["avbd"]]], axis=1)
        wu = _dot(f64[i, q64["p"]].astype(BF16), rhs)
        wr_s[i, 0:c] = wu[:, 0:LANES].astype(BF16)
        f64[i, q64["uc"]] = wu[:, LANES:]

    def stage_state_a(ci, pp):
        i, _, _ = where(ci, pp)
        wr = _dot(wr_s[i], hb_s[pp])
        sa = wr[0:c] + f64[i, q64["uc"]]
        s64[i, b64["sab"]] = sa.astype(BF16)
        s128[i, b128["nb"]] = bd(sa)
        f64[i, q64["y"]] = wr[c:]

    def stage_state_b(ci, pp):
        i, _, _ = where(ci, pp)
        f64[i, q64["y"]] = f64[i, q64["y"]] + _dot(
            jnp.concatenate([s64[i, b64["arb"]], s64[i, b64["ark"]]], axis=1),
            jnp.concatenate([s128[i, b128["nb"]], s128[i, b128["vbd"]]], axis=0))
        h_new = f128[i] * h_ref[pp] + bdmask * _dot(
            s128[i, b128["bkt"]], jnp.concatenate([s64[i, b64["sab"]], s64[i, b64["vb"]]], axis=0))
        h_ref[pp] = h_new
        hb_s[pp] = h_new.astype(BF16)

    def stage_center(ci, pp):
        i, _, _ = where(ci, pp)
        y = f64[i, q64["y"]]
        f64[i, q64["d"]] = y - gsum(y) * (1.0 / RW_HEAD)

    def stage_out(ci, pp):
        i, rs, ls = where(ci, pp)
        prm = prm_ref[:, ls]
        d = f64[i, q64["d"]]
        var = gsum(d * d) * (1.0 / RW_HEAD)
        yn = d * lax.rsqrt(var + GN_EPS) * prm[5:6] + prm[6:7]
        o_ref[0, rs, ls] = ((yn + f64[i, q64["bv"]]) * _silu(z_ref[0, rs, ls])).astype(o_ref.dtype)

    stages = ([stage_gates, stage_intra, stage_av] + [stage_level(lvl) for lvl in range(6)]
              + [stage_wu, stage_state_a, stage_state_b, stage_center, stage_out])
    _emit_skewed(stages, nchunk, pairs)


def _rwkv7(rkvz, wl, al, vl, v_first, prm):
    b, t, d4 = rkvz.shape
    d = d4 // 4
    tc = _tile(t, 256)
    has_vres = vl is not None
    pg = _tile(d // LANES, 8)
    wd = pg * LANES
    c = RW_CHUNK
    nbody = (tc // c) * pg
    ngrp = d // wd

    def sec(s):
        return pl.BlockSpec((1, tc, wd), lambda i, p, j: (i, j, s * ngrp + p))

    flat = pl.BlockSpec((1, tc, wd), lambda i, p, j: (i, j, p))
    in_specs = [sec(0), sec(1), sec(2), sec(3), flat, flat]
    args = [rkvz, rkvz, rkvz, rkvz, wl, al]
    if has_vres:
        in_specs += [flat, sec(2)]
        args += [vl, v_first]
    in_specs.append(pl.BlockSpec((8, wd), lambda i, p, j: (0, p)))
    args.append(prm)
    return pl.pallas_call(
        functools.partial(_rw_kernel, has_vres=has_vres),
        grid=(b, d // wd, t // tc),
        in_specs=in_specs,
        out_specs=flat,
        out_shape=jax.ShapeDtypeStruct((b, t, d), BF16),
        scratch_shapes=[pltpu.VMEM((pg, LANES, LANES), F32),
                        pltpu.VMEM((pg, LANES, LANES), BF16),
                        pltpu.VMEM((nbody, len(_RW_B64), c, LANES), BF16),
                        pltpu.VMEM((nbody, len(_RW_B128), 2 * c, LANES), BF16),
                        pltpu.VMEM((nbody, 2 * c, LANES), BF16),
                        pltpu.VMEM((nbody, 2 * c, LANES), BF16),
                        pltpu.VMEM((nbody, len(_RW_F64), c, LANES), F32),
                        pltpu.VMEM((nbody, LANES, LANES), F32)],
        compiler_params=_params("parallel", "parallel", "arbitrary"),
        name="rwkv7",
    )(*args)


def kernel(x, norm_gains, mix_w_in, dn_conv, dn_a_log, dn_dt_bias, dn_out_norm, hg_lb_logits, hg_out_norm,
           mix_w_out, rw_mu, rw_w_rkvz, rw_w0, rw_w1, rw_w2, rw_a0, rw_a1, rw_a2, rw_v0, rw_v1, rw_v2,
           rw_k_k, rw_k_a, rw_r_k, rw_ln_w, rw_ln_b, rw_w_out, final_norm):
    b, t, d = x.shape
    m = b * t
    depth = norm_gains.shape[0]
    dn_heads = dn_a_log.shape[1]
    dn_width = dn_conv.shape[2] // 3
    hg_width = hg_lb_logits.shape[1]
    hg_heads = hg_width // LANES
    assert dn_width == dn_heads * LANES and hg_out_norm.shape[1] == LANES and rw_r_k.shape[2] == RW_HEAD
    dn_ba = 4 * dn_width
    hg_q = dn_ba + 2 * dn_heads
    assert 2 * dn_heads <= LANES

    h = x.reshape(m, d)
    v_first = None
    for layer in range(depth):
        gain = norm_gains[layer]
        if layer % 2 == 0:
            e = layer // 2
            w_in = mix_w_in[e]
            w_main = jnp.concatenate([w_in[:, :dn_ba], w_in[:, hg_q:]], axis=1).astype(BF16)
            w_ba = jnp.pad(w_in[:, dn_ba:hg_q], ((0, 0), (0, LANES - 2 * dn_heads))).astype(BF16)
            p, ba = _norm_matmul(h, gain, w_main, w_ba)
            p = p.reshape(b, t, -1)
            ba = ba.reshape(b, t, LANES)
            o_a = _deltanet(p, ba, dn_conv[e], dn_a_log[e], dn_dt_bias[e], dn_out_norm[e], dn_heads)
            o_b = _hgrn2(p, dn_ba // LANES, hg_lb_logits, hg_out_norm[e], hg_heads, e)
            h = _out_matmul([o_a.reshape(m, -1), o_b.reshape(m, -1)], mix_w_out[e].astype(BF16), h,
                            final_norm if layer == depth - 1 else None)
        else:
            o = layer // 2
            has_vres = v_first is not None
            rows = [0, 2, 3, 5, 1, 4] + ([3] if has_vres else [])
            downs = [rw_w1[o], rw_a1[o]] + ([rw_v1[o - 1]] if has_vres else [])
            a_down = jnp.stack([jnp.pad(a, ((0, 0), (0, LANES - a.shape[1]))) for a in downs]).astype(BF16)
            rkvz, mids = _mix_matmul(h.reshape(b, t, d), gain, rw_mu[o][jnp.array(rows)],
                                     rw_w_rkvz[o].astype(BF16), a_down)
            rkvz = rkvz.reshape(b, t, 4 * d)
            wl = _lora_up(mids, 0, rw_w2[o], True).reshape(b, t, d)
            al = _lora_up(mids, 1, rw_a2[o], False).reshape(b, t, d)
            vl = None
            v0 = jnp.zeros((d,), F32)
            if has_vres:
                vl = _lora_up(mids, 2, rw_v2[o - 1], False).reshape(b, t, d)
                v0 = rw_v0[o - 1]
            else:
                v_first = rkvz
            prm = jnp.stack([rw_w0[o], rw_a0[o], rw_k_k[o], rw_k_a[o], rw_r_k[o].reshape(d),
                             rw_ln_w[o], rw_ln_b[o], v0]).astype(F32)
            y = _rwkv7(rkvz, wl, al, vl, v_first, prm)
            h = _out_matmul([y.reshape(m, d)], rw_w_out[o].astype(BF16), h,
                            final_norm if layer == depth - 1 else None)
    return h.reshape(b, t, d)
```

```python
import functools
import math

import jax
import jax.numpy as jnp
from jax import lax
from jax.experimental import pallas as pl
from jax.experimental.pallas import tpu as pltpu

F32 = jnp.float32
BF16 = jnp.bfloat16

NORM_EPS = 1e-6
GN_EPS = 64e-5
L2_EPS = 1e-6
CONV_TAPS = 4

LANES = 128
SUBLANES = 8
V7X_VMEM_LIMIT_BYTES = 56 * 1024 * 1024

DN_CHUNK = 64
RW_CHUNK = 64
RW_HEAD = 64
EXP_NEG_HALF = math.exp(-0.5)
MIX_PROLOGUE_ROWS = 256
MIX_DOT_ROWS = 512
STAGE_SKEW = 2
HG_STAGE_SKEW = 2


def _params(*semantics):
    return pltpu.CompilerParams(dimension_semantics=semantics, vmem_limit_bytes=V7X_VMEM_LIMIT_BYTES)


def _tile(dim, pref):
    t = min(dim, pref)
    assert dim % t == 0, (dim, pref)
    return t


def _dot(a, b):
    return jnp.dot(a, b, preferred_element_type=F32)


def _dot_nt(a, b):
    return lax.dot_general(a, b, (((1,), (1,)), ((), ())), preferred_element_type=F32)


def _bdot_nt(a, b):
    return _dot_nt(a.astype(BF16), b.astype(BF16))


def _dot01(m01, x):
    x1 = x.astype(BF16)
    x2 = (x - x1.astype(F32)).astype(BF16)
    return _dot(m01, x1) + _dot(m01, x2)


def _silu(x):
    return x * jax.nn.sigmoid(x)


def _softplus(x):
    return jnp.maximum(x, 0.0) + jnp.log1p(jnp.exp(-jnp.abs(x)))


def _iota2(shape, dim):
    return lax.broadcasted_iota(jnp.int32, shape, dim)


def _shift_rows(x, halo, j):
    full = pltpu.roll(x, j, axis=0)
    hfix = pltpu.roll(halo, j, axis=0)
    top = jnp.where(_iota2(hfix.shape, 0) < j, hfix, full[0:SUBLANES])
    return jnp.concatenate([top, full[SUBLANES:]], axis=0)


def _emit_skewed(stages, nchunk, width, skew=STAGE_SKEW):
    n = len(stages)
    for tau in range(n + skew * (nchunk - 1)):
        for w in range(width):
            for ci in range(nchunk):
                st = tau - skew * ci
                if 0 <= st < n:
                    stages[st](ci, w)


def _rms(x, g):
    return x * lax.rsqrt(jnp.mean(x * x, axis=-1, keepdims=True) + NORM_EPS) * g


def _norm_mm_kernel(x_ref, g_ref, w_ref, wn_ref, o_ref, on_ref, hn_s):
    j = pl.program_id(1)

    @pl.when(j == 0)
    def _():
        hn_s[...] = _rms(x_ref[...], g_ref[...]).astype(BF16)
        on_ref[...] = _dot(hn_s[...], wn_ref[...])

    @pl.when(j > 0)
    def _():
        o_ref[...] = _dot(hn_s[...], w_ref[...])


def _norm_matmul(h, gain, w, w_narrow):
    m, d = h.shape
    n = w.shape[1]
    tm = _tile(m, 1024)
    tn = _tile(n, 1024)
    n_main = n // tn

    def col(j):
        return jnp.maximum(j - 1, 0)

    return pl.pallas_call(
        _norm_mm_kernel,
        grid=(m // tm, n_main + 1),
        in_specs=[pl.BlockSpec((tm, d), lambda i, j: (i, 0)),
                  pl.BlockSpec((1, d), lambda i, j: (0, 0)),
                  pl.BlockSpec((d, tn), lambda i, j: (0, col(j))),
                  pl.BlockSpec((d, LANES), lambda i, j: (0, 0))],
        out_specs=[pl.BlockSpec((tm, tn), lambda i, j: (i, col(j))),
                   pl.BlockSpec((tm, LANES), lambda i, j: (i, 0))],
        out_shape=[jax.ShapeDtypeStruct((m, n), F32), jax.ShapeDtypeStruct((m, LANES), F32)],
        scratch_shapes=[pltpu.VMEM((tm, d), BF16)],
        compiler_params=_params("parallel", "arbitrary"),
        name="norm_matmul",
    )(h, gain.reshape(1, d), w, w_narrow)


def _mix_mm_kernel(x_ref, halo_ref, g_ref, mu_ref, w_ref, a_ref, o_ref, mid_ref, hn_s, xx_s, *,
                   tiles_per_group, tiles_per_seq, n_groups):
    i = pl.program_id(0)
    j = pl.program_id(1)
    tm = hn_s.shape[0]
    rc = min(tm, MIX_PROLOGUE_ROWS)
    mc = min(tm, MIX_DOT_ROWS)

    def mix(rows, mu_row):
        return hn_s[rows, :] + xx_s[rows, :] * mu_row.astype(BF16)

    @pl.when(j == 0)
    def _():
        g = g_ref[...]
        first = lax.rem(i, tiles_per_seq) == 0
        for r in range(tm // rc):
            rows = slice(r * rc, (r + 1) * rc)
            hn = _rms(x_ref[0, rows, :], g)
            if r == 0:
                hp = jnp.where(first, 0.0, _rms(halo_ref[0], g))
            else:
                hp = _rms(x_ref[0, r * rc - SUBLANES:r * rc, :], g)
            hn_s[rows, :] = hn.astype(BF16)
            xx_s[rows, :] = (_shift_rows(hn, hp, 1) - hn).astype(BF16)

        for l in range(a_ref.shape[0]):
            mu_row = mu_ref[n_groups + l:n_groups + l + 1, :]
            for r in range(tm // mc):
                rows = slice(r * mc, (r + 1) * mc)
                mid_ref[rows, l * LANES:(l + 1) * LANES] = _dot(mix(rows, mu_row), a_ref[l])

    @pl.when(j > 0)
    def _():
        mu_row = mu_ref[pl.ds(lax.div(j - 1, tiles_per_group), 1), :]
        for r in range(tm // mc):
            rows = slice(r * mc, (r + 1) * mc)
            o_ref[rows, :] = _dot(mix(rows, mu_row), w_ref[0])


def _mix_matmul(h, gain, mu, w, a_down):
    b, t, d = h.shape
    m = b * t
    g, _, n = w.shape
    nl = a_down.shape[0]
    tm = _tile(t, 1024)
    tn = _tile(n, 1024)
    tpg = n // tn
    n_main = g * tpg
    tps = t // tm
    hb = tm // SUBLANES

    def col(j):
        return jnp.maximum(j - 1, 0)

    return pl.pallas_call(
        functools.partial(_mix_mm_kernel, tiles_per_group=tpg, tiles_per_seq=tps, n_groups=g),
        grid=(m // tm, n_main + 1),
        in_specs=[pl.BlockSpec((1, tm, d), lambda i, j: (i // tps, i % tps, 0)),
                  pl.BlockSpec((1, SUBLANES, d), lambda i, j: (i // tps, jnp.maximum((i % tps) * hb - 1, 0), 0)),
                  pl.BlockSpec((1, d), lambda i, j: (0, 0)),
                  pl.BlockSpec((g + nl, d), lambda i, j: (0, 0)),
                  pl.BlockSpec((1, d, tn), lambda i, j: (col(j) // tpg, 0, col(j) % tpg)),
                  pl.BlockSpec((nl, d, LANES), lambda i, j: (0, 0, 0))],
        out_specs=[pl.BlockSpec((tm, tn), lambda i, j: (i, col(j))),
                   pl.BlockSpec((tm, nl * LANES), lambda i, j: (i, 0))],
        out_shape=[jax.ShapeDtypeStruct((m, g * n), F32), jax.ShapeDtypeStruct((m, nl * LANES), F32)],
        scratch_shapes=[pltpu.VMEM((tm, d), BF16), pltpu.VMEM((tm, d), BF16)],
        compiler_params=_params("parallel", "arbitrary"),
        name="mix_matmul",
    )(h, h, gain.reshape(1, d), mu, w, a_down)


def _lora_up_kernel(mid_ref, b_ref, o_ref, *, use_tanh):
    mid = mid_ref[...]
    if use_tanh:
        mid = jnp.tanh(mid)
    o_ref[...] = _dot(mid.astype(BF16), b_ref[...])


def _lora_up(mids, sel, b_up, use_tanh):
    m = mids.shape[0]
    rank, n = b_up.shape
    b_p = jnp.pad(b_up, ((0, LANES - rank), (0, 0))).astype(BF16)
    tm = _tile(m, 1024)
    return pl.pallas_call(
        functools.partial(_lora_up_kernel, use_tanh=use_tanh),
        grid=(m // tm,),
        in_specs=[pl.BlockSpec((tm, LANES), lambda i: (i, sel)),
                  pl.BlockSpec((LANES, n), lambda i: (0, 0))],
        out_specs=pl.BlockSpec((tm, n), lambda i: (i, 0)),
        out_shape=jax.ShapeDtypeStruct((m, n), F32),
        compiler_params=_params("parallel"),
        name="lora_up",
    )(mids, b_p)


def _out_mm_kernel(*refs, n_x, final_norm):
    x_refs, w_ref, r_ref = refs[:n_x], refs[n_x], refs[n_x + 1]
    acc = r_ref[...]
    k0 = 0
    for x_ref in x_refs:
        kx = x_ref.shape[1]
        acc = acc + _dot(x_ref[...], w_ref[k0:k0 + kx, :])
        k0 += kx
    if final_norm:
        g_ref, o_ref = refs[n_x + 2], refs[n_x + 3]
        o_ref[...] = _rms(acc, g_ref[...])
    else:
        refs[n_x + 2][...] = acc


def _out_matmul(xs, w, residual, final_gain=None):
    m = xs[0].shape[0]
    k, n = w.shape
    tm = _tile(m, 512)
    tn = _tile(n, 2048)
    in_specs = [pl.BlockSpec((tm, x.shape[1]), lambda i, j: (i, 0)) for x in xs]
    in_specs += [pl.BlockSpec((k, tn), lambda i, j: (0, j)), pl.BlockSpec((tm, tn), lambda i, j: (i, j))]
    args = [*xs, w, residual]
    if final_gain is not None:
        assert tn == n
        in_specs.append(pl.BlockSpec((1, n), lambda i, j: (0, 0)))
        args.append(final_gain.reshape(1, n))
    return pl.pallas_call(
        functools.partial(_out_mm_kernel, n_x=len(xs), final_norm=final_gain is not None),
        grid=(m // tm, n // tn),
        in_specs=in_specs,
        out_specs=pl.BlockSpec((tm, tn), lambda i, j: (i, j)),
        out_shape=jax.ShapeDtypeStruct((m, n), F32),
        compiler_params=_params("parallel", "parallel"),
        name="out_matmul",
    )(*args)


_DN_F128 = dict(q=0, k=1, vb=2, bc=3, g=4, u0=5, o=6)
_DN_F64 = dict(p=0, dmat=1)


def _dn_kernel(q_ref, k_ref, v_ref, qh_ref, kh_ref, vh_ref, z_ref, ba_ref, sel_ref, hp_ref, cw_ref, on_ref,
               o_ref, s_ref, sb_s, f128, wq_s, ub_s, nc_s, f64, rhs_s, akd_s, cd_s):
    t = pl.program_id(2)
    tc = q_ref.shape[1]
    heads = s_ref.shape[0]
    c = DN_CHUNK
    qf, qp = _DN_F128, _DN_F64

    @pl.when(t == 0)
    def _():
        s_ref[...] = jnp.zeros_like(s_ref)
        sb_s[...] = jnp.zeros_like(sb_s)

    row = _iota2((c, c), 0)
    col = _iota2((c, c), 1)
    causal = row >= col
    strict = row > col
    strict_f = strict.astype(F32)
    eye = (row == col).astype(F32)
    tril = causal.astype(BF16)
    onorm = on_ref[...]

    def where(ci, hh):
        return ci * heads + hh, slice(ci * c, (ci + 1) * c), slice(hh * LANES, (hh + 1) * LANES)

    def conv_silu(x_ref, h_ref, ci, rs, ls, w):
        x = x_ref[0, rs, ls]
        if ci == 0:
            halo = jnp.where(t > 0, h_ref[0, :, ls], 0.0)
        else:
            halo = x_ref[0, ci * c - SUBLANES:ci * c, ls]
        y = x * w[CONV_TAPS - 1:CONV_TAPS, :]
        for j in range(1, CONV_TAPS):
            y = y + _shift_rows(x, halo, j) * w[CONV_TAPS - 1 - j:CONV_TAPS - j, :]
        return _silu(y)

    def l2n(x):
        return x * lax.rsqrt(jnp.sum(x * x, axis=-1, keepdims=True) + L2_EPS)

    def stage_gates(ci, hh):
        i, rs, ls = where(ci, hh)
        cw = cw_ref[hh]
        sel = sel_ref[hh]
        hp = hp_ref[hh]
        ba = ba_ref[0, rs, :]
        qc = l2n(conv_silu(q_ref, qh_ref, ci, rs, ls, cw[0:CONV_TAPS])) * (LANES ** -0.5)
        kc = l2n(conv_silu(k_ref, kh_ref, ci, rs, ls, cw[CONV_TAPS:2 * CONV_TAPS]))
        vc = conv_silu(v_ref, vh_ref, ci, rs, ls, cw[2 * CONV_TAPS:3 * CONV_TAPS])
        beta = jax.nn.sigmoid(jnp.sum(ba * sel[0:1], axis=-1, keepdims=True))
        alpha_pre = jnp.sum(ba * sel[1:2], axis=-1, keepdims=True)
        bc = jnp.broadcast_to(beta, (c, LANES))
        lac = -jnp.exp(hp[0:1]) * _softplus(jnp.broadcast_to(alpha_pre, (c, LANES)) + hp[1:2])
        f128[i, qf["q"]] = qc
        f128[i, qf["k"]] = kc
        f128[i, qf["vb"]] = vc * bc
        f128[i, qf["bc"]] = bc
        f128[i, qf["g"]] = _dot01(tril, lac)
        f64[i, qp["dmat"]] = _dot01(tril, lac[:, 0:c] * strict_f)

    def stage_intra(ci, hh):
        i, rs, ls = where(ci, hh)
        qc = f128[i, qf["q"]]
        kc = f128[i, qf["k"]]
        bc = f128[i, qf["bc"]]
        g = f128[i, qf["g"]]
        decay = jnp.where(causal, jnp.exp(jnp.where(causal, f64[i, qp["dmat"]], 0.0)), 0.0)
        kb = kc.astype(BF16)
        qkk = _dot_nt(jnp.concatenate([qc.astype(BF16), kb], axis=0), kb)
        lower = jnp.where(strict, bc[:, 0:c] * qkk[c:] * decay, 0.0)
        gamma = jnp.exp(g)
        glast = g[c - 1:c, :]
        akd_s[i, 0:c] = jnp.where(causal, qkk[0:c] * decay, 0.0).astype(BF16)
        nc_s[i] = (-lower).astype(BF16)
        f64[i, qp["p"]] = eye - lower
        rhs_s[i] = jnp.concatenate([f128[i, qf["vb"]], kc * (bc * gamma)], axis=1).astype(BF16)
        wq_s[i, c:] = (qc * gamma).astype(BF16)
        akd_s[i, c:] = (kc * jnp.exp(glast - g)).T.astype(BF16)
        cd_s[i] = jnp.broadcast_to(jnp.exp(glast), (SUBLANES, LANES))

    def stage_square(ci, hh):
        i, _, _ = where(ci, hh)
        nc = nc_s[i]
        nc_s[i] = _dot(nc, nc).astype(BF16)

    def stage_accumulate(ci, hh):
        i, _, _ = where(ci, hh)
        p = f64[i, qp["p"]]
        f64[i, qp["p"]] = p + _dot(p.astype(BF16), nc_s[i])

    def stage_solve(ci, hh):
        i, _, _ = where(ci, hh)
        sol = _dot(f64[i, qp["p"]].astype(BF16), rhs_s[i])
        f128[i, qf["u0"]] = sol[:, 0:LANES]
        wq_s[i, 0:c] = sol[:, LANES:].astype(BF16)

    def stage_state_a(ci, hh):
        i, _, _ = where(ci, hh)
        wq = _dot(wq_s[i], sb_s[hh])
        ub_s[i] = (f128[i, qf["u0"]] - wq[0:c]).astype(BF16)
        f128[i, qf["o"]] = wq[c:]

    def stage_state_b(ci, hh):
        i, _, _ = where(ci, hh)
        ou = _dot(akd_s[i], ub_s[i])
        f128[i, qf["o"]] = f128[i, qf["o"]] + ou[0:c]
        s_new = cd_s[i][0:1] * s_ref[hh] + ou[c:]
        s_ref[hh] = s_new
        sb_s[hh] = s_new.astype(BF16)

    def stage_out(ci, hh):
        i, rs, ls = where(ci, hh)
        o = f128[i, qf["o"]]
        on = o * lax.rsqrt(jnp.mean(o * o, axis=-1, keepdims=True) + NORM_EPS) * onorm
        o_ref[0, rs, ls] = (on * _silu(z_ref[0, rs, ls])).astype(o_ref.dtype)

    stages = ([stage_gates, stage_intra] + [stage_square, stage_accumulate] * 5
              + [stage_solve, stage_state_a, stage_state_b, stage_out])
    _emit_skewed(stages, tc // c, heads)


def _deltanet(p, ba, conv_w, a_log, dt_bias, out_norm, nheads):
    b, t, _ = p.shape
    tc = _tile(t, 256)
    c = DN_CHUNK
    assert tc % c == 0
    hb = tc // SUBLANES
    nh = nheads
    hg = _tile(nh, 8)
    ng = nh // hg
    wd = hg * LANES
    nbody = (tc // c) * hg
    cw = conv_w.reshape(CONV_TAPS, 3, nh, LANES).transpose(2, 1, 0, 3).reshape(nh, 3 * CONV_TAPS, LANES)
    lane = jnp.arange(LANES)[None, None, :]
    head = jnp.arange(nh)[:, None, None]
    sel = jnp.concatenate([(lane == head), (lane == head + nh)], axis=1).astype(F32)
    hp = jnp.broadcast_to(jnp.stack([a_log, dt_bias], axis=1)[:, :, None], (nh, 2, LANES)).astype(F32)

    def main(sec):
        return pl.BlockSpec((1, tc, wd), lambda i, h, j: (i, j, sec * ng + h))

    def halo(sec):
        return pl.BlockSpec((1, SUBLANES, wd), lambda i, h, j: (i, jnp.maximum(j * hb - 1, 0), sec * ng + h))

    per_head = lambda rows: pl.BlockSpec((hg, rows, LANES), lambda i, h, j: (h, 0, 0))
    return pl.pallas_call(
        _dn_kernel,
        grid=(b, ng, t // tc),
        in_specs=[main(0), main(1), main(2), halo(0), halo(1), halo(2), main(3),
                  pl.BlockSpec((1, tc, LANES), lambda i, h, j: (i, j, 0)),
                  per_head(2), per_head(2), per_head(3 * CONV_TAPS),
                  pl.BlockSpec((1, LANES), lambda i, h, j: (0, 0))],
        out_specs=pl.BlockSpec((1, tc, wd), lambda i, h, j: (i, j, h)),
        out_shape=jax.ShapeDtypeStruct((b, t, nh * LANES), BF16),
        scratch_shapes=[pltpu.VMEM((hg, LANES, LANES), F32),
                        pltpu.VMEM((hg, LANES, LANES), BF16),
                        pltpu.VMEM((nbody, len(_DN_F128), c, LANES), F32),
                        pltpu.VMEM((nbody, 2 * c, LANES), BF16),
                        pltpu.VMEM((nbody, c, LANES), BF16),
                        pltpu.VMEM((nbody, c, c), BF16),
                        pltpu.VMEM((nbody, len(_DN_F64), c, c), F32),
                        pltpu.VMEM((nbody, c, 2 * LANES), BF16),
                        pltpu.VMEM((nbody, c + LANES, c), BF16),
                        pltpu.VMEM((nbody, SUBLANES, LANES), F32)],
        compiler_params=_params("parallel", "parallel", "arbitrary"),
        name="deltanet",
    )(p, p, p, p, p, p, p, ba, sel, hp, cw, out_norm.reshape(1, LANES))


HG_LEVELS = 6
HG_BLOCK = 64


def _hg_tables():
    import numpy as np
    c = HG_BLOCK
    t = np.arange(c)[:, None]
    i = np.arange(c)[None, :]
    rows = [i <= t, i > t]
    q_rows, k_rows = [], []
    level = np.full((c, c), -1, np.int32)
    level[np.arange(c), np.arange(c)] = HG_LEVELS
    for li in range(HG_LEVELS):
        m = c >> (li + 1)
        pos = t % (2 * m)
        upper = pos >= m
        ref = t - pos + m - 1
        q_rows.append(upper & (i > ref) & (i <= t))
        k_rows.append((~upper) & (i > t) & (i <= ref))
        same = (t // (2 * m)) == (i // (2 * m))
        level[same & upper & ((i % (2 * m)) < m)] = li
    table = np.concatenate(rows + q_rows + k_rows, axis=0).astype(np.float32)
    return table, level


def _hg_kernel(q_ref, f_ref, i_ref, z_ref, lb_ref, on_ref, tab_ref, lvl_ref, o_ref,
               s_ref, sb_s, qt_s, kt_s, qb_s, vb_s, klt_s, att_s, ecol_s, inc_s, o_s, *, layer):
    t = pl.program_id(2)
    tc = q_ref.shape[1]
    heads = s_ref.shape[0]
    c = HG_BLOCK
    nl = HG_LEVELS

    @pl.when(t == 0)
    def _():
        s_ref[...] = jnp.zeros_like(s_ref)
        sb_s[...] = jnp.zeros_like(sb_s)

    table = tab_ref[...]
    level = lvl_ref[...]
    onorm = on_ref[...]
    def where(ci, hh):
        return ci * heads + hh, slice(ci * c, (ci + 1) * c), slice(hh * LANES, (hh + 1) * LANES)

    def stage_gates(ci, hh):
        i, rs, ls = where(ci, hh)
        logits = lb_ref[:, ls]
        e = jnp.exp(logits - jnp.max(logits, axis=0, keepdims=True))
        share = e / jnp.sum(e, axis=0, keepdims=True)
        lbound = jnp.sum(share[0:layer + 1], axis=0, keepdims=True) - share[0:1]
        q = _silu(q_ref[0, rs, ls]) * (LANES ** -0.5)
        forget = lbound + (1.0 - lbound) * jax.nn.sigmoid(f_ref[0, rs, ls])
        k = 1.0 - forget
        d = _dot(table, jnp.log(forget).astype(BF16))
        b = d[0:c]
        qb_s[i] = (q * jnp.exp(b)).astype(BF16)
        klt_s[i] = (k * jnp.exp(d[c:2 * c])).T.astype(BF16)
        ecol_s[i] = jnp.broadcast_to(jnp.exp(b[c - 1:c]), (LANES, LANES)).T
        for li in range(nl):
            qt_s[i, li] = (q * jnp.exp(d[(2 + li) * c:(3 + li) * c])).astype(BF16)
            kt_s[i, li] = (k * jnp.exp(d[(2 + nl + li) * c:(3 + nl + li) * c])).astype(BF16)
        qt_s[i, nl] = q.astype(BF16)
        kt_s[i, nl] = k.astype(BF16)
        vb_s[i] = i_ref[0, rs, ls].astype(BF16)

    def stage_intra(ci, hh):
        i, _, _ = where(ci, hh)
        att = jnp.zeros((c, c), F32)
        for li in range(nl + 1):
            att = att + jnp.where(level == li, _dot_nt(qt_s[i, li], kt_s[i, li]), 0.0)
        att_s[i] = att.astype(BF16)
        inc_s[i] = _dot(klt_s[i], vb_s[i])

    def stage_att_v(ci, hh):
        i, _, _ = where(ci, hh)
        o_s[i] = _dot(att_s[i], vb_s[i])

    def stage_state(ci, hh):
        i, _, _ = where(ci, hh)
        o_s[i] = o_s[i] + _dot(qb_s[i], sb_s[hh])
        s_new = ecol_s[i] * s_ref[hh] + inc_s[i]
        s_ref[hh] = s_new
        sb_s[hh] = s_new.astype(BF16)

    def stage_out(ci, hh):
        i, rs, ls = where(ci, hh)
        o = o_s[i]
        on = o * lax.rsqrt(jnp.mean(o * o, axis=-1, keepdims=True) + NORM_EPS) * onorm
        o_ref[0, rs, ls] = (on * _silu(z_ref[0, rs, ls])).astype(o_ref.dtype)

    _emit_skewed([stage_gates, stage_intra, stage_att_v, stage_state, stage_out], tc // c, heads, skew=HG_STAGE_SKEW)


def _hgrn2(p, col0, lb_logits, out_norm, nheads, layer):
    b, t, _ = p.shape
    tc = _tile(t, 256)
    c = HG_BLOCK
    assert tc % c == 0
    nh = nheads
    hg = _tile(nh, 8)
    ng = nh // hg
    wd = hg * LANES
    assert col0 % hg == 0
    cg = col0 // hg
    nbody = (tc // c) * hg
    table, level = _hg_tables()
    nrow = table.shape[0]

    def main(sec):
        return pl.BlockSpec((1, tc, wd), lambda i, h, j: (i, j, cg + sec * ng + h))

    nl = lb_logits.shape[0]
    const = lambda shape: pl.BlockSpec(shape, lambda i, h, j: (0, 0))
    return pl.pallas_call(
        functools.partial(_hg_kernel, layer=layer),
        grid=(b, ng, t // tc),
        in_specs=[main(0), main(1), main(2), main(3),
                  pl.BlockSpec((nl, wd), lambda i, h, j: (0, h)),
                  const((1, LANES)), const((nrow, c)), const((c, c))],
        out_specs=pl.BlockSpec((1, tc, wd), lambda i, h, j: (i, j, h)),
        out_shape=jax.ShapeDtypeStruct((b, t, nh * LANES), BF16),
        scratch_shapes=[pltpu.VMEM((hg, LANES, LANES), F32),
                        pltpu.VMEM((hg, LANES, LANES), BF16),
                        pltpu.VMEM((nbody, HG_LEVELS + 1, c, LANES), BF16),
                        pltpu.VMEM((nbody, HG_LEVELS + 1, c, LANES), BF16),
                        pltpu.VMEM((nbody, c, LANES), BF16),
                        pltpu.VMEM((nbody, c, LANES), BF16),
                        pltpu.VMEM((nbody, LANES, c), BF16),
                        pltpu.VMEM((nbody, c, c), BF16),
                        pltpu.VMEM((nbody, LANES, LANES), F32),
                        pltpu.VMEM((nbody, LANES, LANES), F32),
                        pltpu.VMEM((nbody, c, LANES), F32)],
        compiler_params=_params("parallel", "parallel", "arbitrary"),
        name="hgrn2",
    )(p, p, p, p, lb_logits, out_norm.reshape(1, LANES), jnp.asarray(table, BF16), jnp.asarray(level))


_RW_B64 = dict(aak=0, vb=1, arb=2, ark=3, sab=4)
_RW_B128 = dict(nb=0, vbd=1, bkt=2, atbd=3, avbd=4)
_RW_F64 = dict(p=0, uc=1, g=2, bv=3, lw=4, kkr=5, k2=6, ag=7, v=8, ssq=9, y=10, d=11)


def _rw_kernel(*refs, has_vres):
    if has_vres:
        (r_ref, k_ref, v_ref, z_ref, wl_ref, al_ref, vl_ref, vf_ref, prm_ref,
         o_ref, h_ref, hb_s, s64, s128, wr_s, np_s, f64, f128) = refs
    else:
        (r_ref, k_ref, v_ref, z_ref, wl_ref, al_ref, prm_ref,
         o_ref, h_ref, hb_s, s64, s128, wr_s, np_s, f64, f128) = refs
    t = pl.program_id(2)
    tc = o_ref.shape[1]
    pairs = h_ref.shape[0]
    c = RW_CHUNK
    nchunk = tc // c
    b64, b128, q64 = _RW_B64, _RW_B128, _RW_F64

    @pl.when(t == 0)
    def _():
        h_ref[...] = jnp.zeros_like(h_ref)
        hb_s[...] = jnp.zeros_like(hb_s)

    lane1 = _iota2((1, LANES), 1)
    head0 = lane1 < RW_HEAD
    m0 = head0.astype(F32)
    m1 = 1.0 - m0
    same_head = (_iota2((LANES, LANES), 0) < RW_HEAD) == (_iota2((LANES, LANES), 1) < RW_HEAD)
    bdmask = same_head.astype(F32)
    row = _iota2((c, LANES), 0)
    scol = jnp.bitwise_and(_iota2((c, LANES), 1), RW_HEAD - 1)
    incl = row >= scol
    strict = row > scol
    eye_cat = (row == scol).astype(F32)
    tril = (_iota2((c, c), 0) >= _iota2((c, c), 1)).astype(BF16)

    def bd(x):
        return jnp.concatenate([x * m0, x * m1], axis=0).astype(BF16)

    def gsum(x):
        s0 = jnp.sum(jnp.where(head0, x, 0.0), axis=-1, keepdims=True)
        s1 = jnp.sum(x, axis=-1, keepdims=True) - s0
        return jnp.where(head0, s0, s1)

    def where(ci, pp):
        return ci * pairs + pp, slice(ci * c, (ci + 1) * c), slice(pp * LANES, (pp + 1) * LANES)

    def stage_gates(ci, pp):
        i, rs, ls = where(ci, pp)
        prm = prm_ref[:, ls]
        w0, a0, k_k, k_a, r_k, v0 = prm[0:1], prm[1:2], prm[2:3], prm[3:4], prm[4:5], prm[7:8]
        r = r_ref[0, rs, ls]
        k = k_ref[0, rs, ls]
        v = v_ref[0, rs, ls]
        lw = -EXP_NEG_HALF * jax.nn.sigmoid(w0 + wl_ref[0, rs, ls])
        ag = jax.nn.sigmoid(a0 + al_ref[0, rs, ls])
        if has_vres:
            v = v + (vf_ref[0, rs, ls] - v) * jax.nn.sigmoid(v0 + vl_ref[0, rs, ls])
        kkr = k * k_k
        k2 = k * (1.0 + (ag - 1.0) * k_a)
        sums = gsum(jnp.concatenate([kkr * kkr, r * k2 * r_k], axis=0))
        f64[i, q64["g"]] = _dot01(tril, lw)
        f64[i, q64["lw"]] = lw
        f64[i, q64["ag"]] = ag
        f64[i, q64["v"]] = v
        f64[i, q64["kkr"]] = kkr
        f64[i, q64["k2"]] = k2
        f64[i, q64["ssq"]] = sums[0:c]
        f64[i, q64["bv"]] = sums[c:] * v

    def stage_intra(ci, pp):
        i, rs, ls = where(ci, pp)
        g = f64[i, q64["g"]]
        lw = f64[i, q64["lw"]]
        v = f64[i, q64["v"]]
        kk = f64[i, q64["kkr"]] * lax.rsqrt(f64[i, q64["ssq"]] + L2_EPS)
        e_g = jnp.exp(g)
        e_gn = jnp.exp(-g)
        rt = r_ref[0, rs, ls] * e_g
        at = -kk * jnp.exp(g - lw)
        kt = f64[i, q64["k2"]] * e_gn
        bt = kk * f64[i, q64["ag"]] * e_gn
        x = _bdot_nt(jnp.concatenate([at, rt], axis=0), jnp.concatenate([bd(kt), bd(bt)], axis=0))
        a_ab = jnp.where(strict, x[0:c, LANES:], 0.0)
        e_last = e_g[c - 1:c, :]
        s64[i, b64["aak"]] = jnp.where(strict, x[0:c, 0:LANES], 0.0).astype(BF16)
        s64[i, b64["ark"]] = jnp.where(incl, x[c:, 0:LANES], 0.0).astype(BF16)
        s64[i, b64["arb"]] = jnp.where(incl, x[c:, LANES:], 0.0).astype(BF16)
        np_s[i, 0:c] = a_ab.astype(BF16)
        np_s[i, c:] = (eye_cat + a_ab).astype(BF16)
        s128[i, b128["nb"]] = bd(a_ab)
        f64[i, q64["p"]] = eye_cat + a_ab
        wr_s[i, c:] = rt.astype(BF16)
        s64[i, b64["vb"]] = v.astype(BF16)
        s128[i, b128["atbd"]] = bd(at)
        s128[i, b128["vbd"]] = bd(v)
        s128[i, b128["bkt"]] = jnp.concatenate([bt * e_last, kt * e_last], axis=0).T.astype(BF16)
        f128[i] = jnp.broadcast_to(e_last, (LANES, LANES)).T

    def stage_av(ci, pp):
        i, _, _ = where(ci, pp)
        s128[i, b128["avbd"]] = bd(_dot(s64[i, b64["aak"]], s128[i, b128["vbd"]]))

    def stage_level(lvl):
        def stage(ci, pp):
            i, _, _ = where(ci, pp)
            lhs = np_s[i, 0:c] if lvl == 0 else (np_s[i, c:] if lvl == 5 else np_s[i])
            out = _dot(lhs, s128[i, b128["nb"]])
            if lvl > 0:
                p = f64[i, q64["p"]] + out[-c:]
                f64[i, q64["p"]] = p
                np_s[i, c:] = p.astype(BF16)
            if lvl < 5:
                n2 = out[0:c]
                np_s[i, 0:c] = n2.astype(BF16)
                s128[i, b128["nb"]] = bd(n2)
        return stage

    def stage_wu(ci, pp):
        i, _, _ = where(ci, pp)
        rhs = jnp.concatenate([s128[i, b128["atbd"]], s128[i, b128["avbd"]]], axis=1)
        wu = _dot(f64[i, q64["p"]].astype(BF16), rhs)
        wr_s[i, 0:c] = wu[:, 0:LANES].astype(BF16)
        f64[i, q64["uc"]] = wu[:, LANES:]

    def stage_state_a(ci, pp):
        i, _, _ = where(ci, pp)
        wr = _dot(wr_s[i], hb_s[pp])
        sa = wr[0:c] + f64[i, q64["uc"]]
        s64[i, b64["sab"]] = sa.astype(BF16)
        s128[i, b128["nb"]] = bd(sa)
        f64[i, q64["y"]] = wr[c:]

    def stage_state_b(ci, pp):
        i, _, _ = where(ci, pp)
        f64[i, q64["y"]] = f64[i, q64["y"]] + _dot(
            jnp.concatenate([s64[i, b64["arb"]], s64[i, b64["ark"]]], axis=1),
            jnp.concatenate([s128[i, b128["nb"]], s128[i, b128["vbd"]]], axis=0))
        h_new = f128[i] * h_ref[pp] + bdmask * _dot(
            s128[i, b128["bkt"]], jnp.concatenate([s64[i, b64["sab"]], s64[i, b64["vb"]]], axis=0))
        h_ref[pp] = h_new
        hb_s[pp] = h_new.astype(BF16)

    def stage_center(ci, pp):
        i, _, _ = where(ci, pp)
        y = f64[i, q64["y"]]
        f64[i, q64["d"]] = y - gsum(y) * (1.0 / RW_HEAD)

    def stage_out(ci, pp):
        i, rs, ls = where(ci, pp)
        prm = prm_ref[:, ls]
        d = f64[i, q64["d"]]
        var = gsum(d * d) * (1.0 / RW_HEAD)
        yn = d * lax.rsqrt(var + GN_EPS) * prm[5:6] + prm[6:7]
        o_ref[0, rs, ls] = ((yn + f64[i, q64["bv"]]) * _silu(z_ref[0, rs, ls])).astype(o_ref.dtype)

    stages = ([stage_gates, stage_intra, stage_av] + [stage_level(lvl) for lvl in range(6)]
              + [stage_wu, stage_state_a, stage_state_b, stage_center, stage_out])
    _emit_skewed(stages, nchunk, pairs)


def _rwkv7(rkvz, wl, al, vl, v_first, prm):
    b, t, d4 = rkvz.shape
    d = d4 // 4
    tc = _tile(t, 256)
    has_vres = vl is not None
    pg = _tile(d // LANES, 8)
    wd = pg * LANES
    c = RW_CHUNK
    nbody = (tc // c) * pg
    ngrp = d // wd

    def sec(s):
        return pl.BlockSpec((1, tc, wd), lambda i, p, j: (i, j, s * ngrp + p))

    flat = pl.BlockSpec((1, tc, wd), lambda i, p, j: (i, j, p))
    in_specs = [sec(0), sec(1), sec(2), sec(3), flat, flat]
    args = [rkvz, rkvz, rkvz, rkvz, wl, al]
    if has_vres:
        in_specs += [flat, sec(2)]
        args += [vl, v_first]
    in_specs.append(pl.BlockSpec((8, wd), lambda i, p, j: (0, p)))
    args.append(prm)
    return pl.pallas_call(
        functools.partial(_rw_kernel, has_vres=has_vres),
        grid=(b, d // wd, t // tc),
        in_specs=in_specs,
        out_specs=flat,
        out_shape=jax.ShapeDtypeStruct((b, t, d), BF16),
        scratch_shapes=[pltpu.VMEM((pg, LANES, LANES), F32),
                        pltpu.VMEM((pg, LANES, LANES), BF16),
                        pltpu.VMEM((nbody, len(_RW_B64), c, LANES), BF16),
                        pltpu.VMEM((nbody, len(_RW_B128), 2 * c, LANES), BF16),
                        pltpu.VMEM((nbody, 2 * c, LANES), BF16),
                        pltpu.VMEM((nbody, 2 * c, LANES), BF16),
                        pltpu.VMEM((nbody, len(_RW_F64), c, LANES), F32),
                        pltpu.VMEM((nbody, LANES, LANES), F32)],
        compiler_params=_params("parallel", "parallel", "arbitrary"),
        name="rwkv7",
    )(*args)


def kernel(x, norm_gains, mix_w_in, dn_conv, dn_a_log, dn_dt_bias, dn_out_norm, hg_lb_logits, hg_out_norm,
           mix_w_out, rw_mu, rw_w_rkvz, rw_w0, rw_w1, rw_w2, rw_a0, rw_a1, rw_a2, rw_v0, rw_v1, rw_v2,
           rw_k_k, rw_k_a, rw_r_k, rw_ln_w, rw_ln_b, rw_w_out, final_norm):
    b, t, d = x.shape
    m = b * t
    depth = norm_gains.shape[0]
    dn_heads = dn_a_log.shape[1]
    dn_width = dn_conv.shape[2] // 3
    hg_width = hg_lb_logits.shape[1]
    hg_heads = hg_width // LANES
    assert dn_width == dn_heads * LANES and hg_out_norm.shape[1] == LANES and rw_r_k.shape[2] == RW_HEAD
    dn_ba = 4 * dn_width
    hg_q = dn_ba + 2 * dn_heads
    assert 2 * dn_heads <= LANES

    h = x.reshape(m, d)
    v_first = None
    for layer in range(depth):
        gain = norm_gains[layer]
        if layer % 2 == 0:
            e = layer // 2
            w_in = mix_w_in[e]
            w_main = jnp.concatenate([w_in[:, :dn_ba], w_in[:, hg_q:]], axis=1).astype(BF16)
            w_ba = jnp.pad(w_in[:, dn_ba:hg_q], ((0, 0), (0, LANES - 2 * dn_heads))).astype(BF16)
            p, ba = _norm_matmul(h, gain, w_main, w_ba)
            p = p.reshape(b, t, -1)
            ba = ba.reshape(b, t, LANES)
            o_a = _deltanet(p, ba, dn_conv[e], dn_a_log[e], dn_dt_bias[e], dn_out_norm[e], dn_heads)
            o_b = _hgrn2(p, dn_ba // LANES, hg_lb_logits, hg_out_norm[e], hg_heads, e)
            h = _out_matmul([o_a.reshape(m, -1), o_b.reshape(m, -1)], mix_w_out[e].astype(BF16), h,
                            final_norm if layer == depth - 1 else None)
        else:
            o = layer // 2
            has_vres = v_first is not None
            rows = [0, 2, 3, 5, 1, 4] + ([3] if has_vres else [])
            downs = [rw_w1[o], rw_a1[o]] + ([rw_v1[o - 1]] if has_vres else [])
            a_down = jnp.stack([jnp.pad(a, ((0, 0), (0, LANES - a.shape[1]))) for a in downs]).astype(BF16)
            rkvz, mids = _mix_matmul(h.reshape(b, t, d), gain, rw_mu[o][jnp.array(rows)],
                                     rw_w_rkvz[o].astype(BF16), a_down)
            rkvz = rkvz.reshape(b, t, 4 * d)
            wl = _lora_up(mids, 0, rw_w2[o], True).reshape(b, t, d)
            al = _lora_up(mids, 1, rw_a2[o], False).reshape(b, t, d)
            vl = None
            v0 = jnp.zeros((d,), F32)
            if has_vres:
                vl = _lora_up(mids, 2, rw_v2[o - 1], False).reshape(b, t, d)
                v0 = rw_v0[o - 1]
            else:
                v_first = rkvz
            prm = jnp.stack([rw_w0[o], rw_a0[o], rw_k_k[o], rw_k_a[o], rw_r_k[o].reshape(d),
                             rw_ln_w[o], rw_ln_b[o], v0]).astype(F32)
            y = _rwkv7(rkvz, wl, al, vl, v_first, prm)
            h = _out_matmul([y.reshape(m, d)], rw_w_out[o].astype(BF16), h,
                            final_norm if layer == depth - 1 else None)
    return h.reshape(b, t, d)
```

```python
import functools
import math

import jax
import jax.numpy as jnp
from jax import lax
from jax.experimental import pallas as pl
from jax.experimental.pallas import tpu as pltpu

F32 = jnp.float32
BF16 = jnp.bfloat16

NORM_EPS = 1e-6
GN_EPS = 64e-5
L2_EPS = 1e-6
CONV_TAPS = 4

LANES = 128
SUBLANES = 8
V7X_VMEM_LIMIT_BYTES = 56 * 1024 * 1024

DN_CHUNK = 64
RW_CHUNK = 64
RW_HEAD = 64
EXP_NEG_HALF = math.exp(-0.5)
MIX_PROLOGUE_ROWS = 256
MIX_DOT_ROWS = 512
DN_STAGE_SKEW = 2
RW_STAGE_SKEW = 4
HG_STAGE_SKEW = 1


def _params(*semantics):
    return pltpu.CompilerParams(dimension_semantics=semantics, vmem_limit_bytes=V7X_VMEM_LIMIT_BYTES)


def _tile(dim, pref):
    t = min(dim, pref)
    assert dim % t == 0, (dim, pref)
    return t


def _dot(a, b):
    return jnp.dot(a, b, preferred_element_type=F32)


def _dot_nt(a, b):
    return lax.dot_general(a, b, (((1,), (1,)), ((), ())), preferred_element_type=F32)


def _bdot_nt(a, b):
    return _dot_nt(a.astype(BF16), b.astype(BF16))


def _dot01(m01, x):
    x1 = x.astype(BF16)
    x2 = (x - x1.astype(F32)).astype(BF16)
    return _dot(m01, x1) + _dot(m01, x2)


def _silu(x):
    return x * jax.nn.sigmoid(x)


def _softplus(x):
    return jnp.maximum(x, 0.0) + jnp.log1p(jnp.exp(-jnp.abs(x)))


def _iota2(shape, dim):
    return lax.broadcasted_iota(jnp.int32, shape, dim)


def _shift_rows(x, halo, j):
    full = pltpu.roll(x, j, axis=0)
    hfix = pltpu.roll(halo, j, axis=0)
    top = jnp.where(_iota2(hfix.shape, 0) < j, hfix, full[0:SUBLANES])
    return jnp.concatenate([top, full[SUBLANES:]], axis=0)


def _emit_skewed(stages, nchunk, width, skew):
    n = len(stages)
    for tau in range(n + skew * (nchunk - 1)):
        for w in range(width):
            for ci in range(nchunk):
                st = tau - skew * ci
                if 0 <= st < n:
                    stages[st](ci, w)


def _rms(x, g):
    return x * lax.rsqrt(jnp.mean(x * x, axis=-1, keepdims=True) + NORM_EPS) * g


def _norm_mm_kernel(x_ref, g_ref, w_ref, wn_ref, o_ref, on_ref, hn_s):
    j = pl.program_id(1)

    @pl.when(j == 0)
    def _():
        hn_s[...] = _rms(x_ref[...], g_ref[...]).astype(BF16)
        on_ref[...] = _dot(hn_s[...], wn_ref[...])

    @pl.when(j > 0)
    def _():
        o_ref[...] = _dot(hn_s[...], w_ref[...])


def _norm_matmul(h, gain, w, w_narrow):
    m, d = h.shape
    n = w.shape[1]
    tm = _tile(m, 1024)
    tn = _tile(n, 1024)
    n_main = n // tn

    def col(j):
        return jnp.maximum(j - 1, 0)

    return pl.pallas_call(
        _norm_mm_kernel,
        grid=(m // tm, n_main + 1),
        in_specs=[pl.BlockSpec((tm, d), lambda i, j: (i, 0)),
                  pl.BlockSpec((1, d), lambda i, j: (0, 0)),
                  pl.BlockSpec((d, tn), lambda i, j: (0, col(j))),
                  pl.BlockSpec((d, LANES), lambda i, j: (0, 0))],
        out_specs=[pl.BlockSpec((tm, tn), lambda i, j: (i, col(j))),
                   pl.BlockSpec((tm, LANES), lambda i, j: (i, 0))],
        out_shape=[jax.ShapeDtypeStruct((m, n), F32), jax.ShapeDtypeStruct((m, LANES), F32)],
        scratch_shapes=[pltpu.VMEM((tm, d), BF16)],
        compiler_params=_params("parallel", "arbitrary"),
        name="norm_matmul",
    )(h, gain.reshape(1, d), w, w_narrow)


def _mix_mm_kernel(x_ref, halo_ref, g_ref, mu_ref, w_ref, a_ref, o_ref, mid_ref, hn_s, xx_s, *,
                   tiles_per_group, tiles_per_seq, n_groups):
    i = pl.program_id(0)
    j = pl.program_id(1)
    tm = hn_s.shape[0]
    rc = min(tm, MIX_PROLOGUE_ROWS)
    mc = min(tm, MIX_DOT_ROWS)

    def mix(rows, mu_row):
        return hn_s[rows, :] + xx_s[rows, :] * mu_row.astype(BF16)

    @pl.when(j == 0)
    def _():
        g = g_ref[...]
        first = lax.rem(i, tiles_per_seq) == 0
        for r in range(tm // rc):
            rows = slice(r * rc, (r + 1) * rc)
            hn = _rms(x_ref[0, rows, :], g)
            if r == 0:
                hp = jnp.where(first, 0.0, _rms(halo_ref[0], g))
            else:
                hp = _rms(x_ref[0, r * rc - SUBLANES:r * rc, :], g)
            hn_s[rows, :] = hn.astype(BF16)
            xx_s[rows, :] = (_shift_rows(hn, hp, 1) - hn).astype(BF16)

        for l in range(a_ref.shape[0]):
            mu_row = mu_ref[n_groups + l:n_groups + l + 1, :]
            for r in range(tm // mc):
                rows = slice(r * mc, (r + 1) * mc)
                mid_ref[rows, l * LANES:(l + 1) * LANES] = _dot(mix(rows, mu_row), a_ref[l])

    @pl.when(j > 0)
    def _():
        mu_row = mu_ref[pl.ds(lax.div(j - 1, tiles_per_group), 1), :]
        for r in range(tm // mc):
            rows = slice(r * mc, (r + 1) * mc)
            o_ref[rows, :] = _dot(mix(rows, mu_row), w_ref[0])


def _mix_matmul(h, gain, mu, w, a_down):
    b, t, d = h.shape
    m = b * t
    g, _, n = w.shape
    nl = a_down.shape[0]
    tm = _tile(t, 1024)
    tn = _tile(n, 1024)
    tpg = n // tn
    n_main = g * tpg
    tps = t // tm
    hb = tm // SUBLANES

    def col(j):
        return jnp.maximum(j - 1, 0)

    return pl.pallas_call(
        functools.partial(_mix_mm_kernel, tiles_per_group=tpg, tiles_per_seq=tps, n_groups=g),
        grid=(m // tm, n_main + 1),
        in_specs=[pl.BlockSpec((1, tm, d), lambda i, j: (i // tps, i % tps, 0)),
                  pl.BlockSpec((1, SUBLANES, d), lambda i, j: (i // tps, jnp.maximum((i % tps) * hb - 1, 0), 0)),
                  pl.BlockSpec((1, d), lambda i, j: (0, 0)),
                  pl.BlockSpec((g + nl, d), lambda i, j: (0, 0)),
                  pl.BlockSpec((1, d, tn), lambda i, j: (col(j) // tpg, 0, col(j) % tpg)),
                  pl.BlockSpec((nl, d, LANES), lambda i, j: (0, 0, 0))],
        out_specs=[pl.BlockSpec((tm, tn), lambda i, j: (i, col(j))),
                   pl.BlockSpec((tm, nl * LANES), lambda i, j: (i, 0))],
        out_shape=[jax.ShapeDtypeStruct((m, g * n), F32), jax.ShapeDtypeStruct((m, nl * LANES), F32)],
        scratch_shapes=[pltpu.VMEM((tm, d), BF16), pltpu.VMEM((tm, d), BF16)],
        compiler_params=_params("parallel", "arbitrary"),
        name="mix_matmul",
    )(h, h, gain.reshape(1, d), mu, w, a_down)


def _lora_up_kernel(mid_ref, b_ref, o_ref, *, use_tanh):
    mid = mid_ref[...]
    if use_tanh:
        mid = jnp.tanh(mid)
    o_ref[...] = _dot(mid.astype(BF16), b_ref[...])


def _lora_up(mids, sel, b_up, use_tanh):
    m = mids.shape[0]
    rank, n = b_up.shape
    b_p = jnp.pad(b_up, ((0, LANES - rank), (0, 0))).astype(BF16)
    tm = _tile(m, 1024)
    return pl.pallas_call(
        functools.partial(_lora_up_kernel, use_tanh=use_tanh),
        grid=(m // tm,),
        in_specs=[pl.BlockSpec((tm, LANES), lambda i: (i, sel)),
                  pl.BlockSpec((LANES, n), lambda i: (0, 0))],
        out_specs=pl.BlockSpec((tm, n), lambda i: (i, 0)),
        out_shape=jax.ShapeDtypeStruct((m, n), F32),
        compiler_params=_params("parallel"),
        name="lora_up",
    )(mids, b_p)


def _out_mm_kernel(*refs, n_x, final_norm):
    x_refs, w_ref, r_ref = refs[:n_x], refs[n_x], refs[n_x + 1]
    acc = r_ref[...]
    k0 = 0
    for x_ref in x_refs:
        kx = x_ref.shape[1]
        acc = acc + _dot(x_ref[...], w_ref[k0:k0 + kx, :])
        k0 += kx
    if final_norm:
        g_ref, o_ref = refs[n_x + 2], refs[n_x + 3]
        o_ref[...] = _rms(acc, g_ref[...])
    else:
        refs[n_x + 2][...] = acc


def _out_matmul(xs, w, residual, final_gain=None):
    m = xs[0].shape[0]
    k, n = w.shape
    tm = _tile(m, 512)
    tn = _tile(n, 2048)
    in_specs = [pl.BlockSpec((tm, x.shape[1]), lambda i, j: (i, 0)) for x in xs]
    in_specs += [pl.BlockSpec((k, tn), lambda i, j: (0, j)), pl.BlockSpec((tm, tn), lambda i, j: (i, j))]
    args = [*xs, w, residual]
    if final_gain is not None:
        assert tn == n
        in_specs.append(pl.BlockSpec((1, n), lambda i, j: (0, 0)))
        args.append(final_gain.reshape(1, n))
    return pl.pallas_call(
        functools.partial(_out_mm_kernel, n_x=len(xs), final_norm=final_gain is not None),
        grid=(m // tm, n // tn),
        in_specs=in_specs,
        out_specs=pl.BlockSpec((tm, tn), lambda i, j: (i, j)),
        out_shape=jax.ShapeDtypeStruct((m, n), F32),
        compiler_params=_params("parallel", "parallel"),
        name="out_matmul",
    )(*args)


_DN_F128 = dict(q=0, k=1, vb=2, bc=3, g=4, u0=5, o=6)
_DN_F64 = dict(p=0, dmat=1)


def _dn_kernel(q_ref, k_ref, v_ref, qh_ref, kh_ref, vh_ref, z_ref, ba_ref, sel_ref, hp_ref, cw_ref, on_ref,
               o_ref, s_ref, sb_s, f128, wq_s, ub_s, nc_s, f64, rhs_s, akd_s, cd_s):
    t = pl.program_id(2)
    tc = q_ref.shape[1]
    heads = s_ref.shape[0]
    c = DN_CHUNK
    qf, qp = _DN_F128, _DN_F64

    @pl.when(t == 0)
    def _():
        s_ref[...] = jnp.zeros_like(s_ref)
        sb_s[...] = jnp.zeros_like(sb_s)

    row = _iota2((c, c), 0)
    col = _iota2((c, c), 1)
    causal = row >= col
    strict = row > col
    strict_f = strict.astype(F32)
    eye = (row == col).astype(F32)
    tril = causal.astype(BF16)
    onorm = on_ref[...]

    def where(ci, hh):
        return ci * heads + hh, slice(ci * c, (ci + 1) * c), slice(hh * LANES, (hh + 1) * LANES)

    def conv_silu(x_ref, h_ref, ci, rs, ls, w):
        x = x_ref[0, rs, ls]
        if ci == 0:
            halo = jnp.where(t > 0, h_ref[0, :, ls], 0.0)
        else:
            halo = x_ref[0, ci * c - SUBLANES:ci * c, ls]
        y = x * w[CONV_TAPS - 1:CONV_TAPS, :]
        for j in range(1, CONV_TAPS):
            y = y + _shift_rows(x, halo, j) * w[CONV_TAPS - 1 - j:CONV_TAPS - j, :]
        return _silu(y)

    def l2n(x):
        return x * lax.rsqrt(jnp.sum(x * x, axis=-1, keepdims=True) + L2_EPS)

    def stage_gates(ci, hh):
        i, rs, ls = where(ci, hh)
        cw = cw_ref[hh]
        sel = sel_ref[hh]
        hp = hp_ref[hh]
        ba = ba_ref[0, rs, :]
        qc = l2n(conv_silu(q_ref, qh_ref, ci, rs, ls, cw[0:CONV_TAPS])) * (LANES ** -0.5)
        kc = l2n(conv_silu(k_ref, kh_ref, ci, rs, ls, cw[CONV_TAPS:2 * CONV_TAPS]))
        vc = conv_silu(v_ref, vh_ref, ci, rs, ls, cw[2 * CONV_TAPS:3 * CONV_TAPS])
        beta = jax.nn.sigmoid(jnp.sum(ba * sel[0:1], axis=-1, keepdims=True))
        alpha_pre = jnp.sum(ba * sel[1:2], axis=-1, keepdims=True)
        bc = jnp.broadcast_to(beta, (c, LANES))
        lac = -jnp.exp(hp[0:1]) * _softplus(jnp.broadcast_to(alpha_pre, (c, LANES)) + hp[1:2])
        f128[i, qf["q"]] = qc
        f128[i, qf["k"]] = kc
        f128[i, qf["vb"]] = vc * bc
        f128[i, qf["bc"]] = bc
        f128[i, qf["g"]] = _dot01(tril, lac)
        f64[i, qp["dmat"]] = _dot01(tril, lac[:, 0:c] * strict_f)

    def stage_intra(ci, hh):
        i, rs, ls = where(ci, hh)
        qc = f128[i, qf["q"]]
        kc = f128[i, qf["k"]]
        bc = f128[i, qf["bc"]]
        g = f128[i, qf["g"]]
        decay = jnp.where(causal, jnp.exp(jnp.where(causal, f64[i, qp["dmat"]], 0.0)), 0.0)
        kb = kc.astype(BF16)
        qkk = _dot_nt(jnp.concatenate([qc.astype(BF16), kb], axis=0), kb)
        lower = jnp.where(strict, bc[:, 0:c] * qkk[c:] * decay, 0.0)
        gamma = jnp.exp(g)
        glast = g[c - 1:c, :]
        akd_s[i, 0:c] = jnp.where(causal, qkk[0:c] * decay, 0.0).astype(BF16)
        nc_s[i] = (-lower).astype(BF16)
        f64[i, qp["p"]] = eye - lower
        rhs_s[i] = jnp.concatenate([f128[i, qf["vb"]], kc * (bc * gamma)], axis=1).astype(BF16)
        wq_s[i, c:] = (qc * gamma).astype(BF16)
        akd_s[i, c:] = (kc * jnp.exp(glast - g)).T.astype(BF16)
        cd_s[i] = jnp.broadcast_to(jnp.exp(glast), (SUBLANES, LANES))

    def stage_square(ci, hh):
        i, _, _ = where(ci, hh)
        nc = nc_s[i]
        nc_s[i] = _dot(nc, nc).astype(BF16)

    def stage_accumulate(ci, hh):
        i, _, _ = where(ci, hh)
        p = f64[i, qp["p"]]
        f64[i, qp["p"]] = p + _dot(p.astype(BF16), nc_s[i])

    def stage_solve(ci, hh):
        i, _, _ = where(ci, hh)
        sol = _dot(f64[i, qp["p"]].astype(BF16), rhs_s[i])
        f128[i, qf["u0"]] = sol[:, 0:LANES]
        wq_s[i, 0:c] = sol[:, LANES:].astype(BF16)

    def stage_state_a(ci, hh):
        i, _, _ = where(ci, hh)
        wq = _dot(wq_s[i], sb_s[hh])
        ub_s[i] = (f128[i, qf["u0"]] - wq[0:c]).astype(BF16)
        f128[i, qf["o"]] = wq[c:]

    def stage_state_b(ci, hh):
        i, _, _ = where(ci, hh)
        ou = _dot(akd_s[i], ub_s[i])
        f128[i, qf["o"]] = f128[i, qf["o"]] + ou[0:c]
        s_new = cd_s[i][0:1] * s_ref[hh] + ou[c:]
        s_ref[hh] = s_new
        sb_s[hh] = s_new.astype(BF16)

    def stage_out(ci, hh):
        i, rs, ls = where(ci, hh)
        o = f128[i, qf["o"]]
        on = o * lax.rsqrt(jnp.mean(o * o, axis=-1, keepdims=True) + NORM_EPS) * onorm
        o_ref[0, rs, ls] = (on * _silu(z_ref[0, rs, ls])).astype(o_ref.dtype)

    stages = ([stage_gates, stage_intra] + [stage_square, stage_accumulate] * 5
              + [stage_solve, stage_state_a, stage_state_b, stage_out])
    _emit_skewed(stages, tc // c, heads, DN_STAGE_SKEW)


def _deltanet(p, ba, conv_w, a_log, dt_bias, out_norm, nheads):
    b, t, _ = p.shape
    tc = _tile(t, 256)
    c = DN_CHUNK
    assert tc % c == 0
    hb = tc // SUBLANES
    nh = nheads
    hg = _tile(nh, 8)
    ng = nh // hg
    wd = hg * LANES
    nbody = (tc // c) * hg
    cw = conv_w.reshape(CONV_TAPS, 3, nh, LANES).transpose(2, 1, 0, 3).reshape(nh, 3 * CONV_TAPS, LANES)
    lane = jnp.arange(LANES)[None, None, :]
    head = jnp.arange(nh)[:, None, None]
    sel = jnp.concatenate([(lane == head), (lane == head + nh)], axis=1).astype(F32)
    hp = jnp.broadcast_to(jnp.stack([a_log, dt_bias], axis=1)[:, :, None], (nh, 2, LANES)).astype(F32)

    def main(sec):
        return pl.BlockSpec((1, tc, wd), lambda i, h, j: (i, j, sec * ng + h))

    def halo(sec):
        return pl.BlockSpec((1, SUBLANES, wd), lambda i, h, j: (i, jnp.maximum(j * hb - 1, 0), sec * ng + h))

    per_head = lambda rows: pl.BlockSpec((hg, rows, LANES), lambda i, h, j: (h, 0, 0))
    return pl.pallas_call(
        _dn_kernel,
        grid=(b, ng, t // tc),
        in_specs=[main(0), main(1), main(2), halo(0), halo(1), halo(2), main(3),
                  pl.BlockSpec((1, tc, LANES), lambda i, h, j: (i, j, 0)),
                  per_head(2), per_head(2), per_head(3 * CONV_TAPS),
                  pl.BlockSpec((1, LANES), lambda i, h, j: (0, 0))],
        out_specs=pl.BlockSpec((1, tc, wd), lambda i, h, j: (i, j, h)),
        out_shape=jax.ShapeDtypeStruct((b, t, nh * LANES), BF16),
        scratch_shapes=[pltpu.VMEM((hg, LANES, LANES), F32),
                        pltpu.VMEM((hg, LANES, LANES), BF16),
                        pltpu.VMEM((nbody, len(_DN_F128), c, LANES), F32),
                        pltpu.VMEM((nbody, 2 * c, LANES), BF16),
                        pltpu.VMEM((nbody, c, LANES), BF16),
                        pltpu.VMEM((nbody, c, c), BF16),
                        pltpu.VMEM((nbody, len(_DN_F64), c, c), F32),
                        pltpu.VMEM((nbody, c, 2 * LANES), BF16),
                        pltpu.VMEM((nbody, c + LANES, c), BF16),
                        pltpu.VMEM((nbody, SUBLANES, LANES), F32)],
        compiler_params=_params("parallel", "parallel", "arbitrary"),
        name="deltanet",
    )(p, p, p, p, p, p, p, ba, sel, hp, cw, out_norm.reshape(1, LANES))


HG_LEVELS = 6
HG_BLOCK = 64


def _hg_tables():
    import numpy as np
    c = HG_BLOCK
    t = np.arange(c)[:, None]
    i = np.arange(c)[None, :]
    rows = [i <= t, i > t]
    q_rows, k_rows = [], []
    level = np.full((c, c), -1, np.int32)
    level[np.arange(c), np.arange(c)] = HG_LEVELS
    for li in range(HG_LEVELS):
        m = c >> (li + 1)
        pos = t % (2 * m)
        upper = pos >= m
        ref = t - pos + m - 1
        q_rows.append(upper & (i > ref) & (i <= t))
        k_rows.append((~upper) & (i > t) & (i <= ref))
        same = (t // (2 * m)) == (i // (2 * m))
        level[same & upper & ((i % (2 * m)) < m)] = li
    table = np.concatenate(rows + q_rows + k_rows, axis=0).astype(np.float32)
    return table, level


def _hg_kernel(q_ref, f_ref, i_ref, z_ref, lb_ref, on_ref, tab_ref, lvl_ref, o_ref,
               s_ref, sb_s, qt_s, kt_s, qb_s, vb_s, klt_s, att_s, ecol_s, inc_s, o_s, *, layer):
    t = pl.program_id(2)
    tc = q_ref.shape[1]
    heads = s_ref.shape[0]
    c = HG_BLOCK
    nl = HG_LEVELS

    @pl.when(t == 0)
    def _():
        s_ref[...] = jnp.zeros_like(s_ref)
        sb_s[...] = jnp.zeros_like(sb_s)

    table = tab_ref[...]
    level = lvl_ref[...]
    onorm = on_ref[...]
    def where(ci, hh):
        return ci * heads + hh, slice(ci * c, (ci + 1) * c), slice(hh * LANES, (hh + 1) * LANES)

    def stage_gates(ci, hh):
        i, rs, ls = where(ci, hh)
        logits = lb_ref[:, ls]
        e = jnp.exp(logits - jnp.max(logits, axis=0, keepdims=True))
        share = e / jnp.sum(e, axis=0, keepdims=True)
        lbound = jnp.sum(share[0:layer + 1], axis=0, keepdims=True) - share[0:1]
        q = _silu(q_ref[0, rs, ls]) * (LANES ** -0.5)
        forget = lbound + (1.0 - lbound) * jax.nn.sigmoid(f_ref[0, rs, ls])
        k = 1.0 - forget
        d = _dot(table, jnp.log(forget).astype(BF16))
        b = d[0:c]
        qb_s[i] = (q * jnp.exp(b)).astype(BF16)
        klt_s[i] = (k * jnp.exp(d[c:2 * c])).T.astype(BF16)
        ecol_s[i] = jnp.broadcast_to(jnp.exp(b[c - 1:c]), (LANES, LANES)).T
        for li in range(nl):
            qt_s[i, li] = (q * jnp.exp(d[(2 + li) * c:(3 + li) * c])).astype(BF16)
            kt_s[i, li] = (k * jnp.exp(d[(2 + nl + li) * c:(3 + nl + li) * c])).astype(BF16)
        qt_s[i, nl] = q.astype(BF16)
        kt_s[i, nl] = k.astype(BF16)
        vb_s[i] = i_ref[0, rs, ls].astype(BF16)

    def stage_intra(ci, hh):
        i, _, _ = where(ci, hh)
        att = jnp.zeros((c, c), F32)
        for li in range(nl + 1):
            att = att + jnp.where(level == li, _dot_nt(qt_s[i, li], kt_s[i, li]), 0.0)
        att_s[i] = att.astype(BF16)
        inc_s[i] = _dot(klt_s[i], vb_s[i])

    def stage_att_v(ci, hh):
        i, _, _ = where(ci, hh)
        o_s[i] = _dot(att_s[i], vb_s[i])

    def stage_state(ci, hh):
        i, _, _ = where(ci, hh)
        o_s[i] = o_s[i] + _dot(qb_s[i], sb_s[hh])
        s_new = ecol_s[i] * s_ref[hh] + inc_s[i]
        s_ref[hh] = s_new
        sb_s[hh] = s_new.astype(BF16)

    def stage_out(ci, hh):
        i, rs, ls = where(ci, hh)
        o = o_s[i]
        on = o * lax.rsqrt(jnp.mean(o * o, axis=-1, keepdims=True) + NORM_EPS) * onorm
        o_ref[0, rs, ls] = (on * _silu(z_ref[0, rs, ls])).astype(o_ref.dtype)

    _emit_skewed([stage_gates, stage_intra, stage_att_v, stage_state, stage_out], tc // c, heads, HG_STAGE_SKEW)


def _hgrn2(p, col0, lb_logits, out_norm, nheads, layer):
    b, t, _ = p.shape
    tc = _tile(t, 256)
    c = HG_BLOCK
    assert tc % c == 0
    nh = nheads
    hg = _tile(nh, 8)
    ng = nh // hg
    wd = hg * LANES
    assert col0 % hg == 0
    cg = col0 // hg
    nbody = (tc // c) * hg
    table, level = _hg_tables()
    nrow = table.shape[0]

    def main(sec):
        return pl.BlockSpec((1, tc, wd), lambda i, h, j: (i, j, cg + sec * ng + h))

    nl = lb_logits.shape[0]
    const = lambda shape: pl.BlockSpec(shape, lambda i, h, j: (0, 0))
    return pl.pallas_call(
        functools.partial(_hg_kernel, layer=layer),
        grid=(b, ng, t // tc),
        in_specs=[main(0), main(1), main(2), main(3),
                  pl.BlockSpec((nl, wd), lambda i, h, j: (0, h)),
                  const((1, LANES)), const((nrow, c)), const((c, c))],
        out_specs=pl.BlockSpec((1, tc, wd), lambda i, h, j: (i, j, h)),
        out_shape=jax.ShapeDtypeStruct((b, t, nh * LANES), BF16),
        scratch_shapes=[pltpu.VMEM((hg, LANES, LANES), F32),
                        pltpu.VMEM((hg, LANES, LANES), BF16),
                        pltpu.VMEM((nbody, HG_LEVELS + 1, c, LANES), BF16),
                        pltpu.VMEM((nbody, HG_LEVELS + 1, c, LANES), BF16),
                        pltpu.VMEM((nbody, c, LANES), BF16),
                        pltpu.VMEM((nbody, c, LANES), BF16),
                        pltpu.VMEM((nbody, LANES, c), BF16),
                        pltpu.VMEM((nbody, c, c), BF16),
                        pltpu.VMEM((nbody, LANES, LANES), F32),
                        pltpu.VMEM((nbody, LANES, LANES), F32),
                        pltpu.VMEM((nbody, c, LANES), F32)],
        compiler_params=_params("parallel", "parallel", "arbitrary"),
        name="hgrn2",
    )(p, p, p, p, lb_logits, out_norm.reshape(1, LANES), jnp.asarray(table, BF16), jnp.asarray(level))


_RW_B64 = dict(aak=0, vb=1, arb=2, ark=3, sab=4)
_RW_B128 = dict(nb=0, vbd=1, bkt=2, atbd=3, avbd=4)
_RW_F64 = dict(p=0, uc=1, g=2, bv=3, lw=4, kkr=5, k2=6, ag=7, v=8, ssq=9, y=10, d=11)


def _rw_kernel(*refs, has_vres):
    if has_vres:
        (r_ref, k_ref, v_ref, z_ref, wl_ref, al_ref, vl_ref, vf_ref, prm_ref,
         o_ref, h_ref, hb_s, s64, s128, wr_s, np_s, f64, f128) = refs
    else:
        (r_ref, k_ref, v_ref, z_ref, wl_ref, al_ref, prm_ref,
         o_ref, h_ref, hb_s, s64, s128, wr_s, np_s, f64, f128) = refs
    t = pl.program_id(2)
    tc = o_ref.shape[1]
    pairs = h_ref.shape[0]
    c = RW_CHUNK
    nchunk = tc // c
    b64, b128, q64 = _RW_B64, _RW_B128, _RW_F64

    @pl.when(t == 0)
    def _():
        h_ref[...] = jnp.zeros_like(h_ref)
        hb_s[...] = jnp.zeros_like(hb_s)

    lane1 = _iota2((1, LANES), 1)
    head0 = lane1 < RW_HEAD
    m0 = head0.astype(F32)
    m1 = 1.0 - m0
    same_head = (_iota2((LANES, LANES), 0) < RW_HEAD) == (_iota2((LANES, LANES), 1) < RW_HEAD)
    bdmask = same_head.astype(F32)
    row = _iota2((c, LANES), 0)
    scol = jnp.bitwise_and(_iota2((c, LANES), 1), RW_HEAD - 1)
    incl = row >= scol
    strict = row > scol
    eye_cat = (row == scol).astype(F32)
    tril = (_iota2((c, c), 0) >= _iota2((c, c), 1)).astype(BF16)

    def bd(x):
        return jnp.concatenate([x * m0, x * m1], axis=0).astype(BF16)

    def gsum(x):
        s0 = jnp.sum(jnp.where(head0, x, 0.0), axis=-1, keepdims=True)
        s1 = jnp.sum(x, axis=-1, keepdims=True) - s0
        return jnp.where(head0, s0, s1)

    def where(ci, pp):
        return ci * pairs + pp, slice(ci * c, (ci + 1) * c), slice(pp * LANES, (pp + 1) * LANES)

    def stage_gates(ci, pp):
        i, rs, ls = where(ci, pp)
        prm = prm_ref[:, ls]
        w0, a0, k_k, k_a, r_k, v0 = prm[0:1], prm[1:2], prm[2:3], prm[3:4], prm[4:5], prm[7:8]
        r = r_ref[0, rs, ls]
        k = k_ref[0, rs, ls]
        v = v_ref[0, rs, ls]
        lw = -EXP_NEG_HALF * jax.nn.sigmoid(w0 + wl_ref[0, rs, ls])
        ag = jax.nn.sigmoid(a0 + al_ref[0, rs, ls])
        if has_vres:
            v = v + (vf_ref[0, rs, ls] - v) * jax.nn.sigmoid(v0 + vl_ref[0, rs, ls])
        kkr = k * k_k
        k2 = k * (1.0 + (ag - 1.0) * k_a)
        sums = gsum(jnp.concatenate([kkr * kkr, r * k2 * r_k], axis=0))
        f64[i, q64["g"]] = _dot01(tril, lw)
        f64[i, q64["lw"]] = lw
        f64[i, q64["ag"]] = ag
        f64[i, q64["v"]] = v
        f64[i, q64["kkr"]] = kkr
        f64[i, q64["k2"]] = k2
        f64[i, q64["ssq"]] = sums[0:c]
        f64[i, q64["bv"]] = sums[c:] * v

    def stage_intra(ci, pp):
        i, rs, ls = where(ci, pp)
        g = f64[i, q64["g"]]
        lw = f64[i, q64["lw"]]
        v = f64[i, q64["v"]]
        kk = f64[i, q64["kkr"]] * lax.rsqrt(f64[i, q64["ssq"]] + L2_EPS)
        e_g = jnp.exp(g)
        e_gn = jnp.exp(-g)
        rt = r_ref[0, rs, ls] * e_g
        at = -kk * jnp.exp(g - lw)
        kt = f64[i, q64["k2"]] * e_gn
        bt = kk * f64[i, q64["ag"]] * e_gn
        x = _bdot_nt(jnp.concatenate([at, rt], axis=0), jnp.concatenate([bd(kt), bd(bt)], axis=0))
        a_ab = jnp.where(strict, x[0:c, LANES:], 0.0)
        e_last = e_g[c - 1:c, :]
        s64[i, b64["aak"]] = jnp.where(strict, x[0:c, 0:LANES], 0.0).astype(BF16)
        s64[i, b64["ark"]] = jnp.where(incl, x[c:, 0:LANES], 0.0).astype(BF16)
        s64[i, b64["arb"]] = jnp.where(incl, x[c:, LANES:], 0.0).astype(BF16)
        np_s[i, 0:c] = a_ab.astype(BF16)
        np_s[i, c:] = (eye_cat + a_ab).astype(BF16)
        s128[i, b128["nb"]] = bd(a_ab)
        f64[i, q64["p"]] = eye_cat + a_ab
        wr_s[i, c:] = rt.astype(BF16)
        s64[i, b64["vb"]] = v.astype(BF16)
        s128[i, b128["atbd"]] = bd(at)
        s128[i, b128["vbd"]] = bd(v)
        s128[i, b128["bkt"]] = jnp.concatenate([bt * e_last, kt * e_last], axis=0).T.astype(BF16)
        f128[i] = jnp.broadcast_to(e_last, (LANES, LANES)).T

    def stage_av(ci, pp):
        i, _, _ = where(ci, pp)
        s128[i, b128["avbd"]] = bd(_dot(s64[i, b64["aak"]], s128[i, b128["vbd"]]))

    def stage_level(lvl):
        def stage(ci, pp):
            i, _, _ = where(ci, pp)
            lhs = np_s[i, 0:c] if lvl == 0 else (np_s[i, c:] if lvl == 5 else np_s[i])
            out = _dot(lhs, s128[i, b128["nb"]])
            if lvl > 0:
                p = f64[i, q64["p"]] + out[-c:]
                f64[i, q64["p"]] = p
                np_s[i, c:] = p.astype(BF16)
            if lvl < 5:
                n2 = out[0:c]
                np_s[i, 0:c] = n2.astype(BF16)
                s128[i, b128["nb"]] = bd(n2)
        return stage

    def stage_wu(ci, pp):
        i, _, _ = where(ci, pp)
        rhs = jnp.concatenate([s128[i, b128["atbd"]], s128[i, b128["avbd"]]], axis=1)
        wu = _dot(f64[i, q64["p"]].astype(BF16), rhs)
        wr_s[i, 0:c] = wu[:, 0:LANES].astype(BF16)
        f64[i, q64["uc"]] = wu[:, LANES:]

    def stage_state_a(ci, pp):
        i, _, _ = where(ci, pp)
        wr = _dot(wr_s[i], hb_s[pp])
        sa = wr[0:c] + f64[i, q64["uc"]]
        s64[i, b64["sab"]] = sa.astype(BF16)
        s128[i, b128["nb"]] = bd(sa)
        f64[i, q64["y"]] = wr[c:]

    def stage_state_b(ci, pp):
        i, _, _ = where(ci, pp)
        f64[i, q64["y"]] = f64[i, q64["y"]] + _dot(
            jnp.concatenate([s64[i, b64["arb"]], s64[i, b64["ark"]]], axis=1),
            jnp.concatenate([s128[i, b128["nb"]], s128[i, b128["vbd"]]], axis=0))
        h_new = f128[i] * h_ref[pp] + bdmask * _dot(
            s128[i, b128["bkt"]], jnp.concatenate([s64[i, b64["sab"]], s64[i, b64["vb"]]], axis=0))
        h_ref[pp] = h_new
        hb_s[pp] = h_new.astype(BF16)

    def stage_center(ci, pp):
        i, _, _ = where(ci, pp)
        y = f64[i, q64["y"]]
        f64[i, q64["d"]] = y - gsum(y) * (1.0 / RW_HEAD)

    def stage_out(ci, pp):
        i, rs, ls = where(ci, pp)
        prm = prm_ref[:, ls]
        d = f64[i, q64["d"]]
        var = gsum(d * d) * (1.0 / RW_HEAD)
        yn = d * lax.rsqrt(var + GN_EPS) * prm[5:6] + prm[6:7]
        o_ref[0, rs, ls] = ((yn + f64[i, q64["bv"]]) * _silu(z_ref[0, rs, ls])).astype(o_ref.dtype)

    stages = ([stage_gates, stage_intra, stage_av] + [stage_level(lvl) for lvl in range(6)]
              + [stage_wu, stage_state_a, stage_state_b, stage_center, stage_out])
    _emit_skewed(stages, nchunk, pairs, RW_STAGE_SKEW)


def _rwkv7(rkvz, wl, al, vl, v_first, prm):
    b, t, d4 = rkvz.shape
    d = d4 // 4
    tc = _tile(t, 256)
    has_vres = vl is not None
    pg = _tile(d // LANES, 8)
    wd = pg * LANES
    c = RW_CHUNK
    nbody = (tc // c) * pg
    ngrp = d // wd

    def sec(s):
        return pl.BlockSpec((1, tc, wd), lambda i, p, j: (i, j, s * ngrp + p))

    flat = pl.BlockSpec((1, tc, wd), lambda i, p, j: (i, j, p))
    in_specs = [sec(0), sec(1), sec(2), sec(3), flat, flat]
    args = [rkvz, rkvz, rkvz, rkvz, wl, al]
    if has_vres:
        in_specs += [flat, sec(2)]
        args += [vl, v_first]
    in_specs.append(pl.BlockSpec((8, wd), lambda i, p, j: (0, p)))
    args.append(prm)
    return pl.pallas_call(
        functools.partial(_rw_kernel, has_vres=has_vres),
        grid=(b, d // wd, t // tc),
        in_specs=in_specs,
        out_specs=flat,
        out_shape=jax.ShapeDtypeStruct((b, t, d), BF16),
        scratch_shapes=[pltpu.VMEM((pg, LANES, LANES), F32),
                        pltpu.VMEM((pg, LANES, LANES), BF16),
                        pltpu.VMEM((nbody, len(_RW_B64), c, LANES), BF16),
                        pltpu.VMEM((nbody, len(_RW_B128), 2 * c, LANES), BF16),
                        pltpu.VMEM((nbody, 2 * c, LANES), BF16),
                        pltpu.VMEM((nbody, 2 * c, LANES), BF16),
                        pltpu.VMEM((nbody, len(_RW_F64), c, LANES), F32),
                        pltpu.VMEM((nbody, LANES, LANES), F32)],
        compiler_params=_params("parallel", "parallel", "arbitrary"),
        name="rwkv7",
    )(*args)


def kernel(x, norm_gains, mix_w_in, dn_conv, dn_a_log, dn_dt_bias, dn_out_norm, hg_lb_logits, hg_out_norm,
           mix_w_out, rw_mu, rw_w_rkvz, rw_w0, rw_w1, rw_w2, rw_a0, rw_a1, rw_a2, rw_v0, rw_v1, rw_v2,
           rw_k_k, rw_k_a, rw_r_k, rw_ln_w, rw_ln_b, rw_w_out, final_norm):
    b, t, d = x.shape
    m = b * t
    depth = norm_gains.shape[0]
    dn_heads = dn_a_log.shape[1]
    dn_width = dn_conv.shape[2] // 3
    hg_width = hg_lb_logits.shape[1]
    hg_heads = hg_width // LANES
    assert dn_width == dn_heads * LANES and hg_out_norm.shape[1] == LANES and rw_r_k.shape[2] == RW_HEAD
    dn_ba = 4 * dn_width
    hg_q = dn_ba + 2 * dn_heads
    assert 2 * dn_heads <= LANES

    h = x.reshape(m, d)
    v_first = None
    for layer in range(depth):
        gain = norm_gains[layer]
        if layer % 2 == 0:
            e = layer // 2
            w_in = mix_w_in[e]
            w_main = jnp.concatenate([w_in[:, :dn_ba], w_in[:, hg_q:]], axis=1).astype(BF16)
            w_ba = jnp.pad(w_in[:, dn_ba:hg_q], ((0, 0), (0, LANES - 2 * dn_heads))).astype(BF16)
            p, ba = _norm_matmul(h, gain, w_main, w_ba)
            p = p.reshape(b, t, -1)
            ba = ba.reshape(b, t, LANES)
            o_a = _deltanet(p, ba, dn_conv[e], dn_a_log[e], dn_dt_bias[e], dn_out_norm[e], dn_heads)
            o_b = _hgrn2(p, dn_ba // LANES, hg_lb_logits, hg_out_norm[e], hg_heads, e)
            h = _out_matmul([o_a.reshape(m, -1), o_b.reshape(m, -1)], mix_w_out[e].astype(BF16), h,
                            final_norm if layer == depth - 1 else None)
        else:
            o = layer // 2
            has_vres = v_first is not None
            rows = [0, 2, 3, 5, 1, 4] + ([3] if has_vres else [])
            downs = [rw_w1[o], rw_a1[o]] + ([rw_v1[o - 1]] if has_vres else [])
            a_down = jnp.stack([jnp.pad(a, ((0, 0), (0, LANES - a.shape[1]))) for a in downs]).astype(BF16)
            rkvz, mids = _mix_matmul(h.reshape(b, t, d), gain, rw_mu[o][jnp.array(rows)],
                                     rw_w_rkvz[o].astype(BF16), a_down)
            rkvz = rkvz.reshape(b, t, 4 * d)
            wl = _lora_up(mids, 0, rw_w2[o], True).reshape(b, t, d)
            al = _lora_up(mids, 1, rw_a2[o], False).reshape(b, t, d)
            vl = None
            v0 = jnp.zeros((d,), F32)
            if has_vres:
                vl = _lora_up(mids, 2, rw_v2[o - 1], False).reshape(b, t, d)
                v0 = rw_v0[o - 1]
            else:
                v_first = rkvz
            prm = jnp.stack([rw_w0[o], rw_a0[o], rw_k_k[o], rw_k_a[o], rw_r_k[o].reshape(d),
                             rw_ln_w[o], rw_ln_b[o], v0]).astype(F32)
            y = _rwkv7(rkvz, wl, al, vl, v_first, prm)
            h = _out_matmul([y.reshape(m, d)], rw_w_out[o].astype(BF16), h,
                            final_norm if layer == depth - 1 else None)
    return h.reshape(b, t, d)
```

```python
import functools
import math

import jax
import jax.numpy as jnp
from jax import lax
from jax.experimental import pallas as pl
from jax.experimental.pallas import tpu as pltpu

F32 = jnp.float32
BF16 = jnp.bfloat16

NORM_EPS = 1e-6
GN_EPS = 64e-5
L2_EPS = 1e-6
CONV_TAPS = 4

LANES = 128
SUBLANES = 8
V7X_VMEM_LIMIT_BYTES = 56 * 1024 * 1024

DN_CHUNK = 64
RW_CHUNK = 64
RW_HEAD = 64
EXP_NEG_HALF = math.exp(-0.5)
MIX_PROLOGUE_ROWS = 256
MIX_DOT_ROWS = 512
DN_STAGE_SKEW = 1
RW_STAGE_SKEW = 3
HG_STAGE_SKEW = 0


def _params(*semantics):
    return pltpu.CompilerParams(dimension_semantics=semantics, vmem_limit_bytes=V7X_VMEM_LIMIT_BYTES)


def _tile(dim, pref):
    t = min(dim, pref)
    assert dim % t == 0, (dim, pref)
    return t


def _dot(a, b):
    return jnp.dot(a, b, preferred_element_type=F32)


def _dot_nt(a, b):
    return lax.dot_general(a, b, (((1,), (1,)), ((), ())), preferred_element_type=F32)


def _bdot_nt(a, b):
    return _dot_nt(a.astype(BF16), b.astype(BF16))


def _dot01(m01, x):
    x1 = x.astype(BF16)
    x2 = (x - x1.astype(F32)).astype(BF16)
    return _dot(m01, x1) + _dot(m01, x2)


def _silu(x):
    return x * jax.nn.sigmoid(x)


def _softplus(x):
    return jnp.maximum(x, 0.0) + jnp.log1p(jnp.exp(-jnp.abs(x)))


def _iota2(shape, dim):
    return lax.broadcasted_iota(jnp.int32, shape, dim)


def _shift_rows(x, halo, j):
    full = pltpu.roll(x, j, axis=0)
    hfix = pltpu.roll(halo, j, axis=0)
    top = jnp.where(_iota2(hfix.shape, 0) < j, hfix, full[0:SUBLANES])
    return jnp.concatenate([top, full[SUBLANES:]], axis=0)


def _emit_skewed(stages, nchunk, width, skew):
    n = len(stages)
    for tau in range(n + skew * (nchunk - 1)):
        for w in range(width):
            for ci in range(nchunk):
                st = tau - skew * ci
                if 0 <= st < n:
                    stages[st](ci, w)


def _rms(x, g):
    return x * lax.rsqrt(jnp.mean(x * x, axis=-1, keepdims=True) + NORM_EPS) * g


def _norm_mm_kernel(x_ref, g_ref, w_ref, wn_ref, o_ref, on_ref, hn_s):
    j = pl.program_id(1)

    @pl.when(j == 0)
    def _():
        hn_s[...] = _rms(x_ref[...], g_ref[...]).astype(BF16)
        on_ref[...] = _dot(hn_s[...], wn_ref[...])

    @pl.when(j > 0)
    def _():
        o_ref[...] = _dot(hn_s[...], w_ref[...])


def _norm_matmul(h, gain, w, w_narrow):
    m, d = h.shape
    n = w.shape[1]
    tm = _tile(m, 1024)
    tn = _tile(n, 1024)
    n_main = n // tn

    def col(j):
        return jnp.maximum(j - 1, 0)

    return pl.pallas_call(
        _norm_mm_kernel,
        grid=(m // tm, n_main + 1),
        in_specs=[pl.BlockSpec((tm, d), lambda i, j: (i, 0)),
                  pl.BlockSpec((1, d), lambda i, j: (0, 0)),
                  pl.BlockSpec((d, tn), lambda i, j: (0, col(j))),
                  pl.BlockSpec((d, LANES), lambda i, j: (0, 0))],
        out_specs=[pl.BlockSpec((tm, tn), lambda i, j: (i, col(j))),
                   pl.BlockSpec((tm, LANES), lambda i, j: (i, 0))],
        out_shape=[jax.ShapeDtypeStruct((m, n), F32), jax.ShapeDtypeStruct((m, LANES), F32)],
        scratch_shapes=[pltpu.VMEM((tm, d), BF16)],
        compiler_params=_params("parallel", "arbitrary"),
        name="norm_matmul",
    )(h, gain.reshape(1, d), w, w_narrow)


def _mix_mm_kernel(x_ref, halo_ref, g_ref, mu_ref, w_ref, a_ref, o_ref, mid_ref, hn_s, xx_s, *,
                   tiles_per_group, tiles_per_seq, n_groups):
    i = pl.program_id(0)
    j = pl.program_id(1)
    tm = hn_s.shape[0]
    rc = min(tm, MIX_PROLOGUE_ROWS)
    mc = min(tm, MIX_DOT_ROWS)

    def mix(rows, mu_row):
        return hn_s[rows, :] + xx_s[rows, :] * mu_row.astype(BF16)

    @pl.when(j == 0)
    def _():
        g = g_ref[...]
        first = lax.rem(i, tiles_per_seq) == 0
        for r in range(tm // rc):
            rows = slice(r * rc, (r + 1) * rc)
            hn = _rms(x_ref[0, rows, :], g)
            if r == 0:
                hp = jnp.where(first, 0.0, _rms(halo_ref[0], g))
            else:
                hp = _rms(x_ref[0, r * rc - SUBLANES:r * rc, :], g)
            hn_s[rows, :] = hn.astype(BF16)
            xx_s[rows, :] = (_shift_rows(hn, hp, 1) - hn).astype(BF16)

        for l in range(a_ref.shape[0]):
            mu_row = mu_ref[n_groups + l:n_groups + l + 1, :]
            for r in range(tm // mc):
                rows = slice(r * mc, (r + 1) * mc)
                mid_ref[rows, l * LANES:(l + 1) * LANES] = _dot(mix(rows, mu_row), a_ref[l])

    @pl.when(j > 0)
    def _():
        mu_row = mu_ref[pl.ds(lax.div(j - 1, tiles_per_group), 1), :]
        for r in range(tm // mc):
            rows = slice(r * mc, (r + 1) * mc)
            o_ref[rows, :] = _dot(mix(rows, mu_row), w_ref[0])


def _mix_matmul(h, gain, mu, w, a_down):
    b, t, d = h.shape
    m = b * t
    g, _, n = w.shape
    nl = a_down.shape[0]
    tm = _tile(t, 1024)
    tn = _tile(n, 1024)
    tpg = n // tn
    n_main = g * tpg
    tps = t // tm
    hb = tm // SUBLANES

    def col(j):
        return jnp.maximum(j - 1, 0)

    return pl.pallas_call(
        functools.partial(_mix_mm_kernel, tiles_per_group=tpg, tiles_per_seq=tps, n_groups=g),
        grid=(m // tm, n_main + 1),
        in_specs=[pl.BlockSpec((1, tm, d), lambda i, j: (i // tps, i % tps, 0)),
                  pl.BlockSpec((1, SUBLANES, d), lambda i, j: (i // tps, jnp.maximum((i % tps) * hb - 1, 0), 0)),
                  pl.BlockSpec((1, d), lambda i, j: (0, 0)),
                  pl.BlockSpec((g + nl, d), lambda i, j: (0, 0)),
                  pl.BlockSpec((1, d, tn), lambda i, j: (col(j) // tpg, 0, col(j) % tpg)),
                  pl.BlockSpec((nl, d, LANES), lambda i, j: (0, 0, 0))],
        out_specs=[pl.BlockSpec((tm, tn), lambda i, j: (i, col(j))),
                   pl.BlockSpec((tm, nl * LANES), lambda i, j: (i, 0))],
        out_shape=[jax.ShapeDtypeStruct((m, g * n), F32), jax.ShapeDtypeStruct((m, nl * LANES), F32)],
        scratch_shapes=[pltpu.VMEM((tm, d), BF16), pltpu.VMEM((tm, d), BF16)],
        compiler_params=_params("parallel", "arbitrary"),
        name="mix_matmul",
    )(h, h, gain.reshape(1, d), mu, w, a_down)


def _lora_up_kernel(mid_ref, b_ref, o_ref, *, use_tanh):
    mid = mid_ref[...]
    if use_tanh:
        mid = jnp.tanh(mid)
    o_ref[...] = _dot(mid.astype(BF16), b_ref[...])


def _lora_up(mids, sel, b_up, use_tanh):
    m = mids.shape[0]
    rank, n = b_up.shape
    b_p = jnp.pad(b_up, ((0, LANES - rank), (0, 0))).astype(BF16)
    tm = _tile(m, 1024)
    return pl.pallas_call(
        functools.partial(_lora_up_kernel, use_tanh=use_tanh),
        grid=(m // tm,),
        in_specs=[pl.BlockSpec((tm, LANES), lambda i: (i, sel)),
                  pl.BlockSpec((LANES, n), lambda i: (0, 0))],
        out_specs=pl.BlockSpec((tm, n), lambda i: (i, 0)),
        out_shape=jax.ShapeDtypeStruct((m, n), F32),
        compiler_params=_params("parallel"),
        name="lora_up",
    )(mids, b_p)


def _out_mm_kernel(*refs, n_x, final_norm):
    x_refs, w_ref, r_ref = refs[:n_x], refs[n_x], refs[n_x + 1]
    acc = r_ref[...]
    k0 = 0
    for x_ref in x_refs:
        kx = x_ref.shape[1]
        acc = acc + _dot(x_ref[...], w_ref[k0:k0 + kx, :])
        k0 += kx
    if final_norm:
        g_ref, o_ref = refs[n_x + 2], refs[n_x + 3]
        o_ref[...] = _rms(acc, g_ref[...])
    else:
        refs[n_x + 2][...] = acc


def _out_matmul(xs, w, residual, final_gain=None):
    m = xs[0].shape[0]
    k, n = w.shape
    tm = _tile(m, 512)
    tn = _tile(n, 2048)
    in_specs = [pl.BlockSpec((tm, x.shape[1]), lambda i, j: (i, 0)) for x in xs]
    in_specs += [pl.BlockSpec((k, tn), lambda i, j: (0, j)), pl.BlockSpec((tm, tn), lambda i, j: (i, j))]
    args = [*xs, w, residual]
    if final_gain is not None:
        assert tn == n
        in_specs.append(pl.BlockSpec((1, n), lambda i, j: (0, 0)))
        args.append(final_gain.reshape(1, n))
    return pl.pallas_call(
        functools.partial(_out_mm_kernel, n_x=len(xs), final_norm=final_gain is not None),
        grid=(m // tm, n // tn),
        in_specs=in_specs,
        out_specs=pl.BlockSpec((tm, tn), lambda i, j: (i, j)),
        out_shape=jax.ShapeDtypeStruct((m, n), F32),
        compiler_params=_params("parallel", "parallel"),
        name="out_matmul",
    )(*args)


_DN_F128 = dict(q=0, k=1, vb=2, bc=3, g=4, u0=5, o=6)
_DN_F64 = dict(p=0, dmat=1)


def _dn_kernel(q_ref, k_ref, v_ref, qh_ref, kh_ref, vh_ref, z_ref, ba_ref, sel_ref, hp_ref, cw_ref, on_ref,
               o_ref, s_ref, sb_s, f128, wq_s, ub_s, nc_s, f64, rhs_s, akd_s, cd_s):
    t = pl.program_id(2)
    tc = q_ref.shape[1]
    heads = s_ref.shape[0]
    c = DN_CHUNK
    qf, qp = _DN_F128, _DN_F64

    @pl.when(t == 0)
    def _():
        s_ref[...] = jnp.zeros_like(s_ref)
        sb_s[...] = jnp.zeros_like(sb_s)

    row = _iota2((c, c), 0)
    col = _iota2((c, c), 1)
    causal = row >= col
    strict = row > col
    strict_f = strict.astype(F32)
    eye = (row == col).astype(F32)
    tril = causal.astype(BF16)
    onorm = on_ref[...]

    def where(ci, hh):
        return ci * heads + hh, slice(ci * c, (ci + 1) * c), slice(hh * LANES, (hh + 1) * LANES)

    def conv_silu(x_ref, h_ref, ci, rs, ls, w):
        x = x_ref[0, rs, ls]
        if ci == 0:
            halo = jnp.where(t > 0, h_ref[0, :, ls], 0.0)
        else:
            halo = x_ref[0, ci * c - SUBLANES:ci * c, ls]
        y = x * w[CONV_TAPS - 1:CONV_TAPS, :]
        for j in range(1, CONV_TAPS):
            y = y + _shift_rows(x, halo, j) * w[CONV_TAPS - 1 - j:CONV_TAPS - j, :]
        return _silu(y)

    def l2n(x):
        return x * lax.rsqrt(jnp.sum(x * x, axis=-1, keepdims=True) + L2_EPS)

    def stage_gates(ci, hh):
        i, rs, ls = where(ci, hh)
        cw = cw_ref[hh]
        sel = sel_ref[hh]
        hp = hp_ref[hh]
        ba = ba_ref[0, rs, :]
        qc = l2n(conv_silu(q_ref, qh_ref, ci, rs, ls, cw[0:CONV_TAPS])) * (LANES ** -0.5)
        kc = l2n(conv_silu(k_ref, kh_ref, ci, rs, ls, cw[CONV_TAPS:2 * CONV_TAPS]))
        vc = conv_silu(v_ref, vh_ref, ci, rs, ls, cw[2 * CONV_TAPS:3 * CONV_TAPS])
        beta = jax.nn.sigmoid(jnp.sum(ba * sel[0:1], axis=-1, keepdims=True))
        alpha_pre = jnp.sum(ba * sel[1:2], axis=-1, keepdims=True)
        bc = jnp.broadcast_to(beta, (c, LANES))
        lac = -jnp.exp(hp[0:1]) * _softplus(jnp.broadcast_to(alpha_pre, (c, LANES)) + hp[1:2])
        f128[i, qf["q"]] = qc
        f128[i, qf["k"]] = kc
        f128[i, qf["vb"]] = vc * bc
        f128[i, qf["bc"]] = bc
        f128[i, qf["g"]] = _dot01(tril, lac)
        f64[i, qp["dmat"]] = _dot01(tril, lac[:, 0:c] * strict_f)

    def stage_intra(ci, hh):
        i, rs, ls = where(ci, hh)
        qc = f128[i, qf["q"]]
        kc = f128[i, qf["k"]]
        bc = f128[i, qf["bc"]]
        g = f128[i, qf["g"]]
        decay = jnp.where(causal, jnp.exp(jnp.where(causal, f64[i, qp["dmat"]], 0.0)), 0.0)
        kb = kc.astype(BF16)
        qkk = _dot_nt(jnp.concatenate([qc.astype(BF16), kb], axis=0), kb)
        lower = jnp.where(strict, bc[:, 0:c] * qkk[c:] * decay, 0.0)
        gamma = jnp.exp(g)
        glast = g[c - 1:c, :]
        akd_s[i, 0:c] = jnp.where(causal, qkk[0:c] * decay, 0.0).astype(BF16)
        nc_s[i] = (-lower).astype(BF16)
        f64[i, qp["p"]] = eye - lower
        rhs_s[i] = jnp.concatenate([f128[i, qf["vb"]], kc * (bc * gamma)], axis=1).astype(BF16)
        wq_s[i, c:] = (qc * gamma).astype(BF16)
        akd_s[i, c:] = (kc * jnp.exp(glast - g)).T.astype(BF16)
        cd_s[i] = jnp.broadcast_to(jnp.exp(glast), (SUBLANES, LANES))

    def stage_square(ci, hh):
        i, _, _ = where(ci, hh)
        nc = nc_s[i]
        nc_s[i] = _dot(nc, nc).astype(BF16)

    def stage_accumulate(ci, hh):
        i, _, _ = where(ci, hh)
        p = f64[i, qp["p"]]
        f64[i, qp["p"]] = p + _dot(p.astype(BF16), nc_s[i])

    def stage_solve(ci, hh):
        i, _, _ = where(ci, hh)
        sol = _dot(f64[i, qp["p"]].astype(BF16), rhs_s[i])
        f128[i, qf["u0"]] = sol[:, 0:LANES]
        wq_s[i, 0:c] = sol[:, LANES:].astype(BF16)

    def stage_state_a(ci, hh):
        i, _, _ = where(ci, hh)
        wq = _dot(wq_s[i], sb_s[hh])
        ub_s[i] = (f128[i, qf["u0"]] - wq[0:c]).astype(BF16)
        f128[i, qf["o"]] = wq[c:]

    def stage_state_b(ci, hh):
        i, _, _ = where(ci, hh)
        ou = _dot(akd_s[i], ub_s[i])
        f128[i, qf["o"]] = f128[i, qf["o"]] + ou[0:c]
        s_new = cd_s[i][0:1] * s_ref[hh] + ou[c:]
        s_ref[hh] = s_new
        sb_s[hh] = s_new.astype(BF16)

    def stage_out(ci, hh):
        i, rs, ls = where(ci, hh)
        o = f128[i, qf["o"]]
        on = o * lax.rsqrt(jnp.mean(o * o, axis=-1, keepdims=True) + NORM_EPS) * onorm
        o_ref[0, rs, ls] = (on * _silu(z_ref[0, rs, ls])).astype(o_ref.dtype)

    stages = ([stage_gates, stage_intra] + [stage_square, stage_accumulate] * 5
              + [stage_solve, stage_state_a, stage_state_b, stage_out])
    _emit_skewed(stages, tc // c, heads, DN_STAGE_SKEW)


def _deltanet(p, ba, conv_w, a_log, dt_bias, out_norm, nheads):
    b, t, _ = p.shape
    tc = _tile(t, 256)
    c = DN_CHUNK
    assert tc % c == 0
    hb = tc // SUBLANES
    nh = nheads
    hg = _tile(nh, 8)
    ng = nh // hg
    wd = hg * LANES
    nbody = (tc // c) * hg
    cw = conv_w.reshape(CONV_TAPS, 3, nh, LANES).transpose(2, 1, 0, 3).reshape(nh, 3 * CONV_TAPS, LANES)
    lane = jnp.arange(LANES)[None, None, :]
    head = jnp.arange(nh)[:, None, None]
    sel = jnp.concatenate([(lane == head), (lane == head + nh)], axis=1).astype(F32)
    hp = jnp.broadcast_to(jnp.stack([a_log, dt_bias], axis=1)[:, :, None], (nh, 2, LANES)).astype(F32)

    def main(sec):
        return pl.BlockSpec((1, tc, wd), lambda i, h, j: (i, j, sec * ng + h))

    def halo(sec):
        return pl.BlockSpec((1, SUBLANES, wd), lambda i, h, j: (i, jnp.maximum(j * hb - 1, 0), sec * ng + h))

    per_head = lambda rows: pl.BlockSpec((hg, rows, LANES), lambda i, h, j: (h, 0, 0))
    return pl.pallas_call(
        _dn_kernel,
        grid=(b, ng, t // tc),
        in_specs=[main(0), main(1), main(2), halo(0), halo(1), halo(2), main(3),
                  pl.BlockSpec((1, tc, LANES), lambda i, h, j: (i, j, 0)),
                  per_head(2), per_head(2), per_head(3 * CONV_TAPS),
                  pl.BlockSpec((1, LANES), lambda i, h, j: (0, 0))],
        out_specs=pl.BlockSpec((1, tc, wd), lambda i, h, j: (i, j, h)),
        out_shape=jax.ShapeDtypeStruct((b, t, nh * LANES), BF16),
        scratch_shapes=[pltpu.VMEM((hg, LANES, LANES), F32),
                        pltpu.VMEM((hg, LANES, LANES), BF16),
                        pltpu.VMEM((nbody, len(_DN_F128), c, LANES), F32),
                        pltpu.VMEM((nbody, 2 * c, LANES), BF16),
                        pltpu.VMEM((nbody, c, LANES), BF16),
                        pltpu.VMEM((nbody, c, c), BF16),
                        pltpu.VMEM((nbody, len(_DN_F64), c, c), F32),
                        pltpu.VMEM((nbody, c, 2 * LANES), BF16),
                        pltpu.VMEM((nbody, c + LANES, c), BF16),
                        pltpu.VMEM((nbody, SUBLANES, LANES), F32)],
        compiler_params=_params("parallel", "parallel", "arbitrary"),
        name="deltanet",
    )(p, p, p, p, p, p, p, ba, sel, hp, cw, out_norm.reshape(1, LANES))


HG_LEVELS = 6
HG_BLOCK = 64


def _hg_tables():
    import numpy as np
    c = HG_BLOCK
    t = np.arange(c)[:, None]
    i = np.arange(c)[None, :]
    rows = [i <= t, i > t]
    q_rows, k_rows = [], []
    level = np.full((c, c), -1, np.int32)
    level[np.arange(c), np.arange(c)] = HG_LEVELS
    for li in range(HG_LEVELS):
        m = c >> (li + 1)
        pos = t % (2 * m)
        upper = pos >= m
        ref = t - pos + m - 1
        q_rows.append(upper & (i > ref) & (i <= t))
        k_rows.append((~upper) & (i > t) & (i <= ref))
        same = (t // (2 * m)) == (i // (2 * m))
        level[same & upper & ((i % (2 * m)) < m)] = li
    table = np.concatenate(rows + q_rows + k_rows, axis=0).astype(np.float32)
    return table, level


def _hg_kernel(q_ref, f_ref, i_ref, z_ref, lb_ref, on_ref, tab_ref, lvl_ref, o_ref,
               s_ref, sb_s, qt_s, kt_s, qb_s, vb_s, klt_s, att_s, ecol_s, inc_s, o_s, *, layer):
    t = pl.program_id(2)
    tc = q_ref.shape[1]
    heads = s_ref.shape[0]
    c = HG_BLOCK
    nl = HG_LEVELS

    @pl.when(t == 0)
    def _():
        s_ref[...] = jnp.zeros_like(s_ref)
        sb_s[...] = jnp.zeros_like(sb_s)

    table = tab_ref[...]
    level = lvl_ref[...]
    onorm = on_ref[...]
    def where(ci, hh):
        return ci * heads + hh, slice(ci * c, (ci + 1) * c), slice(hh * LANES, (hh + 1) * LANES)

    def stage_gates(ci, hh):
        i, rs, ls = where(ci, hh)
        logits = lb_ref[:, ls]
        e = jnp.exp(logits - jnp.max(logits, axis=0, keepdims=True))
        share = e / jnp.sum(e, axis=0, keepdims=True)
        lbound = jnp.sum(share[0:layer + 1], axis=0, keepdims=True) - share[0:1]
        q = _silu(q_ref[0, rs, ls]) * (LANES ** -0.5)
        forget = lbound + (1.0 - lbound) * jax.nn.sigmoid(f_ref[0, rs, ls])
        k = 1.0 - forget
        d = _dot(table, jnp.log(forget).astype(BF16))
        b = d[0:c]
        qb_s[i] = (q * jnp.exp(b)).astype(BF16)
        klt_s[i] = (k * jnp.exp(d[c:2 * c])).T.astype(BF16)
        ecol_s[i] = jnp.broadcast_to(jnp.exp(b[c - 1:c]), (LANES, LANES)).T
        for li in range(nl):
            qt_s[i, li] = (q * jnp.exp(d[(2 + li) * c:(3 + li) * c])).astype(BF16)
            kt_s[i, li] = (k * jnp.exp(d[(2 + nl + li) * c:(3 + nl + li) * c])).astype(BF16)
        qt_s[i, nl] = q.astype(BF16)
        kt_s[i, nl] = k.astype(BF16)
        vb_s[i] = i_ref[0, rs, ls].astype(BF16)

    def stage_intra(ci, hh):
        i, _, _ = where(ci, hh)
        att = jnp.zeros((c, c), F32)
        for li in range(nl + 1):
            att = att + jnp.where(level == li, _dot_nt(qt_s[i, li], kt_s[i, li]), 0.0)
        att_s[i] = att.astype(BF16)
        inc_s[i] = _dot(klt_s[i], vb_s[i])

    def stage_att_v(ci, hh):
        i, _, _ = where(ci, hh)
        o_s[i] = _dot(att_s[i], vb_s[i])

    def stage_state(ci, hh):
        i, _, _ = where(ci, hh)
        o_s[i] = o_s[i] + _dot(qb_s[i], sb_s[hh])
        s_new = ecol_s[i] * s_ref[hh] + inc_s[i]
        s_ref[hh] = s_new
        sb_s[hh] = s_new.astype(BF16)

    def stage_out(ci, hh):
        i, rs, ls = where(ci, hh)
        o = o_s[i]
        on = o * lax.rsqrt(jnp.mean(o * o, axis=-1, keepdims=True) + NORM_EPS) * onorm
        o_ref[0, rs, ls] = (on * _silu(z_ref[0, rs, ls])).astype(o_ref.dtype)

    _emit_skewed([stage_gates, stage_intra, stage_att_v, stage_state, stage_out], tc // c, heads, HG_STAGE_SKEW)


def _hgrn2(p, col0, lb_logits, out_norm, nheads, layer):
    b, t, _ = p.shape
    tc = _tile(t, 256)
    c = HG_BLOCK
    assert tc % c == 0
    nh = nheads
    hg = _tile(nh, 8)
    ng = nh // hg
    wd = hg * LANES
    assert col0 % hg == 0
    cg = col0 // hg
    nbody = (tc // c) * hg
    table, level = _hg_tables()
    nrow = table.shape[0]

    def main(sec):
        return pl.BlockSpec((1, tc, wd), lambda i, h, j: (i, j, cg + sec * ng + h))

    nl = lb_logits.shape[0]
    const = lambda shape: pl.BlockSpec(shape, lambda i, h, j: (0, 0))
    return pl.pallas_call(
        functools.partial(_hg_kernel, layer=layer),
        grid=(b, ng, t // tc),
        in_specs=[main(0), main(1), main(2), main(3),
                  pl.BlockSpec((nl, wd), lambda i, h, j: (0, h)),
                  const((1, LANES)), const((nrow, c)), const((c, c))],
        out_specs=pl.BlockSpec((1, tc, wd), lambda i, h, j: (i, j, h)),
        out_shape=jax.ShapeDtypeStruct((b, t, nh * LANES), BF16),
        scratch_shapes=[pltpu.VMEM((hg, LANES, LANES), F32),
                        pltpu.VMEM((hg, LANES, LANES), BF16),
                        pltpu.VMEM((nbody, HG_LEVELS + 1, c, LANES), BF16),
                        pltpu.VMEM((nbody, HG_LEVELS + 1, c, LANES), BF16),
                        pltpu.VMEM((nbody, c, LANES), BF16),
                        pltpu.VMEM((nbody, c, LANES), BF16),
                        pltpu.VMEM((nbody, LANES, c), BF16),
                        pltpu.VMEM((nbody, c, c), BF16),
                        pltpu.VMEM((nbody, LANES, LANES), F32),
                        pltpu.VMEM((nbody, LANES, LANES), F32),
                        pltpu.VMEM((nbody, c, LANES), F32)],
        compiler_params=_params("parallel", "parallel", "arbitrary"),
        name="hgrn2",
    )(p, p, p, p, lb_logits, out_norm.reshape(1, LANES), jnp.asarray(table, BF16), jnp.asarray(level))


_RW_B64 = dict(aak=0, vb=1, arb=2, ark=3, sab=4)
_RW_B128 = dict(nb=0, vbd=1, bkt=2, atbd=3, avbd=4)
_RW_F64 = dict(p=0, uc=1, g=2, bv=3, lw=4, kkr=5, k2=6, ag=7, v=8, ssq=9, y=10, d=11)


def _rw_kernel(*refs, has_vres):
    if has_vres:
        (r_ref, k_ref, v_ref, z_ref, wl_ref, al_ref, vl_ref, vf_ref, prm_ref,
         o_ref, h_ref, hb_s, s64, s128, wr_s, np_s, f64, f128) = refs
    else:
        (r_ref, k_ref, v_ref, z_ref, wl_ref, al_ref, prm_ref,
         o_ref, h_ref, hb_s, s64, s128, wr_s, np_s, f64, f128) = refs
    t = pl.program_id(2)
    tc = o_ref.shape[1]
    pairs = h_ref.shape[0]
    c = RW_CHUNK
    nchunk = tc // c
    b64, b128, q64 = _RW_B64, _RW_B128, _RW_F64

    @pl.when(t == 0)
    def _():
        h_ref[...] = jnp.zeros_like(h_ref)
        hb_s[...] = jnp.zeros_like(hb_s)

    lane1 = _iota2((1, LANES), 1)
    head0 = lane1 < RW_HEAD
    m0 = head0.astype(F32)
    m1 = 1.0 - m0
    same_head = (_iota2((LANES, LANES), 0) < RW_HEAD) == (_iota2((LANES, LANES), 1) < RW_HEAD)
    bdmask = same_head.astype(F32)
    row = _iota2((c, LANES), 0)
    scol = jnp.bitwise_and(_iota2((c, LANES), 1), RW_HEAD - 1)
    incl = row >= scol
    strict = row > scol
    eye_cat = (row == scol).astype(F32)
    tril = (_iota2((c, c), 0) >= _iota2((c, c), 1)).astype(BF16)

    def bd(x):
        return jnp.concatenate([x * m0, x * m1], axis=0).astype(BF16)

    def gsum(x):
        s0 = jnp.sum(jnp.where(head0, x, 0.0), axis=-1, keepdims=True)
        s1 = jnp.sum(x, axis=-1, keepdims=True) - s0
        return jnp.where(head0, s0, s1)

    def where(ci, pp):
        return ci * pairs + pp, slice(ci * c, (ci + 1) * c), slice(pp * LANES, (pp + 1) * LANES)

    def stage_gates(ci, pp):
        i, rs, ls = where(ci, pp)
        prm = prm_ref[:, ls]
        w0, a0, k_k, k_a, r_k, v0 = prm[0:1], prm[1:2], prm[2:3], prm[3:4], prm[4:5], prm[7:8]
        r = r_ref[0, rs, ls]
        k = k_ref[0, rs, ls]
        v = v_ref[0, rs, ls]
        lw = -EXP_NEG_HALF * jax.nn.sigmoid(w0 + wl_ref[0, rs, ls])
        ag = jax.nn.sigmoid(a0 + al_ref[0, rs, ls])
        if has_vres:
            v = v + (vf_ref[0, rs, ls] - v) * jax.nn.sigmoid(v0 + vl_ref[0, rs, ls])
        kkr = k * k_k
        k2 = k * (1.0 + (ag - 1.0) * k_a)
        sums = gsum(jnp.concatenate([kkr * kkr, r * k2 * r_k], axis=0))
        f64[i, q64["g"]] = _dot01(tril, lw)
        f64[i, q64["lw"]] = lw
        f64[i, q64["ag"]] = ag
        f64[i, q64["v"]] = v
        f64[i, q64["kkr"]] = kkr
        f64[i, q64["k2"]] = k2
        f64[i, q64["ssq"]] = sums[0:c]
        f64[i, q64["bv"]] = sums[c:] * v

    def stage_intra(ci, pp):
        i, rs, ls = where(ci, pp)
        g = f64[i, q64["g"]]
        lw = f64[i, q64["lw"]]
        v = f64[i, q64["v"]]
        kk = f64[i, q64["kkr"]] * lax.rsqrt(f64[i, q64["ssq"]] + L2_EPS)
        e_g = jnp.exp(g)
        e_gn = jnp.exp(-g)
        rt = r_ref[0, rs, ls] * e_g
        at = -kk * jnp.exp(g - lw)
        kt = f64[i, q64["k2"]] * e_gn
        bt = kk * f64[i, q64["ag"]] * e_gn
        x = _bdot_nt(jnp.concatenate([at, rt], axis=0), jnp.concatenate([bd(kt), bd(bt)], axis=0))
        a_ab = jnp.where(strict, x[0:c, LANES:], 0.0)
        e_last = e_g[c - 1:c, :]
        s64[i, b64["aak"]] = jnp.where(strict, x[0:c, 0:LANES], 0.0).astype(BF16)
        s64[i, b64["ark"]] = jnp.where(incl, x[c:, 0:LANES], 0.0).astype(BF16)
        s64[i, b64["arb"]] = jnp.where(incl, x[c:, LANES:], 0.0).astype(BF16)
        np_s[i, 0:c] = a_ab.astype(BF16)
        np_s[i, c:] = (eye_cat + a_ab).astype(BF16)
        s128[i, b128["nb"]] = bd(a_ab)
        f64[i, q64["p"]] = eye_cat + a_ab
        wr_s[i, c:] = rt.astype(BF16)
        s64[i, b64["vb"]] = v.astype(BF16)
        s128[i, b128["atbd"]] = bd(at)
        s128[i, b128["vbd"]] = bd(v)
        s128[i, b128["bkt"]] = jnp.concatenate([bt * e_last, kt * e_last], axis=0).T.astype(BF16)
        f128[i] = jnp.broadcast_to(e_last, (LANES, LANES)).T

    def stage_av(ci, pp):
        i, _, _ = where(ci, pp)
        s128[i, b128["avbd"]] = bd(_dot(s64[i, b64["aak"]], s128[i, b128["vbd"]]))

    def stage_level(lvl):
        def stage(ci, pp):
            i, _, _ = where(ci, pp)
            lhs = np_s[i, 0:c] if lvl == 0 else (np_s[i, c:] if lvl == 5 else np_s[i])
            out = _dot(lhs, s128[i, b128["nb"]])
            if lvl > 0:
                p = f64[i, q64["p"]] + out[-c:]
                f64[i, q64["p"]] = p
                np_s[i, c:] = p.astype(BF16)
            if lvl < 5:
                n2 = out[0:c]
                np_s[i, 0:c] = n2.astype(BF16)
                s128[i, b128["nb"]] = bd(n2)
        return stage

    def stage_wu(ci, pp):
        i, _, _ = where(ci, pp)
        rhs = jnp.concatenate([s128[i, b128["atbd"]], s128[i, b128["avbd"]]], axis=1)
        wu = _dot(f64[i, q64["p"]].astype(BF16), rhs)
        wr_s[i, 0:c] = wu[:, 0:LANES].astype(BF16)
        f64[i, q64["uc"]] = wu[:, LANES:]

    def stage_state_a(ci, pp):
        i, _, _ = where(ci, pp)
        wr = _dot(wr_s[i], hb_s[pp])
        sa = wr[0:c] + f64[i, q64["uc"]]
        s64[i, b64["sab"]] = sa.astype(BF16)
        s128[i, b128["nb"]] = bd(sa)
        f64[i, q64["y"]] = wr[c:]

    def stage_state_b(ci, pp):
        i, _, _ = where(ci, pp)
        f64[i, q64["y"]] = f64[i, q64["y"]] + _dot(
            jnp.concatenate([s64[i, b64["arb"]], s64[i, b64["ark"]]], axis=1),
            jnp.concatenate([s128[i, b128["nb"]], s128[i, b128["vbd"]]], axis=0))
        h_new = f128[i] * h_ref[pp] + bdmask * _dot(
            s128[i, b128["bkt"]], jnp.concatenate([s64[i, b64["sab"]], s64[i, b64["vb"]]], axis=0))
        h_ref[pp] = h_new
        hb_s[pp] = h_new.astype(BF16)

    def stage_center(ci, pp):
        i, _, _ = where(ci, pp)
        y = f64[i, q64["y"]]
        f64[i, q64["d"]] = y - gsum(y) * (1.0 / RW_HEAD)

    def stage_out(ci, pp):
        i, rs, ls = where(ci, pp)
        prm = prm_ref[:, ls]
        d = f64[i, q64["d"]]
        var = gsum(d * d) * (1.0 / RW_HEAD)
        yn = d * lax.rsqrt(var + GN_EPS) * prm[5:6] + prm[6:7]
        o_ref[0, rs, ls] = ((yn + f64[i, q64["bv"]]) * _silu(z_ref[0, rs, ls])).astype(o_ref.dtype)

    stages = ([stage_gates, stage_intra, stage_av] + [stage_level(lvl) for lvl in range(6)]
              + [stage_wu, stage_state_a, stage_state_b, stage_center, stage_out])
    _emit_skewed(stages, nchunk, pairs, RW_STAGE_SKEW)


def _rwkv7(rkvz, wl, al, vl, v_first, prm):
    b, t, d4 = rkvz.shape
    d = d4 // 4
    tc = _tile(t, 256)
    has_vres = vl is not None
    pg = _tile(d // LANES, 8)
    wd = pg * LANES
    c = RW_CHUNK
    nbody = (tc // c) * pg
    ngrp = d // wd

    def sec(s):
        return pl.BlockSpec((1, tc, wd), lambda i, p, j: (i, j, s * ngrp + p))

    flat = pl.BlockSpec((1, tc, wd), lambda i, p, j: (i, j, p))
    in_specs = [sec(0), sec(1), sec(2), sec(3), flat, flat]
    args = [rkvz, rkvz, rkvz, rkvz, wl, al]
    if has_vres:
        in_specs += [flat, sec(2)]
        args += [vl, v_first]
    in_specs.append(pl.BlockSpec((8, wd), lambda i, p, j: (0, p)))
    args.append(prm)
    return pl.pallas_call(
        functools.partial(_rw_kernel, has_vres=has_vres),
        grid=(b, d // wd, t // tc),
        in_specs=in_specs,
        out_specs=flat,
        out_shape=jax.ShapeDtypeStruct((b, t, d), BF16),
        scratch_shapes=[pltpu.VMEM((pg, LANES, LANES), F32),
                        pltpu.VMEM((pg, LANES, LANES), BF16),
                        pltpu.VMEM((nbody, len(_RW_B64), c, LANES), BF16),
                        pltpu.VMEM((nbody, len(_RW_B128), 2 * c, LANES), BF16),
                        pltpu.VMEM((nbody, 2 * c, LANES), BF16),
                        pltpu.VMEM((nbody, 2 * c, LANES), BF16),
                        pltpu.VMEM((nbody, len(_RW_F64), c, LANES), F32),
                        pltpu.VMEM((nbody, LANES, LANES), F32)],
        compiler_params=_params("parallel", "parallel", "arbitrary"),
        name="rwkv7",
    )(*args)


def kernel(x, norm_gains, mix_w_in, dn_conv, dn_a_log, dn_dt_bias, dn_out_norm, hg_lb_logits, hg_out_norm,
           mix_w_out, rw_mu, rw_w_rkvz, rw_w0, rw_w1, rw_w2, rw_a0, rw_a1, rw_a2, rw_v0, rw_v1, rw_v2,
           rw_k_k, rw_k_a, rw_r_k, rw_ln_w, rw_ln_b, rw_w_out, final_norm):
    b, t, d = x.shape
    m = b * t
    depth = norm_gains.shape[0]
    dn_heads = dn_a_log.shape[1]
    dn_width = dn_conv.shape[2] // 3
    hg_width = hg_lb_logits.shape[1]
    hg_heads = hg_width // LANES
    assert dn_width == dn_heads * LANES and hg_out_norm.shape[1] == LANES and rw_r_k.shape[2] == RW_HEAD
    dn_ba = 4 * dn_width
    hg_q = dn_ba + 2 * dn_heads
    assert 2 * dn_heads <= LANES

    h = x.reshape(m, d)
    v_first = None
    for layer in range(depth):
        gain = norm_gains[layer]
        if layer % 2 == 0:
            e = layer // 2
            w_in = mix_w_in[e]
            w_main = jnp.concatenate([w_in[:, :dn_ba], w_in[:, hg_q:]], axis=1).astype(BF16)
            w_ba = jnp.pad(w_in[:, dn_ba:hg_q], ((0, 0), (0, LANES - 2 * dn_heads))).astype(BF16)
            p, ba = _norm_matmul(h, gain, w_main, w_ba)
            p = p.reshape(b, t, -1)
            ba = ba.reshape(b, t, LANES)
            o_a = _deltanet(p, ba, dn_conv[e], dn_a_log[e], dn_dt_bias[e], dn_out_norm[e], dn_heads)
            o_b = _hgrn2(p, dn_ba // LANES, hg_lb_logits, hg_out_norm[e], hg_heads, e)
            h = _out_matmul([o_a.reshape(m, -1), o_b.reshape(m, -1)], mix_w_out[e].astype(BF16), h,
                            final_norm if layer == depth - 1 else None)
        else:
            o = layer // 2
            has_vres = v_first is not None
            rows = [0, 2, 3, 5, 1, 4] + ([3] if has_vres else [])
            downs = [rw_w1[o], rw_a1[o]] + ([rw_v1[o - 1]] if has_vres else [])
            a_down = jnp.stack([jnp.pad(a, ((0, 0), (0, LANES - a.shape[1]))) for a in downs]).astype(BF16)
            rkvz, mids = _mix_matmul(h.reshape(b, t, d), gain, rw_mu[o][jnp.array(rows)],
                                     rw_w_rkvz[o].astype(BF16), a_down)
            rkvz = rkvz.reshape(b, t, 4 * d)
            wl = _lora_up(mids, 0, rw_w2[o], True).reshape(b, t, d)
            al = _lora_up(mids, 1, rw_a2[o], False).reshape(b, t, d)
            vl = None
            v0 = jnp.zeros((d,), F32)
            if has_vres:
                vl = _lora_up(mids, 2, rw_v2[o - 1], False).reshape(b, t, d)
                v0 = rw_v0[o - 1]
            else:
                v_first = rkvz
            prm = jnp.stack([rw_w0[o], rw_a0[o], rw_k_k[o], rw_k_a[o], rw_r_k[o].reshape(d),
                             rw_ln_w[o], rw_ln_b[o], v0]).astype(F32)
            y = _rwkv7(rkvz, wl, al, vl, v_first, prm)
            h = _out_matmul([y.reshape(m, d)], rw_w_out[o].astype(BF16), h,
                            final_norm if layer == depth - 1 else None)
    return h.reshape(b, t, d)
```

```python
import functools
import math

import jax
import jax.numpy as jnp
from jax import lax
from jax.experimental import pallas as pl
from jax.experimental.pallas import tpu as pltpu

F32 = jnp.float32
BF16 = jnp.bfloat16

NORM_EPS = 1e-6
GN_EPS = 64e-5
L2_EPS = 1e-6
CONV_TAPS = 4

LANES = 128
SUBLANES = 8
V7X_VMEM_LIMIT_BYTES = 56 * 1024 * 1024

DN_CHUNK = 64
RW_CHUNK = 64
RW_HEAD = 64
EXP_NEG_HALF = math.exp(-0.5)
MIX_PROLOGUE_ROWS = 256
MIX_DOT_ROWS = 512
DN_STAGE_SKEW = 2
RW_STAGE_SKEW = 3
HG_STAGE_SKEW = 1


def _params(*semantics):
    return pltpu.CompilerParams(dimension_semantics=semantics, vmem_limit_bytes=V7X_VMEM_LIMIT_BYTES)


def _tile(dim, pref):
    t = min(dim, pref)
    assert dim % t == 0, (dim, pref)
    return t


def _dot(a, b):
    return jnp.dot(a, b, preferred_element_type=F32)


def _dot_nt(a, b):
    return lax.dot_general(a, b, (((1,), (1,)), ((), ())), preferred_element_type=F32)


def _bdot_nt(a, b):
    return _dot_nt(a.astype(BF16), b.astype(BF16))


def _dot01(m01, x):
    x1 = x.astype(BF16)
    x2 = (x - x1.astype(F32)).astype(BF16)
    return _dot(m01, x1) + _dot(m01, x2)


def _silu(x):
    return x * jax.nn.sigmoid(x)


def _softplus(x):
    return jnp.maximum(x, 0.0) + jnp.log1p(jnp.exp(-jnp.abs(x)))


def _iota2(shape, dim):
    return lax.broadcasted_iota(jnp.int32, shape, dim)


def _shift_rows(x, halo, j):
    full = pltpu.roll(x, j, axis=0)
    hfix = pltpu.roll(halo, j, axis=0)
    top = jnp.where(_iota2(hfix.shape, 0) < j, hfix, full[0:SUBLANES])
    return jnp.concatenate([top, full[SUBLANES:]], axis=0)


def _emit_skewed(stages, nchunk, width, skew):
    assert skew >= 1
    n = len(stages)
    for tau in range(n + skew * (nchunk - 1)):
        for w in range(width):
            for ci in range(nchunk):
                st = tau - skew * ci
                if 0 <= st < n:
                    stages[st](ci, w)


def _rms(x, g):
    return x * lax.rsqrt(jnp.mean(x * x, axis=-1, keepdims=True) + NORM_EPS) * g


def _norm_mm_kernel(x_ref, g_ref, w_ref, wn_ref, o_ref, on_ref, hn_s):
    j = pl.program_id(1)

    @pl.when(j == 0)
    def _():
        hn_s[...] = _rms(x_ref[...], g_ref[...]).astype(BF16)
        on_ref[...] = _dot(hn_s[...], wn_ref[...])

    @pl.when(j > 0)
    def _():
        o_ref[...] = _dot(hn_s[...], w_ref[...])


def _norm_matmul(h, gain, w, w_narrow):
    m, d = h.shape
    n = w.shape[1]
    tm = _tile(m, 1024)
    tn = _tile(n, 1024)
    n_main = n // tn

    def col(j):
        return jnp.maximum(j - 1, 0)

    return pl.pallas_call(
        _norm_mm_kernel,
        grid=(m // tm, n_main + 1),
        in_specs=[pl.BlockSpec((tm, d), lambda i, j: (i, 0)),
                  pl.BlockSpec((1, d), lambda i, j: (0, 0)),
                  pl.BlockSpec((d, tn), lambda i, j: (0, col(j))),
                  pl.BlockSpec((d, LANES), lambda i, j: (0, 0))],
        out_specs=[pl.BlockSpec((tm, tn), lambda i, j: (i, col(j))),
                   pl.BlockSpec((tm, LANES), lambda i, j: (i, 0))],
        out_shape=[jax.ShapeDtypeStruct((m, n), F32), jax.ShapeDtypeStruct((m, LANES), F32)],
        scratch_shapes=[pltpu.VMEM((tm, d), BF16)],
        compiler_params=_params("parallel", "arbitrary"),
        name="norm_matmul",
    )(h, gain.reshape(1, d), w, w_narrow)


def _mix_mm_kernel(x_ref, halo_ref, g_ref, mu_ref, w_ref, a_ref, o_ref, mid_ref, hn_s, xx_s, *,
                   tiles_per_group, tiles_per_seq, n_groups):
    i = pl.program_id(0)
    j = pl.program_id(1)
    tm = hn_s.shape[0]
    rc = min(tm, MIX_PROLOGUE_ROWS)
    mc = min(tm, MIX_DOT_ROWS)

    def mix(rows, mu_row):
        return hn_s[rows, :] + xx_s[rows, :] * mu_row.astype(BF16)

    @pl.when(j == 0)
    def _():
        g = g_ref[...]
        first = lax.rem(i, tiles_per_seq) == 0
        for r in range(tm // rc):
            rows = slice(r * rc, (r + 1) * rc)
            hn = _rms(x_ref[0, rows, :], g)
            if r == 0:
                hp = jnp.where(first, 0.0, _rms(halo_ref[0], g))
            else:
                hp = _rms(x_ref[0, r * rc - SUBLANES:r * rc, :], g)
            hn_s[rows, :] = hn.astype(BF16)
            xx_s[rows, :] = (_shift_rows(hn, hp, 1) - hn).astype(BF16)

        for l in range(a_ref.shape[0]):
            mu_row = mu_ref[n_groups + l:n_groups + l + 1, :]
            for r in range(tm // mc):
                rows = slice(r * mc, (r + 1) * mc)
                mid_ref[rows, l * LANES:(l + 1) * LANES] = _dot(mix(rows, mu_row), a_ref[l])

    @pl.when(j > 0)
    def _():
        mu_row = mu_ref[pl.ds(lax.div(j - 1, tiles_per_group), 1), :]
        for r in range(tm // mc):
            rows = slice(r * mc, (r + 1) * mc)
            o_ref[rows, :] = _dot(mix(rows, mu_row), w_ref[0])


def _mix_matmul(h, gain, mu, w, a_down):
    b, t, d = h.shape
    m = b * t
    g, _, n = w.shape
    nl = a_down.shape[0]
    tm = _tile(t, 1024)
    tn = _tile(n, 1024)
    tpg = n // tn
    n_main = g * tpg
    tps = t // tm
    hb = tm // SUBLANES

    def col(j):
        return jnp.maximum(j - 1, 0)

    return pl.pallas_call(
        functools.partial(_mix_mm_kernel, tiles_per_group=tpg, tiles_per_seq=tps, n_groups=g),
        grid=(m // tm, n_main + 1),
        in_specs=[pl.BlockSpec((1, tm, d), lambda i, j: (i // tps, i % tps, 0)),
                  pl.BlockSpec((1, SUBLANES, d), lambda i, j: (i // tps, jnp.maximum((i % tps) * hb - 1, 0), 0)),
                  pl.BlockSpec((1, d), lambda i, j: (0, 0)),
                  pl.BlockSpec((g + nl, d), lambda i, j: (0, 0)),
                  pl.BlockSpec((1, d, tn), lambda i, j: (col(j) // tpg, 0, col(j) % tpg)),
                  pl.BlockSpec((nl, d, LANES), lambda i, j: (0, 0, 0))],
        out_specs=[pl.BlockSpec((tm, tn), lambda i, j: (i, col(j))),
                   pl.BlockSpec((tm, nl * LANES), lambda i, j: (i, 0))],
        out_shape=[jax.ShapeDtypeStruct((m, g * n), F32), jax.ShapeDtypeStruct((m, nl * LANES), F32)],
        scratch_shapes=[pltpu.VMEM((tm, d), BF16), pltpu.VMEM((tm, d), BF16)],
        compiler_params=_params("parallel", "arbitrary"),
        name="mix_matmul",
    )(h, h, gain.reshape(1, d), mu, w, a_down)


def _lora_up_kernel(mid_ref, b_ref, o_ref, *, use_tanh):
    mid = mid_ref[...]
    if use_tanh:
        mid = jnp.tanh(mid)
    o_ref[...] = _dot(mid.astype(BF16), b_ref[...])


def _lora_up(mids, sel, b_up, use_tanh):
    m = mids.shape[0]
    rank, n = b_up.shape
    b_p = jnp.pad(b_up, ((0, LANES - rank), (0, 0))).astype(BF16)
    tm = _tile(m, 1024)
    return pl.pallas_call(
        functools.partial(_lora_up_kernel, use_tanh=use_tanh),
        grid=(m // tm,),
        in_specs=[pl.BlockSpec((tm, LANES), lambda i: (i, sel)),
                  pl.BlockSpec((LANES, n), lambda i: (0, 0))],
        out_specs=pl.BlockSpec((tm, n), lambda i: (i, 0)),
        out_shape=jax.ShapeDtypeStruct((m, n), F32),
        compiler_params=_params("parallel"),
        name="lora_up",
    )(mids, b_p)


def _out_mm_kernel(*refs, n_x, final_norm):
    x_refs, w_ref, r_ref = refs[:n_x], refs[n_x], refs[n_x + 1]
    acc = r_ref[...]
    k0 = 0
    for x_ref in x_refs:
        kx = x_ref.shape[1]
        acc = acc + _dot(x_ref[...], w_ref[k0:k0 + kx, :])
        k0 += kx
    if final_norm:
        g_ref, o_ref = refs[n_x + 2], refs[n_x + 3]
        o_ref[...] = _rms(acc, g_ref[...])
    else:
        refs[n_x + 2][...] = acc


def _out_matmul(xs, w, residual, final_gain=None):
    m = xs[0].shape[0]
    k, n = w.shape
    tm = _tile(m, 512)
    tn = _tile(n, 2048)
    in_specs = [pl.BlockSpec((tm, x.shape[1]), lambda i, j: (i, 0)) for x in xs]
    in_specs += [pl.BlockSpec((k, tn), lambda i, j: (0, j)), pl.BlockSpec((tm, tn), lambda i, j: (i, j))]
    args = [*xs, w, residual]
    if final_gain is not None:
        assert tn == n
        in_specs.append(pl.BlockSpec((1, n), lambda i, j: (0, 0)))
        args.append(final_gain.reshape(1, n))
    return pl.pallas_call(
        functools.partial(_out_mm_kernel, n_x=len(xs), final_norm=final_gain is not None),
        grid=(m // tm, n // tn),
        in_specs=in_specs,
        out_specs=pl.BlockSpec((tm, tn), lambda i, j: (i, j)),
        out_shape=jax.ShapeDtypeStruct((m, n), F32),
        compiler_params=_params("parallel", "parallel"),
        name="out_matmul",
    )(*args)


_DN_F128 = dict(q=0, k=1, vb=2, bc=3, g=4, u0=5, o=6)
_DN_F64 = dict(p=0, dmat=1)


def _dn_kernel(q_ref, k_ref, v_ref, qh_ref, kh_ref, vh_ref, z_ref, ba_ref, sel_ref, hp_ref, cw_ref, on_ref,
               o_ref, s_ref, sb_s, f128, wq_s, ub_s, nc_s, f64, rhs_s, akd_s, cd_s):
    t = pl.program_id(2)
    tc = q_ref.shape[1]
    heads = s_ref.shape[0]
    c = DN_CHUNK
    qf, qp = _DN_F128, _DN_F64

    @pl.when(t == 0)
    def _():
        s_ref[...] = jnp.zeros_like(s_ref)
        sb_s[...] = jnp.zeros_like(sb_s)

    row = _iota2((c, c), 0)
    col = _iota2((c, c), 1)
    causal = row >= col
    strict = row > col
    strict_f = strict.astype(F32)
    eye = (row == col).astype(F32)
    tril = causal.astype(BF16)
    onorm = on_ref[...]

    def where(ci, hh):
        return ci * heads + hh, slice(ci * c, (ci + 1) * c), slice(hh * LANES, (hh + 1) * LANES)

    def conv_silu(x_ref, h_ref, ci, rs, ls, w):
        x = x_ref[0, rs, ls]
        if ci == 0:
            halo = jnp.where(t > 0, h_ref[0, :, ls], 0.0)
        else:
            halo = x_ref[0, ci * c - SUBLANES:ci * c, ls]
        y = x * w[CONV_TAPS - 1:CONV_TAPS, :]
        for j in range(1, CONV_TAPS):
            y = y + _shift_rows(x, halo, j) * w[CONV_TAPS - 1 - j:CONV_TAPS - j, :]
        return _silu(y)

    def l2n(x):
        return x * lax.rsqrt(jnp.sum(x * x, axis=-1, keepdims=True) + L2_EPS)

    def stage_gates(ci, hh):
        i, rs, ls = where(ci, hh)
        cw = cw_ref[hh]
        sel = sel_ref[hh]
        hp = hp_ref[hh]
        ba = ba_ref[0, rs, :]
        qc = l2n(conv_silu(q_ref, qh_ref, ci, rs, ls, cw[0:CONV_TAPS])) * (LANES ** -0.5)
        kc = l2n(conv_silu(k_ref, kh_ref, ci, rs, ls, cw[CONV_TAPS:2 * CONV_TAPS]))
        vc = conv_silu(v_ref, vh_ref, ci, rs, ls, cw[2 * CONV_TAPS:3 * CONV_TAPS])
        beta = jax.nn.sigmoid(jnp.sum(ba * sel[0:1], axis=-1, keepdims=True))
        alpha_pre = jnp.sum(ba * sel[1:2], axis=-1, keepdims=True)
        bc = jnp.broadcast_to(beta, (c, LANES))
        lac = -jnp.exp(hp[0:1]) * _softplus(jnp.broadcast_to(alpha_pre, (c, LANES)) + hp[1:2])
        f128[i, qf["q"]] = qc
        f128[i, qf["k"]] = kc
        f128[i, qf["vb"]] = vc * bc
        f128[i, qf["bc"]] = bc
        f128[i, qf["g"]] = _dot01(tril, lac)
        f64[i, qp["dmat"]] = _dot01(tril, lac[:, 0:c] * strict_f)

    def stage_intra(ci, hh):
        i, rs, ls = where(ci, hh)
        qc = f128[i, qf["q"]]
        kc = f128[i, qf["k"]]
        bc = f128[i, qf["bc"]]
        g = f128[i, qf["g"]]
        decay = jnp.where(causal, jnp.exp(jnp.where(causal, f64[i, qp["dmat"]], 0.0)), 0.0)
        kb = kc.astype(BF16)
        qkk = _dot_nt(jnp.concatenate([qc.astype(BF16), kb], axis=0), kb)
        lower = jnp.where(strict, bc[:, 0:c] * qkk[c:] * decay, 0.0)
        gamma = jnp.exp(g)
        glast = g[c - 1:c, :]
        akd_s[i, 0:c] = jnp.where(causal, qkk[0:c] * decay, 0.0).astype(BF16)
        nc_s[i] = (-lower).astype(BF16)
        f64[i, qp["p"]] = eye - lower
        rhs_s[i] = jnp.concatenate([f128[i, qf["vb"]], kc * (bc * gamma)], axis=1).astype(BF16)
        wq_s[i, c:] = (qc * gamma).astype(BF16)
        akd_s[i, c:] = (kc * jnp.exp(glast - g)).T.astype(BF16)
        cd_s[i] = jnp.broadcast_to(jnp.exp(glast), (SUBLANES, LANES))

    def stage_square(ci, hh):
        i, _, _ = where(ci, hh)
        nc = nc_s[i]
        nc_s[i] = _dot(nc, nc).astype(BF16)

    def stage_accumulate(ci, hh):
        i, _, _ = where(ci, hh)
        p = f64[i, qp["p"]]
        f64[i, qp["p"]] = p + _dot(p.astype(BF16), nc_s[i])

    def stage_solve(ci, hh):
        i, _, _ = where(ci, hh)
        sol = _dot(f64[i, qp["p"]].astype(BF16), rhs_s[i])
        f128[i, qf["u0"]] = sol[:, 0:LANES]
        wq_s[i, 0:c] = sol[:, LANES:].astype(BF16)

    def stage_state_a(ci, hh):
        i, _, _ = where(ci, hh)
        wq = _dot(wq_s[i], sb_s[hh])
        ub_s[i] = (f128[i, qf["u0"]] - wq[0:c]).astype(BF16)
        f128[i, qf["o"]] = wq[c:]

    def stage_state_b(ci, hh):
        i, _, _ = where(ci, hh)
        ou = _dot(akd_s[i], ub_s[i])
        f128[i, qf["o"]] = f128[i, qf["o"]] + ou[0:c]
        s_new = cd_s[i][0:1] * s_ref[hh] + ou[c:]
        s_ref[hh] = s_new
        sb_s[hh] = s_new.astype(BF16)

    def stage_out(ci, hh):
        i, rs, ls = where(ci, hh)
        o = f128[i, qf["o"]]
        on = o * lax.rsqrt(jnp.mean(o * o, axis=-1, keepdims=True) + NORM_EPS) * onorm
        o_ref[0, rs, ls] = (on * _silu(z_ref[0, rs, ls])).astype(o_ref.dtype)

    stages = ([stage_gates, stage_intra] + [stage_square, stage_accumulate] * 5
              + [stage_solve, stage_state_a, stage_state_b, stage_out])
    _emit_skewed(stages, tc // c, heads, DN_STAGE_SKEW)


def _deltanet(p, ba, conv_w, a_log, dt_bias, out_norm, nheads):
    b, t, _ = p.shape
    tc = _tile(t, 256)
    c = DN_CHUNK
    assert tc % c == 0
    hb = tc // SUBLANES
    nh = nheads
    hg = _tile(nh, 8)
    ng = nh // hg
    wd = hg * LANES
    nbody = (tc // c) * hg
    cw = conv_w.reshape(CONV_TAPS, 3, nh, LANES).transpose(2, 1, 0, 3).reshape(nh, 3 * CONV_TAPS, LANES)
    lane = jnp.arange(LANES)[None, None, :]
    head = jnp.arange(nh)[:, None, None]
    sel = jnp.concatenate([(lane == head), (lane == head + nh)], axis=1).astype(F32)
    hp = jnp.broadcast_to(jnp.stack([a_log, dt_bias], axis=1)[:, :, None], (nh, 2, LANES)).astype(F32)

    def main(sec):
        return pl.BlockSpec((1, tc, wd), lambda i, h, j: (i, j, sec * ng + h))

    def halo(sec):
        return pl.BlockSpec((1, SUBLANES, wd), lambda i, h, j: (i, jnp.maximum(j * hb - 1, 0), sec * ng + h))

    per_head = lambda rows: pl.BlockSpec((hg, rows, LANES), lambda i, h, j: (h, 0, 0))
    return pl.pallas_call(
        _dn_kernel,
        grid=(b, ng, t // tc),
        in_specs=[main(0), main(1), main(2), halo(0), halo(1), halo(2), main(3),
                  pl.BlockSpec((1, tc, LANES), lambda i, h, j: (i, j, 0)),
                  per_head(2), per_head(2), per_head(3 * CONV_TAPS),
                  pl.BlockSpec((1, LANES), lambda i, h, j: (0, 0))],
        out_specs=pl.BlockSpec((1, tc, wd), lambda i, h, j: (i, j, h)),
        out_shape=jax.ShapeDtypeStruct((b, t, nh * LANES), BF16),
        scratch_shapes=[pltpu.VMEM((hg, LANES, LANES), F32),
                        pltpu.VMEM((hg, LANES, LANES), BF16),
                        pltpu.VMEM((nbody, len(_DN_F128), c, LANES), F32),
                        pltpu.VMEM((nbody, 2 * c, LANES), BF16),
                        pltpu.VMEM((nbody, c, LANES), BF16),
                        pltpu.VMEM((nbody, c, c), BF16),
                        pltpu.VMEM((nbody, len(_DN_F64), c, c), F32),
                        pltpu.VMEM((nbody, c, 2 * LANES), BF16),
                        pltpu.VMEM((nbody, c + LANES, c), BF16),
                        pltpu.VMEM((nbody, SUBLANES, LANES), F32)],
        compiler_params=_params("parallel", "parallel", "arbitrary"),
        name="deltanet",
    )(p, p, p, p, p, p, p, ba, sel, hp, cw, out_norm.reshape(1, LANES))


HG_LEVELS = 6
HG_BLOCK = 64


def _hg_tables():
    import numpy as np
    c = HG_BLOCK
    t = np.arange(c)[:, None]
    i = np.arange(c)[None, :]
    rows = [i <= t, i > t]
    q_rows, k_rows = [], []
    level = np.full((c, c), -1, np.int32)
    level[np.arange(c), np.arange(c)] = HG_LEVELS
    for li in range(HG_LEVELS):
        m = c >> (li + 1)
        pos = t % (2 * m)
        upper = pos >= m
        ref = t - pos + m - 1
        q_rows.append(upper & (i > ref) & (i <= t))
        k_rows.append((~upper) & (i > t) & (i <= ref))
        same = (t // (2 * m)) == (i // (2 * m))
        level[same & upper & ((i % (2 * m)) < m)] = li
    table = np.concatenate(rows + q_rows + k_rows, axis=0).astype(np.float32)
    return table, level


def _hg_kernel(q_ref, f_ref, i_ref, z_ref, lb_ref, on_ref, tab_ref, lvl_ref, o_ref,
               s_ref, sb_s, qt_s, kt_s, qb_s, vb_s, klt_s, att_s, ecol_s, inc_s, o_s, *, layer):
    t = pl.program_id(2)
    tc = q_ref.shape[1]
    heads = s_ref.shape[0]
    c = HG_BLOCK
    nl = HG_LEVELS

    @pl.when(t == 0)
    def _():
        s_ref[...] = jnp.zeros_like(s_ref)
        sb_s[...] = jnp.zeros_like(sb_s)

    table = tab_ref[...]
    level = lvl_ref[...]
    onorm = on_ref[...]
    def where(ci, hh):
        return ci * heads + hh, slice(ci * c, (ci + 1) * c), slice(hh * LANES, (hh + 1) * LANES)

    def stage_gates(ci, hh):
        i, rs, ls = where(ci, hh)
        logits = lb_ref[:, ls]
        e = jnp.exp(logits - jnp.max(logits, axis=0, keepdims=True))
        share = e / jnp.sum(e, axis=0, keepdims=True)
        lbound = jnp.sum(share[0:layer + 1], axis=0, keepdims=True) - share[0:1]
        q = _silu(q_ref[0, rs, ls]) * (LANES ** -0.5)
        forget = lbound + (1.0 - lbound) * jax.nn.sigmoid(f_ref[0, rs, ls])
        k = 1.0 - forget
        d = _dot(table, jnp.log(forget).astype(BF16))
        b = d[0:c]
        qb_s[i] = (q * jnp.exp(b)).astype(BF16)
        klt_s[i] = (k * jnp.exp(d[c:2 * c])).T.astype(BF16)
        ecol_s[i] = jnp.broadcast_to(jnp.exp(b[c - 1:c]), (LANES, LANES)).T
        for li in range(nl):
            qt_s[i, li] = (q * jnp.exp(d[(2 + li) * c:(3 + li) * c])).astype(BF16)
            kt_s[i, li] = (k * jnp.exp(d[(2 + nl + li) * c:(3 + nl + li) * c])).astype(BF16)
        qt_s[i, nl] = q.astype(BF16)
        kt_s[i, nl] = k.astype(BF16)
        vb_s[i] = i_ref[0, rs, ls].astype(BF16)

    def stage_intra(ci, hh):
        i, _, _ = where(ci, hh)
        att = jnp.zeros((c, c), F32)
        for li in range(nl + 1):
            att = att + jnp.where(level == li, _dot_nt(qt_s[i, li], kt_s[i, li]), 0.0)
        att_s[i] = att.astype(BF16)
        inc_s[i] = _dot(klt_s[i], vb_s[i])

    def stage_att_v(ci, hh):
        i, _, _ = where(ci, hh)
        o_s[i] = _dot(att_s[i], vb_s[i])

    def stage_state(ci, hh):
        i, _, _ = where(ci, hh)
        o_s[i] = o_s[i] + _dot(qb_s[i], sb_s[hh])
        s_new = ecol_s[i] * s_ref[hh] + inc_s[i]
        s_ref[hh] = s_new
        sb_s[hh] = s_new.astype(BF16)

    def stage_out(ci, hh):
        i, rs, ls = where(ci, hh)
        o = o_s[i]
        on = o * lax.rsqrt(jnp.mean(o * o, axis=-1, keepdims=True) + NORM_EPS) * onorm
        o_ref[0, rs, ls] = (on * _silu(z_ref[0, rs, ls])).astype(o_ref.dtype)

    _emit_skewed([stage_gates, stage_intra, stage_att_v, stage_state, stage_out], tc // c, heads, HG_STAGE_SKEW)


def _hgrn2(p, col0, lb_logits, out_norm, nheads, layer):
    b, t, _ = p.shape
    tc = _tile(t, 256)
    c = HG_BLOCK
    assert tc % c == 0
    nh = nheads
    hg = _tile(nh, 8)
    ng = nh // hg
    wd = hg * LANES
    assert col0 % hg == 0
    cg = col0 // hg
    nbody = (tc // c) * hg
    table, level = _hg_tables()
    nrow = table.shape[0]

    def main(sec):
        return pl.BlockSpec((1, tc, wd), lambda i, h, j: (i, j, cg + sec * ng + h))

    nl = lb_logits.shape[0]
    const = lambda shape: pl.BlockSpec(shape, lambda i, h, j: (0, 0))
    return pl.pallas_call(
        functools.partial(_hg_kernel, layer=layer),
        grid=(b, ng, t // tc),
        in_specs=[main(0), main(1), main(2), main(3),
                  pl.BlockSpec((nl, wd), lambda i, h, j: (0, h)),
                  const((1, LANES)), const((nrow, c)), const((c, c))],
        out_specs=pl.BlockSpec((1, tc, wd), lambda i, h, j: (i, j, h)),
        out_shape=jax.ShapeDtypeStruct((b, t, nh * LANES), BF16),
        scratch_shapes=[pltpu.VMEM((hg, LANES, LANES), F32),
                        pltpu.VMEM((hg, LANES, LANES), BF16),
                        pltpu.VMEM((nbody, HG_LEVELS + 1, c, LANES), BF16),
                        pltpu.VMEM((nbody, HG_LEVELS + 1, c, LANES), BF16),
                        pltpu.VMEM((nbody, c, LANES), BF16),
                        pltpu.VMEM((nbody, c, LANES), BF16),
                        pltpu.VMEM((nbody, LANES, c), BF16),
                        pltpu.VMEM((nbody, c, c), BF16),
                        pltpu.VMEM((nbody, LANES, LANES), F32),
                        pltpu.VMEM((nbody, LANES, LANES), F32),
                        pltpu.VMEM((nbody, c, LANES), F32)],
        compiler_params=_params("parallel", "parallel", "arbitrary"),
        name="hgrn2",
    )(p, p, p, p, lb_logits, out_norm.reshape(1, LANES), jnp.asarray(table, BF16), jnp.asarray(level))


_RW_B64 = dict(aak=0, vb=1, arb=2, ark=3, sab=4)
_RW_B128 = dict(nb=0, vbd=1, bkt=2, atbd=3, avbd=4)
_RW_F64 = dict(p=0, uc=1, g=2, bv=3, lw=4, kkr=5, k2=6, ag=7, v=8, ssq=9, y=10, d=11)


def _rw_kernel(*refs, has_vres):
    if has_vres:
        (r_ref, k_ref, v_ref, z_ref, wl_ref, al_ref, vl_ref, vf_ref, prm_ref,
         o_ref, h_ref, hb_s, s64, s128, wr_s, np_s, f64, f128) = refs
    else:
        (r_ref, k_ref, v_ref, z_ref, wl_ref, al_ref, prm_ref,
         o_ref, h_ref, hb_s, s64, s128, wr_s, np_s, f64, f128) = refs
    t = pl.program_id(2)
    tc = o_ref.shape[1]
    pairs = h_ref.shape[0]
    c = RW_CHUNK
    nchunk = tc // c
    b64, b128, q64 = _RW_B64, _RW_B128, _RW_F64

    @pl.when(t == 0)
    def _():
        h_ref[...] = jnp.zeros_like(h_ref)
        hb_s[...] = jnp.zeros_like(hb_s)

    lane1 = _iota2((1, LANES), 1)
    head0 = lane1 < RW_HEAD
    m0 = head0.astype(F32)
    m1 = 1.0 - m0
    same_head = (_iota2((LANES, LANES), 0) < RW_HEAD) == (_iota2((LANES, LANES), 1) < RW_HEAD)
    bdmask = same_head.astype(F32)
    row = _iota2((c, LANES), 0)
    scol = jnp.bitwise_and(_iota2((c, LANES), 1), RW_HEAD - 1)
    incl = row >= scol
    strict = row > scol
    eye_cat = (row == scol).astype(F32)
    tril = (_iota2((c, c), 0) >= _iota2((c, c), 1)).astype(BF16)

    def bd(x):
        return jnp.concatenate([x * m0, x * m1], axis=0).astype(BF16)

    def gsum(x):
        s0 = jnp.sum(jnp.where(head0, x, 0.0), axis=-1, keepdims=True)
        s1 = jnp.sum(x, axis=-1, keepdims=True) - s0
        return jnp.where(head0, s0, s1)

    def where(ci, pp):
        return ci * pairs + pp, slice(ci * c, (ci + 1) * c), slice(pp * LANES, (pp + 1) * LANES)

    def stage_gates(ci, pp):
        i, rs, ls = where(ci, pp)
        prm = prm_ref[:, ls]
        w0, a0, k_k, k_a, r_k, v0 = prm[0:1], prm[1:2], prm[2:3], prm[3:4], prm[4:5], prm[7:8]
        r = r_ref[0, rs, ls]
        k = k_ref[0, rs, ls]
        v = v_ref[0, rs, ls]
        lw = -EXP_NEG_HALF * jax.nn.sigmoid(w0 + wl_ref[0, rs, ls])
        ag = jax.nn.sigmoid(a0 + al_ref[0, rs, ls])
        if has_vres:
            v = v + (vf_ref[0, rs, ls] - v) * jax.nn.sigmoid(v0 + vl_ref[0, rs, ls])
        kkr = k * k_k
        k2 = k * (1.0 + (ag - 1.0) * k_a)
        sums = gsum(jnp.concatenate([kkr * kkr, r * k2 * r_k], axis=0))
        f64[i, q64["g"]] = _dot01(tril, lw)
        f64[i, q64["lw"]] = lw
        f64[i, q64["ag"]] = ag
        f64[i, q64["v"]] = v
        f64[i, q64["kkr"]] = kkr
        f64[i, q64["k2"]] = k2
        f64[i, q64["ssq"]] = sums[0:c]
        f64[i, q64["bv"]] = sums[c:] * v

    def stage_intra(ci, pp):
        i, rs, ls = where(ci, pp)
        g = f64[i, q64["g"]]
        lw = f64[i, q64["lw"]]
        v = f64[i, q64["v"]]
        kk = f64[i, q64["kkr"]] * lax.rsqrt(f64[i, q64["ssq"]] + L2_EPS)
        e_g = jnp.exp(g)
        e_gn = jnp.exp(-g)
        rt = r_ref[0, rs, ls] * e_g
        at = -kk * jnp.exp(g - lw)
        kt = f64[i, q64["k2"]] * e_gn
        bt = kk * f64[i, q64["ag"]] * e_gn
        x = _bdot_nt(jnp.concatenate([at, rt], axis=0), jnp.concatenate([bd(kt), bd(bt)], axis=0))
        a_ab = jnp.where(strict, x[0:c, LANES:], 0.0)
        e_last = e_g[c - 1:c, :]
        s64[i, b64["aak"]] = jnp.where(strict, x[0:c, 0:LANES], 0.0).astype(BF16)
        s64[i, b64["ark"]] = jnp.where(incl, x[c:, 0:LANES], 0.0).astype(BF16)
        s64[i, b64["arb"]] = jnp.where(incl, x[c:, LANES:], 0.0).astype(BF16)
        np_s[i, 0:c] = a_ab.astype(BF16)
        np_s[i, c:] = (eye_cat + a_ab).astype(BF16)
        s128[i, b128["nb"]] = bd(a_ab)
        f64[i, q64["p"]] = eye_cat + a_ab
        wr_s[i, c:] = rt.astype(BF16)
        s64[i, b64["vb"]] = v.astype(BF16)
        s128[i, b128["atbd"]] = bd(at)
        s128[i, b128["vbd"]] = bd(v)
        s128[i, b128["bkt"]] = jnp.concatenate([bt * e_last, kt * e_last], axis=0).T.astype(BF16)
        f128[i] = jnp.broadcast_to(e_last, (LANES, LANES)).T

    def stage_av(ci, pp):
        i, _, _ = where(ci, pp)
        s128[i, b128["avbd"]] = bd(_dot(s64[i, b64["aak"]], s128[i, b128["vbd"]]))

    def stage_level(lvl):
        def stage(ci, pp):
            i, _, _ = where(ci, pp)
            lhs = np_s[i, 0:c] if lvl == 0 else (np_s[i, c:] if lvl == 5 else np_s[i])
            out = _dot(lhs, s128[i, b128["nb"]])
            if lvl > 0:
                p = f64[i, q64["p"]] + out[-c:]
                f64[i, q64["p"]] = p
                np_s[i, c:] = p.astype(BF16)
            if lvl < 5:
                n2 = out[0:c]
                np_s[i, 0:c] = n2.astype(BF16)
                s128[i, b128["nb"]] = bd(n2)
        return stage

    def stage_wu(ci, pp):
        i, _, _ = where(ci, pp)
        rhs = jnp.concatenate([s128[i, b128["atbd"]], s128[i, b128["avbd"]]], axis=1)
        wu = _dot(f64[i, q64["p"]].astype(BF16), rhs)
        wr_s[i, 0:c] = wu[:, 0:LANES].astype(BF16)
        f64[i, q64["uc"]] = wu[:, LANES:]

    def stage_state_a(ci, pp):
        i, _, _ = where(ci, pp)
        wr = _dot(wr_s[i], hb_s[pp])
        sa = wr[0:c] + f64[i, q64["uc"]]
        s64[i, b64["sab"]] = sa.astype(BF16)
        s128[i, b128["nb"]] = bd(sa)
        f64[i, q64["y"]] = wr[c:]

    def stage_state_b(ci, pp):
        i, _, _ = where(ci, pp)
        f64[i, q64["y"]] = f64[i, q64["y"]] + _dot(
            jnp.concatenate([s64[i, b64["arb"]], s64[i, b64["ark"]]], axis=1),
            jnp.concatenate([s128[i, b128["nb"]], s128[i, b128["vbd"]]], axis=0))
        h_new = f128[i] * h_ref[pp] + bdmask * _dot(
            s128[i, b128["bkt"]], jnp.concatenate([s64[i, b64["sab"]], s64[i, b64["vb"]]], axis=0))
        h_ref[pp] = h_new
        hb_s[pp] = h_new.astype(BF16)

    def stage_center(ci, pp):
        i, _, _ = where(ci, pp)
        y = f64[i, q64["y"]]
        f64[i, q64["d"]] = y - gsum(y) * (1.0 / RW_HEAD)

    def stage_out(ci, pp):
        i, rs, ls = where(ci, pp)
        prm = prm_ref[:, ls]
        d = f64[i, q64["d"]]
        var = gsum(d * d) * (1.0 / RW_HEAD)
        yn = d * lax.rsqrt(var + GN_EPS) * prm[5:6] + prm[6:7]
        o_ref[0, rs, ls] = ((yn + f64[i, q64["bv"]]) * _silu(z_ref[0, rs, ls])).astype(o_ref.dtype)

    stages = ([stage_gates, stage_intra, stage_av] + [stage_level(lvl) for lvl in range(6)]
              + [stage_wu, stage_state_a, stage_state_b, stage_center, stage_out])
    _emit_skewed(stages, nchunk, pairs, RW_STAGE_SKEW)


def _rwkv7(rkvz, wl, al, vl, v_first, prm):
    b, t, d4 = rkvz.shape
    d = d4 // 4
    tc = _tile(t, 256)
    has_vres = vl is not None
    pg = _tile(d // LANES, 8)
    wd = pg * LANES
    c = RW_CHUNK
    nbody = (tc // c) * pg
    ngrp = d // wd

    def sec(s):
        return pl.BlockSpec((1, tc, wd), lambda i, p, j: (i, j, s * ngrp + p))

    flat = pl.BlockSpec((1, tc, wd), lambda i, p, j: (i, j, p))
    in_specs = [sec(0), sec(1), sec(2), sec(3), flat, flat]
    args = [rkvz, rkvz, rkvz, rkvz, wl, al]
    if has_vres:
        in_specs += [flat, sec(2)]
        args += [vl, v_first]
    in_specs.append(pl.BlockSpec((8, wd), lambda i, p, j: (0, p)))
    args.append(prm)
    return pl.pallas_call(
        functools.partial(_rw_kernel, has_vres=has_vres),
        grid=(b, d // wd, t // tc),
        in_specs=in_specs,
        out_specs=flat,
        out_shape=jax.ShapeDtypeStruct((b, t, d), BF16),
        scratch_shapes=[pltpu.VMEM((pg, LANES, LANES), F32),
                        pltpu.VMEM((pg, LANES, LANES), BF16),
                        pltpu.VMEM((nbody, len(_RW_B64), c, LANES), BF16),
                        pltpu.VMEM((nbody, len(_RW_B128), 2 * c, LANES), BF16),
                        pltpu.VMEM((nbody, 2 * c, LANES), BF16),
                        pltpu.VMEM((nbody, 2 * c, LANES), BF16),
                        pltpu.VMEM((nbody, len(_RW_F64), c, LANES), F32),
                        pltpu.VMEM((nbody, LANES, LANES), F32)],
        compiler_params=_params("parallel", "parallel", "arbitrary"),
        name="rwkv7",
    )(*args)


def kernel(x, norm_gains, mix_w_in, dn_conv, dn_a_log, dn_dt_bias, dn_out_norm, hg_lb_logits, hg_out_norm,
           mix_w_out, rw_mu, rw_w_rkvz, rw_w0, rw_w1, rw_w2, rw_a0, rw_a1, rw_a2, rw_v0, rw_v1, rw_v2,
           rw_k_k, rw_k_a, rw_r_k, rw_ln_w, rw_ln_b, rw_w_out, final_norm):
    b, t, d = x.shape
    m = b * t
    depth = norm_gains.shape[0]
    dn_heads = dn_a_log.shape[1]
    dn_width = dn_conv.shape[2] // 3
    hg_width = hg_lb_logits.shape[1]
    hg_heads = hg_width // LANES
    assert dn_width == dn_heads * LANES and hg_out_norm.shape[1] == LANES and rw_r_k.shape[2] == RW_HEAD
    dn_ba = 4 * dn_width
    hg_q = dn_ba + 2 * dn_heads
    assert 2 * dn_heads <= LANES

    h = x.reshape(m, d)
    v_first = None
    for layer in range(depth):
        gain = norm_gains[layer]
        if layer % 2 == 0:
            e = layer // 2
            w_in = mix_w_in[e]
            w_main = jnp.concatenate([w_in[:, :dn_ba], w_in[:, hg_q:]], axis=1).astype(BF16)
            w_ba = jnp.pad(w_in[:, dn_ba:hg_q], ((0, 0), (0, LANES - 2 * dn_heads))).astype(BF16)
            p, ba = _norm_matmul(h, gain, w_main, w_ba)
            p = p.reshape(b, t, -1)
            ba = ba.reshape(b, t, LANES)
            o_a = _deltanet(p, ba, dn_conv[e], dn_a_log[e], dn_dt_bias[e], dn_out_norm[e], dn_heads)
            o_b = _hgrn2(p, dn_ba // LANES, hg_lb_logits, hg_out_norm[e], hg_heads, e)
            h = _out_matmul([o_a.reshape(m, -1), o_b.reshape(m, -1)], mix_w_out[e].astype(BF16), h,
                            final_norm if layer == depth - 1 else None)
        else:
            o = layer // 2
            has_vres = v_first is not None
            rows = [0, 2, 3, 5, 1, 4] + ([3] if has_vres else [])
            downs = [rw_w1[o], rw_a1[o]] + ([rw_v1[o - 1]] if has_vres else [])
            a_down = jnp.stack([jnp.pad(a, ((0, 0), (0, LANES - a.shape[1]))) for a in downs]).astype(BF16)
            rkvz, mids = _mix_matmul(h.reshape(b, t, d), gain, rw_mu[o][jnp.array(rows)],
                                     rw_w_rkvz[o].astype(BF16), a_down)
            rkvz = rkvz.reshape(b, t, 4 * d)
            wl = _lora_up(mids, 0, rw_w2[o], True).reshape(b, t, d)
            al = _lora_up(mids, 1, rw_a2[o], False).reshape(b, t, d)
            vl = None
            v0 = jnp.zeros((d,), F32)
            if has_vres:
                vl = _lora_up(mids, 2, rw_v2[o - 1], False).reshape(b, t, d)
                v0 = rw_v0[o - 1]
            else:
                v_first = rkvz
            prm = jnp.stack([rw_w0[o], rw_a0[o], rw_k_k[o], rw_k_a[o], rw_r_k[o].reshape(d),
                             rw_ln_w[o], rw_ln_b[o], v0]).astype(F32)
            y = _rwkv7(rkvz, wl, al, vl, v_first, prm)
            h = _out_matmul([y.reshape(m, d)], rw_w_out[o].astype(BF16), h,
                            final_norm if layer == depth - 1 else None)
    return h.reshape(b, t, d)
```

```python
import functools
import math

import jax
import jax.numpy as jnp
from jax import lax
from jax.experimental import pallas as pl
from jax.experimental.pallas import tpu as pltpu

F32 = jnp.float32
BF16 = jnp.bfloat16

NORM_EPS = 1e-6
GN_EPS = 64e-5
L2_EPS = 1e-6
CONV_TAPS = 4

LANES = 128
SUBLANES = 8
V7X_VMEM_LIMIT_BYTES = 56 * 1024 * 1024

DN_CHUNK = 64
RW_CHUNK = 64
RW_HEAD = 64
EXP_NEG_HALF = math.exp(-0.5)
MIX_PROLOGUE_ROWS = 256
MIX_DOT_ROWS = 512
DN_STAGE_SKEW = 2
RW_STAGE_SKEW = 3
HG_STAGE_SKEW = 1


def _params(*semantics):
    return pltpu.CompilerParams(dimension_semantics=semantics, vmem_limit_bytes=V7X_VMEM_LIMIT_BYTES)


def _tile(dim, pref):
    t = min(dim, pref)
    assert dim % t == 0, (dim, pref)
    return t


def _dot(a, b):
    return jnp.dot(a, b, preferred_element_type=F32)


def _dot_nt(a, b):
    return lax.dot_general(a, b, (((1,), (1,)), ((), ())), preferred_element_type=F32)


def _bdot_nt(a, b):
    return _dot_nt(a.astype(BF16), b.astype(BF16))


def _dot01(m01, x):
    x1 = x.astype(BF16)
    x2 = (x - x1.astype(F32)).astype(BF16)
    return _dot(m01, x1) + _dot(m01, x2)


def _silu(x):
    return x * jax.nn.sigmoid(x)


def _softplus(x):
    return jnp.maximum(x, 0.0) + jnp.log1p(jnp.exp(-jnp.abs(x)))


def _iota2(shape, dim):
    return lax.broadcasted_iota(jnp.int32, shape, dim)


def _shift_rows(x, halo, j):
    full = pltpu.roll(x, j, axis=0)
    hfix = pltpu.roll(halo, j, axis=0)
    top = jnp.where(_iota2(hfix.shape, 0) < j, hfix, full[0:SUBLANES])
    return jnp.concatenate([top, full[SUBLANES:]], axis=0)


def _emit_skewed(stages, nchunk, width, skew, chunks_outer=False):
    assert skew >= 1
    n = len(stages)
    for tau in range(n + skew * (nchunk - 1)):
        order = ([(ci, w) for ci in range(nchunk) for w in range(width)] if chunks_outer
                 else [(ci, w) for w in range(width) for ci in range(nchunk)])
        for ci, w in order:
            st = tau - skew * ci
            if 0 <= st < n:
                stages[st](ci, w)


def _rms(x, g):
    return x * lax.rsqrt(jnp.mean(x * x, axis=-1, keepdims=True) + NORM_EPS) * g


def _norm_mm_kernel(x_ref, g_ref, w_ref, wn_ref, o_ref, on_ref, hn_s):
    j = pl.program_id(1)

    @pl.when(j == 0)
    def _():
        hn_s[...] = _rms(x_ref[...], g_ref[...]).astype(BF16)
        on_ref[...] = _dot(hn_s[...], wn_ref[...])

    @pl.when(j > 0)
    def _():
        o_ref[...] = _dot(hn_s[...], w_ref[...])


def _norm_matmul(h, gain, w, w_narrow):
    m, d = h.shape
    n = w.shape[1]
    tm = _tile(m, 1024)
    tn = _tile(n, 1024)
    n_main = n // tn

    def col(j):
        return jnp.maximum(j - 1, 0)

    return pl.pallas_call(
        _norm_mm_kernel,
        grid=(m // tm, n_main + 1),
        in_specs=[pl.BlockSpec((tm, d), lambda i, j: (i, 0)),
                  pl.BlockSpec((1, d), lambda i, j: (0, 0)),
                  pl.BlockSpec((d, tn), lambda i, j: (0, col(j))),
                  pl.BlockSpec((d, LANES), lambda i, j: (0, 0))],
        out_specs=[pl.BlockSpec((tm, tn), lambda i, j: (i, col(j))),
                   pl.BlockSpec((tm, LANES), lambda i, j: (i, 0))],
        out_shape=[jax.ShapeDtypeStruct((m, n), F32), jax.ShapeDtypeStruct((m, LANES), F32)],
        scratch_shapes=[pltpu.VMEM((tm, d), BF16)],
        compiler_params=_params("parallel", "arbitrary"),
        name="norm_matmul",
    )(h, gain.reshape(1, d), w, w_narrow)


def _mix_mm_kernel(x_ref, halo_ref, g_ref, mu_ref, w_ref, a_ref, o_ref, mid_ref, hn_s, xx_s, *,
                   tiles_per_group, tiles_per_seq, n_groups):
    i = pl.program_id(0)
    j = pl.program_id(1)
    tm = hn_s.shape[0]
    rc = min(tm, MIX_PROLOGUE_ROWS)
    mc = min(tm, MIX_DOT_ROWS)

    def mix(rows, mu_row):
        return hn_s[rows, :] + xx_s[rows, :] * mu_row.astype(BF16)

    @pl.when(j == 0)
    def _():
        g = g_ref[...]
        first = lax.rem(i, tiles_per_seq) == 0
        for r in range(tm // rc):
            rows = slice(r * rc, (r + 1) * rc)
            hn = _rms(x_ref[0, rows, :], g)
            if r == 0:
                hp = jnp.where(first, 0.0, _rms(halo_ref[0], g))
            else:
                hp = _rms(x_ref[0, r * rc - SUBLANES:r * rc, :], g)
            hn_s[rows, :] = hn.astype(BF16)
            xx_s[rows, :] = (_shift_rows(hn, hp, 1) - hn).astype(BF16)

        for l in range(a_ref.shape[0]):
            mu_row = mu_ref[n_groups + l:n_groups + l + 1, :]
            for r in range(tm // mc):
                rows = slice(r * mc, (r + 1) * mc)
                mid_ref[rows, l * LANES:(l + 1) * LANES] = _dot(mix(rows, mu_row), a_ref[l])

    @pl.when(j > 0)
    def _():
        mu_row = mu_ref[pl.ds(lax.div(j - 1, tiles_per_group), 1), :]
        for r in range(tm // mc):
            rows = slice(r * mc, (r + 1) * mc)
            o_ref[rows, :] = _dot(mix(rows, mu_row), w_ref[0])


def _mix_matmul(h, gain, mu, w, a_down):
    b, t, d = h.shape
    m = b * t
    g, _, n = w.shape
    nl = a_down.shape[0]
    tm = _tile(t, 1024)
    tn = _tile(n, 1024)
    tpg = n // tn
    n_main = g * tpg
    tps = t // tm
    hb = tm // SUBLANES

    def col(j):
        return jnp.maximum(j - 1, 0)

    return pl.pallas_call(
        functools.partial(_mix_mm_kernel, tiles_per_group=tpg, tiles_per_seq=tps, n_groups=g),
        grid=(m // tm, n_main + 1),
        in_specs=[pl.BlockSpec((1, tm, d), lambda i, j: (i // tps, i % tps, 0)),
                  pl.BlockSpec((1, SUBLANES, d), lambda i, j: (i // tps, jnp.maximum((i % tps) * hb - 1, 0), 0)),
                  pl.BlockSpec((1, d), lambda i, j: (0, 0)),
                  pl.BlockSpec((g + nl, d), lambda i, j: (0, 0)),
                  pl.BlockSpec((1, d, tn), lambda i, j: (col(j) // tpg, 0, col(j) % tpg)),
                  pl.BlockSpec((nl, d, LANES), lambda i, j: (0, 0, 0))],
        out_specs=[pl.BlockSpec((tm, tn), lambda i, j: (i, col(j))),
                   pl.BlockSpec((tm, nl * LANES), lambda i, j: (i, 0))],
        out_shape=[jax.ShapeDtypeStruct((m, g * n), F32), jax.ShapeDtypeStruct((m, nl * LANES), F32)],
        scratch_shapes=[pltpu.VMEM((tm, d), BF16), pltpu.VMEM((tm, d), BF16)],
        compiler_params=_params("parallel", "arbitrary"),
        name="mix_matmul",
    )(h, h, gain.reshape(1, d), mu, w, a_down)


def _lora_up_kernel(mid_ref, b_ref, o_ref, *, use_tanh):
    mid = mid_ref[...]
    if use_tanh:
        mid = jnp.tanh(mid)
    o_ref[...] = _dot(mid.astype(BF16), b_ref[...])


def _lora_up(mids, sel, b_up, use_tanh):
    m = mids.shape[0]
    rank, n = b_up.shape
    b_p = jnp.pad(b_up, ((0, LANES - rank), (0, 0))).astype(BF16)
    tm = _tile(m, 1024)
    return pl.pallas_call(
        functools.partial(_lora_up_kernel, use_tanh=use_tanh),
        grid=(m // tm,),
        in_specs=[pl.BlockSpec((tm, LANES), lambda i: (i, sel)),
                  pl.BlockSpec((LANES, n), lambda i: (0, 0))],
        out_specs=pl.BlockSpec((tm, n), lambda i: (i, 0)),
        out_shape=jax.ShapeDtypeStruct((m, n), F32),
        compiler_params=_params("parallel"),
        name="lora_up",
    )(mids, b_p)


def _out_mm_kernel(*refs, n_x, final_norm):
    x_refs, w_ref, r_ref = refs[:n_x], refs[n_x], refs[n_x + 1]
    acc = r_ref[...]
    k0 = 0
    for x_ref in x_refs:
        kx = x_ref.shape[1]
        acc = acc + _dot(x_ref[...], w_ref[k0:k0 + kx, :])
        k0 += kx
    if final_norm:
        g_ref, o_ref = refs[n_x + 2], refs[n_x + 3]
        o_ref[...] = _rms(acc, g_ref[...])
    else:
        refs[n_x + 2][...] = acc


def _out_matmul(xs, w, residual, final_gain=None):
    m = xs[0].shape[0]
    k, n = w.shape
    tm = _tile(m, 512)
    tn = _tile(n, 2048)
    in_specs = [pl.BlockSpec((tm, x.shape[1]), lambda i, j: (i, 0)) for x in xs]
    in_specs += [pl.BlockSpec((k, tn), lambda i, j: (0, j)), pl.BlockSpec((tm, tn), lambda i, j: (i, j))]
    args = [*xs, w, residual]
    if final_gain is not None:
        assert tn == n
        in_specs.append(pl.BlockSpec((1, n), lambda i, j: (0, 0)))
        args.append(final_gain.reshape(1, n))
    return pl.pallas_call(
        functools.partial(_out_mm_kernel, n_x=len(xs), final_norm=final_gain is not None),
        grid=(m // tm, n // tn),
        in_specs=in_specs,
        out_specs=pl.BlockSpec((tm, tn), lambda i, j: (i, j)),
        out_shape=jax.ShapeDtypeStruct((m, n), F32),
        compiler_params=_params("parallel", "parallel"),
        name="out_matmul",
    )(*args)


_DN_F128 = dict(q=0, k=1, vb=2, bc=3, g=4, u0=5, o=6)
_DN_F64 = dict(p=0, dmat=1)


def _dn_kernel(q_ref, k_ref, v_ref, qh_ref, kh_ref, vh_ref, z_ref, ba_ref, sel_ref, hp_ref, cw_ref, on_ref,
               o_ref, s_ref, sb_s, f128, wq_s, ub_s, nc_s, f64, rhs_s, akd_s, cd_s):
    t = pl.program_id(2)
    tc = q_ref.shape[1]
    heads = s_ref.shape[0]
    c = DN_CHUNK
    qf, qp = _DN_F128, _DN_F64

    @pl.when(t == 0)
    def _():
        s_ref[...] = jnp.zeros_like(s_ref)
        sb_s[...] = jnp.zeros_like(sb_s)

    row = _iota2((c, c), 0)
    col = _iota2((c, c), 1)
    causal = row >= col
    strict = row > col
    strict_f = strict.astype(F32)
    eye = (row == col).astype(F32)
    tril = causal.astype(BF16)
    onorm = on_ref[...]

    def where(ci, hh):
        return ci * heads + hh, slice(ci * c, (ci + 1) * c), slice(hh * LANES, (hh + 1) * LANES)

    def conv_silu(x_ref, h_ref, ci, rs, ls, w):
        x = x_ref[0, rs, ls]
        if ci == 0:
            halo = jnp.where(t > 0, h_ref[0, :, ls], 0.0)
        else:
            halo = x_ref[0, ci * c - SUBLANES:ci * c, ls]
        y = x * w[CONV_TAPS - 1:CONV_TAPS, :]
        for j in range(1, CONV_TAPS):
            y = y + _shift_rows(x, halo, j) * w[CONV_TAPS - 1 - j:CONV_TAPS - j, :]
        return _silu(y)

    def l2n(x):
        return x * lax.rsqrt(jnp.sum(x * x, axis=-1, keepdims=True) + L2_EPS)

    def stage_gates(ci, hh):
        i, rs, ls = where(ci, hh)
        cw = cw_ref[hh]
        sel = sel_ref[hh]
        hp = hp_ref[hh]
        ba = ba_ref[0, rs, :]
        qc = l2n(conv_silu(q_ref, qh_ref, ci, rs, ls, cw[0:CONV_TAPS])) * (LANES ** -0.5)
        kc = l2n(conv_silu(k_ref, kh_ref, ci, rs, ls, cw[CONV_TAPS:2 * CONV_TAPS]))
        vc = conv_silu(v_ref, vh_ref, ci, rs, ls, cw[2 * CONV_TAPS:3 * CONV_TAPS])
        beta = jax.nn.sigmoid(jnp.sum(ba * sel[0:1], axis=-1, keepdims=True))
        alpha_pre = jnp.sum(ba * sel[1:2], axis=-1, keepdims=True)
        bc = jnp.broadcast_to(beta, (c, LANES))
        lac = -jnp.exp(hp[0:1]) * _softplus(jnp.broadcast_to(alpha_pre, (c, LANES)) + hp[1:2])
        f128[i, qf["q"]] = qc
        f128[i, qf["k"]] = kc
        f128[i, qf["vb"]] = vc * bc
        f128[i, qf["bc"]] = bc
        f128[i, qf["g"]] = _dot01(tril, lac)
        f64[i, qp["dmat"]] = _dot01(tril, lac[:, 0:c] * strict_f)

    def stage_intra(ci, hh):
        i, rs, ls = where(ci, hh)
        qc = f128[i, qf["q"]]
        kc = f128[i, qf["k"]]
        bc = f128[i, qf["bc"]]
        g = f128[i, qf["g"]]
        decay = jnp.where(causal, jnp.exp(jnp.where(causal, f64[i, qp["dmat"]], 0.0)), 0.0)
        kb = kc.astype(BF16)
        qkk = _dot_nt(jnp.concatenate([qc.astype(BF16), kb], axis=0), kb)
        lower = jnp.where(strict, bc[:, 0:c] * qkk[c:] * decay, 0.0)
        gamma = jnp.exp(g)
        glast = g[c - 1:c, :]
        akd_s[i, 0:c] = jnp.where(causal, qkk[0:c] * decay, 0.0).astype(BF16)
        nc_s[i] = (-lower).astype(BF16)
        f64[i, qp["p"]] = eye - lower
        rhs_s[i] = jnp.concatenate([f128[i, qf["vb"]], kc * (bc * gamma)], axis=1).astype(BF16)
        wq_s[i, c:] = (qc * gamma).astype(BF16)
        akd_s[i, c:] = (kc * jnp.exp(glast - g)).T.astype(BF16)
        cd_s[i] = jnp.broadcast_to(jnp.exp(glast), (SUBLANES, LANES))

    def stage_square(ci, hh):
        i, _, _ = where(ci, hh)
        nc = nc_s[i]
        nc_s[i] = _dot(nc, nc).astype(BF16)

    def stage_accumulate(ci, hh):
        i, _, _ = where(ci, hh)
        p = f64[i, qp["p"]]
        f64[i, qp["p"]] = p + _dot(p.astype(BF16), nc_s[i])

    def stage_solve(ci, hh):
        i, _, _ = where(ci, hh)
        sol = _dot(f64[i, qp["p"]].astype(BF16), rhs_s[i])
        f128[i, qf["u0"]] = sol[:, 0:LANES]
        wq_s[i, 0:c] = sol[:, LANES:].astype(BF16)

    def stage_state_a(ci, hh):
        i, _, _ = where(ci, hh)
        wq = _dot(wq_s[i], sb_s[hh])
        ub_s[i] = (f128[i, qf["u0"]] - wq[0:c]).astype(BF16)
        f128[i, qf["o"]] = wq[c:]

    def stage_state_b(ci, hh):
        i, _, _ = where(ci, hh)
        ou = _dot(akd_s[i], ub_s[i])
        f128[i, qf["o"]] = f128[i, qf["o"]] + ou[0:c]
        s_new = cd_s[i][0:1] * s_ref[hh] + ou[c:]
        s_ref[hh] = s_new
        sb_s[hh] = s_new.astype(BF16)

    def stage_out(ci, hh):
        i, rs, ls = where(ci, hh)
        o = f128[i, qf["o"]]
        on = o * lax.rsqrt(jnp.mean(o * o, axis=-1, keepdims=True) + NORM_EPS) * onorm
        o_ref[0, rs, ls] = (on * _silu(z_ref[0, rs, ls])).astype(o_ref.dtype)

    stages = ([stage_gates, stage_intra] + [stage_square, stage_accumulate] * 5
              + [stage_solve, stage_state_a, stage_state_b, stage_out])
    _emit_skewed(stages, tc // c, heads, DN_STAGE_SKEW)


def _deltanet(p, ba, conv_w, a_log, dt_bias, out_norm, nheads):
    b, t, _ = p.shape
    tc = _tile(t, 256)
    c = DN_CHUNK
    assert tc % c == 0
    hb = tc // SUBLANES
    nh = nheads
    hg = _tile(nh, 8)
    ng = nh // hg
    wd = hg * LANES
    nbody = (tc // c) * hg
    cw = conv_w.reshape(CONV_TAPS, 3, nh, LANES).transpose(2, 1, 0, 3).reshape(nh, 3 * CONV_TAPS, LANES)
    lane = jnp.arange(LANES)[None, None, :]
    head = jnp.arange(nh)[:, None, None]
    sel = jnp.concatenate([(lane == head), (lane == head + nh)], axis=1).astype(F32)
    hp = jnp.broadcast_to(jnp.stack([a_log, dt_bias], axis=1)[:, :, None], (nh, 2, LANES)).astype(F32)

    def main(sec):
        return pl.BlockSpec((1, tc, wd), lambda i, h, j: (i, j, sec * ng + h))

    def halo(sec):
        return pl.BlockSpec((1, SUBLANES, wd), lambda i, h, j: (i, jnp.maximum(j * hb - 1, 0), sec * ng + h))

    per_head = lambda rows: pl.BlockSpec((hg, rows, LANES), lambda i, h, j: (h, 0, 0))
    return pl.pallas_call(
        _dn_kernel,
        grid=(b, ng, t // tc),
        in_specs=[main(0), main(1), main(2), halo(0), halo(1), halo(2), main(3),
                  pl.BlockSpec((1, tc, LANES), lambda i, h, j: (i, j, 0)),
                  per_head(2), per_head(2), per_head(3 * CONV_TAPS),
                  pl.BlockSpec((1, LANES), lambda i, h, j: (0, 0))],
        out_specs=pl.BlockSpec((1, tc, wd), lambda i, h, j: (i, j, h)),
        out_shape=jax.ShapeDtypeStruct((b, t, nh * LANES), BF16),
        scratch_shapes=[pltpu.VMEM((hg, LANES, LANES), F32),
                        pltpu.VMEM((hg, LANES, LANES), BF16),
                        pltpu.VMEM((nbody, len(_DN_F128), c, LANES), F32),
                        pltpu.VMEM((nbody, 2 * c, LANES), BF16),
                        pltpu.VMEM((nbody, c, LANES), BF16),
                        pltpu.VMEM((nbody, c, c), BF16),
                        pltpu.VMEM((nbody, len(_DN_F64), c, c), F32),
                        pltpu.VMEM((nbody, c, 2 * LANES), BF16),
                        pltpu.VMEM((nbody, c + LANES, c), BF16),
                        pltpu.VMEM((nbody, SUBLANES, LANES), F32)],
        compiler_params=_params("parallel", "parallel", "arbitrary"),
        name="deltanet",
    )(p, p, p, p, p, p, p, ba, sel, hp, cw, out_norm.reshape(1, LANES))


HG_LEVELS = 6
HG_BLOCK = 64


def _hg_tables():
    import numpy as np
    c = HG_BLOCK
    t = np.arange(c)[:, None]
    i = np.arange(c)[None, :]
    rows = [i <= t, i > t]
    q_rows, k_rows = [], []
    level = np.full((c, c), -1, np.int32)
    level[np.arange(c), np.arange(c)] = HG_LEVELS
    for li in range(HG_LEVELS):
        m = c >> (li + 1)
        pos = t % (2 * m)
        upper = pos >= m
        ref = t - pos + m - 1
        q_rows.append(upper & (i > ref) & (i <= t))
        k_rows.append((~upper) & (i > t) & (i <= ref))
        same = (t // (2 * m)) == (i // (2 * m))
        level[same & upper & ((i % (2 * m)) < m)] = li
    table = np.concatenate(rows + q_rows + k_rows, axis=0).astype(np.float32)
    return table, level


def _hg_kernel(q_ref, f_ref, i_ref, z_ref, lb_ref, on_ref, tab_ref, lvl_ref, o_ref,
               s_ref, sb_s, qt_s, kt_s, qb_s, vb_s, klt_s, att_s, ecol_s, inc_s, o_s, *, layer):
    t = pl.program_id(2)
    tc = q_ref.shape[1]
    heads = s_ref.shape[0]
    c = HG_BLOCK
    nl = HG_LEVELS

    @pl.when(t == 0)
    def _():
        s_ref[...] = jnp.zeros_like(s_ref)
        sb_s[...] = jnp.zeros_like(sb_s)

    table = tab_ref[...]
    level = lvl_ref[...]
    onorm = on_ref[...]
    def where(ci, hh):
        return ci * heads + hh, slice(ci * c, (ci + 1) * c), slice(hh * LANES, (hh + 1) * LANES)

    def stage_gates(ci, hh):
        i, rs, ls = where(ci, hh)
        logits = lb_ref[:, ls]
        e = jnp.exp(logits - jnp.max(logits, axis=0, keepdims=True))
        share = e / jnp.sum(e, axis=0, keepdims=True)
        lbound = jnp.sum(share[0:layer + 1], axis=0, keepdims=True) - share[0:1]
        q = _silu(q_ref[0, rs, ls]) * (LANES ** -0.5)
        forget = lbound + (1.0 - lbound) * jax.nn.sigmoid(f_ref[0, rs, ls])
        k = 1.0 - forget
        d = _dot(table, jnp.log(forget).astype(BF16))
        b = d[0:c]
        qb_s[i] = (q * jnp.exp(b)).astype(BF16)
        klt_s[i] = (k * jnp.exp(d[c:2 * c])).T.astype(BF16)
        ecol_s[i] = jnp.broadcast_to(jnp.exp(b[c - 1:c]), (LANES, LANES)).T
        for li in range(nl):
            qt_s[i, li] = (q * jnp.exp(d[(2 + li) * c:(3 + li) * c])).astype(BF16)
            kt_s[i, li] = (k * jnp.exp(d[(2 + nl + li) * c:(3 + nl + li) * c])).astype(BF16)
        qt_s[i, nl] = q.astype(BF16)
        kt_s[i, nl] = k.astype(BF16)
        vb_s[i] = i_ref[0, rs, ls].astype(BF16)

    def stage_intra(ci, hh):
        i, _, _ = where(ci, hh)
        att = jnp.zeros((c, c), F32)
        for li in range(nl + 1):
            att = att + jnp.where(level == li, _dot_nt(qt_s[i, li], kt_s[i, li]), 0.0)
        att_s[i] = att.astype(BF16)
        inc_s[i] = _dot(klt_s[i], vb_s[i])

    def stage_att_v(ci, hh):
        i, _, _ = where(ci, hh)
        o_s[i] = _dot(att_s[i], vb_s[i])

    def stage_state(ci, hh):
        i, _, _ = where(ci, hh)
        o_s[i] = o_s[i] + _dot(qb_s[i], sb_s[hh])
        s_new = ecol_s[i] * s_ref[hh] + inc_s[i]
        s_ref[hh] = s_new
        sb_s[hh] = s_new.astype(BF16)

    def stage_out(ci, hh):
        i, rs, ls = where(ci, hh)
        o = o_s[i]
        on = o * lax.rsqrt(jnp.mean(o * o, axis=-1, keepdims=True) + NORM_EPS) * onorm
        o_ref[0, rs, ls] = (on * _silu(z_ref[0, rs, ls])).astype(o_ref.dtype)

    _emit_skewed([stage_gates, stage_intra, stage_att_v, stage_state, stage_out], tc // c, heads, HG_STAGE_SKEW)


def _hgrn2(p, col0, lb_logits, out_norm, nheads, layer):
    b, t, _ = p.shape
    tc = _tile(t, 512)
    c = HG_BLOCK
    assert tc % c == 0
    nh = nheads
    hg = _tile(nh, 8)
    ng = nh // hg
    wd = hg * LANES
    assert col0 % hg == 0
    cg = col0 // hg
    nbody = (tc // c) * hg
    table, level = _hg_tables()
    nrow = table.shape[0]

    def main(sec):
        return pl.BlockSpec((1, tc, wd), lambda i, h, j: (i, j, cg + sec * ng + h))

    nl = lb_logits.shape[0]
    const = lambda shape: pl.BlockSpec(shape, lambda i, h, j: (0, 0))
    return pl.pallas_call(
        functools.partial(_hg_kernel, layer=layer),
        grid=(b, ng, t // tc),
        in_specs=[main(0), main(1), main(2), main(3),
                  pl.BlockSpec((nl, wd), lambda i, h, j: (0, h)),
                  const((1, LANES)), const((nrow, c)), const((c, c))],
        out_specs=pl.BlockSpec((1, tc, wd), lambda i, h, j: (i, j, h)),
        out_shape=jax.ShapeDtypeStruct((b, t, nh * LANES), BF16),
        scratch_shapes=[pltpu.VMEM((hg, LANES, LANES), F32),
                        pltpu.VMEM((hg, LANES, LANES), BF16),
                        pltpu.VMEM((nbody, HG_LEVELS + 1, c, LANES), BF16),
                        pltpu.VMEM((nbody, HG_LEVELS + 1, c, LANES), BF16),
                        pltpu.VMEM((nbody, c, LANES), BF16),
                        pltpu.VMEM((nbody, c, LANES), BF16),
                        pltpu.VMEM((nbody, LANES, c), BF16),
                        pltpu.VMEM((nbody, c, c), BF16),
                        pltpu.VMEM((nbody, LANES, LANES), F32),
                        pltpu.VMEM((nbody, LANES, LANES), F32),
                        pltpu.VMEM((nbody, c, LANES), F32)],
        compiler_params=_params("parallel", "parallel", "arbitrary"),
        name="hgrn2",
    )(p, p, p, p, lb_logits, out_norm.reshape(1, LANES), jnp.asarray(table, BF16), jnp.asarray(level))


_RW_B64 = dict(aak=0, vb=1, arb=2, ark=3, sab=4)
_RW_B128 = dict(nb=0, vbd=1, bkt=2, atbd=3, avbd=4)
_RW_F64 = dict(p=0, uc=1, g=2, bv=3, lw=4, kkr=5, k2=6, ag=7, v=8, ssq=9, y=10, d=11)


def _rw_kernel(*refs, has_vres):
    if has_vres:
        (r_ref, k_ref, v_ref, z_ref, wl_ref, al_ref, vl_ref, vf_ref, prm_ref,
         o_ref, h_ref, hb_s, s64, s128, wr_s, np_s, f64, f128) = refs
    else:
        (r_ref, k_ref, v_ref, z_ref, wl_ref, al_ref, prm_ref,
         o_ref, h_ref, hb_s, s64, s128, wr_s, np_s, f64, f128) = refs
    t = pl.program_id(2)
    tc = o_ref.shape[1]
    pairs = h_ref.shape[0]
    c = RW_CHUNK
    nchunk = tc // c
    b64, b128, q64 = _RW_B64, _RW_B128, _RW_F64

    @pl.when(t == 0)
    def _():
        h_ref[...] = jnp.zeros_like(h_ref)
        hb_s[...] = jnp.zeros_like(hb_s)

    lane1 = _iota2((1, LANES), 1)
    head0 = lane1 < RW_HEAD
    m0 = head0.astype(F32)
    m1 = 1.0 - m0
    same_head = (_iota2((LANES, LANES), 0) < RW_HEAD) == (_iota2((LANES, LANES), 1) < RW_HEAD)
    bdmask = same_head.astype(F32)
    row = _iota2((c, LANES), 0)
    scol = jnp.bitwise_and(_iota2((c, LANES), 1), RW_HEAD - 1)
    incl = row >= scol
    strict = row > scol
    eye_cat = (row == scol).astype(F32)
    tril = (_iota2((c, c), 0) >= _iota2((c, c), 1)).astype(BF16)

    def bd(x):
        return jnp.concatenate([x * m0, x * m1], axis=0).astype(BF16)

    def gsum(x):
        s0 = jnp.sum(jnp.where(head0, x, 0.0), axis=-1, keepdims=True)
        s1 = jnp.sum(x, axis=-1, keepdims=True) - s0
        return jnp.where(head0, s0, s1)

    def where(ci, pp):
        return ci * pairs + pp, slice(ci * c, (ci + 1) * c), slice(pp * LANES, (pp + 1) * LANES)

    def stage_gates(ci, pp):
        i, rs, ls = where(ci, pp)
        prm = prm_ref[:, ls]
        w0, a0, k_k, k_a, r_k, v0 = prm[0:1], prm[1:2], prm[2:3], prm[3:4], prm[4:5], prm[7:8]
        r = r_ref[0, rs, ls]
        k = k_ref[0, rs, ls]
        v = v_ref[0, rs, ls]
        lw = -EXP_NEG_HALF * jax.nn.sigmoid(w0 + wl_ref[0, rs, ls])
        ag = jax.nn.sigmoid(a0 + al_ref[0, rs, ls])
        if has_vres:
            v = v + (vf_ref[0, rs, ls] - v) * jax.nn.sigmoid(v0 + vl_ref[0, rs, ls])
        kkr = k * k_k
        k2 = k * (1.0 + (ag - 1.0) * k_a)
        sums = gsum(jnp.concatenate([kkr * kkr, r * k2 * r_k], axis=0))
        f64[i, q64["g"]] = _dot01(tril, lw)
        f64[i, q64["lw"]] = lw
        f64[i, q64["ag"]] = ag
        f64[i, q64["v"]] = v
        f64[i, q64["kkr"]] = kkr
        f64[i, q64["k2"]] = k2
        f64[i, q64["ssq"]] = sums[0:c]
        f64[i, q64["bv"]] = sums[c:] * v

    def stage_intra(ci, pp):
        i, rs, ls = where(ci, pp)
        g = f64[i, q64["g"]]
        lw = f64[i, q64["lw"]]
        v = f64[i, q64["v"]]
        kk = f64[i, q64["kkr"]] * lax.rsqrt(f64[i, q64["ssq"]] + L2_EPS)
        e_g = jnp.exp(g)
        e_gn = jnp.exp(-g)
        rt = r_ref[0, rs, ls] * e_g
        at = -kk * jnp.exp(g - lw)
        kt = f64[i, q64["k2"]] * e_gn
        bt = kk * f64[i, q64["ag"]] * e_gn
        x = _bdot_nt(jnp.concatenate([at, rt], axis=0), jnp.concatenate([bd(kt), bd(bt)], axis=0))
        a_ab = jnp.where(strict, x[0:c, LANES:], 0.0)
        e_last = e_g[c - 1:c, :]
        s64[i, b64["aak"]] = jnp.where(strict, x[0:c, 0:LANES], 0.0).astype(BF16)
        s64[i, b64["ark"]] = jnp.where(incl, x[c:, 0:LANES], 0.0).astype(BF16)
        s64[i, b64["arb"]] = jnp.where(incl, x[c:, LANES:], 0.0).astype(BF16)
        np_s[i, 0:c] = a_ab.astype(BF16)
        np_s[i, c:] = (eye_cat + a_ab).astype(BF16)
        s128[i, b128["nb"]] = bd(a_ab)
        f64[i, q64["p"]] = eye_cat + a_ab
        wr_s[i, c:] = rt.astype(BF16)
        s64[i, b64["vb"]] = v.astype(BF16)
        s128[i, b128["atbd"]] = bd(at)
        s128[i, b128["vbd"]] = bd(v)
        s128[i, b128["bkt"]] = jnp.concatenate([bt * e_last, kt * e_last], axis=0).T.astype(BF16)
        f128[i] = jnp.broadcast_to(e_last, (LANES, LANES)).T

    def stage_av(ci, pp):
        i, _, _ = where(ci, pp)
        s128[i, b128["avbd"]] = bd(_dot(s64[i, b64["aak"]], s128[i, b128["vbd"]]))

    def stage_level(lvl):
        def stage(ci, pp):
            i, _, _ = where(ci, pp)
            lhs = np_s[i, 0:c] if lvl == 0 else (np_s[i, c:] if lvl == 5 else np_s[i])
            out = _dot(lhs, s128[i, b128["nb"]])
            if lvl > 0:
                p = f64[i, q64["p"]] + out[-c:]
                f64[i, q64["p"]] = p
                np_s[i, c:] = p.astype(BF16)
            if lvl < 5:
                n2 = out[0:c]
                np_s[i, 0:c] = n2.astype(BF16)
                s128[i, b128["nb"]] = bd(n2)
        return stage

    def stage_wu(ci, pp):
        i, _, _ = where(ci, pp)
        rhs = jnp.concatenate([s128[i, b128["atbd"]], s128[i, b128["avbd"]]], axis=1)
        wu = _dot(f64[i, q64["p"]].astype(BF16), rhs)
        wr_s[i, 0:c] = wu[:, 0:LANES].astype(BF16)
        f64[i, q64["uc"]] = wu[:, LANES:]

    def stage_state_a(ci, pp):
        i, _, _ = where(ci, pp)
        wr = _dot(wr_s[i], hb_s[pp])
        sa = wr[0:c] + f64[i, q64["uc"]]
        s64[i, b64["sab"]] = sa.astype(BF16)
        s128[i, b128["nb"]] = bd(sa)
        f64[i, q64["y"]] = wr[c:]

    def stage_state_b(ci, pp):
        i, _, _ = where(ci, pp)
        f64[i, q64["y"]] = f64[i, q64["y"]] + _dot(
            jnp.concatenate([s64[i, b64["arb"]], s64[i, b64["ark"]]], axis=1),
            jnp.concatenate([s128[i, b128["nb"]], s128[i, b128["vbd"]]], axis=0))
        h_new = f128[i] * h_ref[pp] + bdmask * _dot(
            s128[i, b128["bkt"]], jnp.concatenate([s64[i, b64["sab"]], s64[i, b64["vb"]]], axis=0))
        h_ref[pp] = h_new
        hb_s[pp] = h_new.astype(BF16)

    def stage_center(ci, pp):
        i, _, _ = where(ci, pp)
        y = f64[i, q64["y"]]
        f64[i, q64["d"]] = y - gsum(y) * (1.0 / RW_HEAD)

    def stage_out(ci, pp):
        i, rs, ls = where(ci, pp)
        prm = prm_ref[:, ls]
        d = f64[i, q64["d"]]
        var = gsum(d * d) * (1.0 / RW_HEAD)
        yn = d * lax.rsqrt(var + GN_EPS) * prm[5:6] + prm[6:7]
        o_ref[0, rs, ls] = ((yn + f64[i, q64["bv"]]) * _silu(z_ref[0, rs, ls])).astype(o_ref.dtype)

    stages = ([stage_gates, stage_intra, stage_av] + [stage_level(lvl) for lvl in range(6)]
              + [stage_wu, stage_state_a, stage_state_b, stage_center, stage_out])
    _emit_skewed(stages, nchunk, pairs, RW_STAGE_SKEW, chunks_outer=True)


def _rwkv7(rkvz, wl, al, vl, v_first, prm):
    b, t, d4 = rkvz.shape
    d = d4 // 4
    tc = _tile(t, 256)
    has_vres = vl is not None
    pg = _tile(d // LANES, 8)
    wd = pg * LANES
    c = RW_CHUNK
    nbody = (tc // c) * pg
    ngrp = d // wd

    def sec(s):
        return pl.BlockSpec((1, tc, wd), lambda i, p, j: (i, j, s * ngrp + p))

    flat = pl.BlockSpec((1, tc, wd), lambda i, p, j: (i, j, p))
    in_specs = [sec(0), sec(1), sec(2), sec(3), flat, flat]
    args = [rkvz, rkvz, rkvz, rkvz, wl, al]
    if has_vres:
        in_specs += [flat, sec(2)]
        args += [vl, v_first]
    in_specs.append(pl.BlockSpec((8, wd), lambda i, p, j: (0, p)))
    args.append(prm)
    return pl.pallas_call(
        functools.partial(_rw_kernel, has_vres=has_vres),
        grid=(b, d // wd, t // tc),
        in_specs=in_specs,
        out_specs=flat,
        out_shape=jax.ShapeDtypeStruct((b, t, d), BF16),
        scratch_shapes=[pltpu.VMEM((pg, LANES, LANES), F32),
                        pltpu.VMEM((pg, LANES, LANES), BF16),
                        pltpu.VMEM((nbody, len(_RW_B64), c, LANES), BF16),
                        pltpu.VMEM((nbody, len(_RW_B128), 2 * c, LANES), BF16),
                        pltpu.VMEM((nbody, 2 * c, LANES), BF16),
                        pltpu.VMEM((nbody, 2 * c, LANES), BF16),
                        pltpu.VMEM((nbody, len(_RW_F64), c, LANES), F32),
                        pltpu.VMEM((nbody, LANES, LANES), F32)],
        compiler_params=_params("parallel", "parallel", "arbitrary"),
        name="rwkv7",
    )(*args)


def kernel(x, norm_gains, mix_w_in, dn_conv, dn_a_log, dn_dt_bias, dn_out_norm, hg_lb_logits, hg_out_norm,
           mix_w_out, rw_mu, rw_w_rkvz, rw_w0, rw_w1, rw_w2, rw_a0, rw_a1, rw_a2, rw_v0, rw_v1, rw_v2,
           rw_k_k, rw_k_a, rw_r_k, rw_ln_w, rw_ln_b, rw_w_out, final_norm):
    b, t, d = x.shape
    m = b * t
    depth = norm_gains.shape[0]
    dn_heads = dn_a_log.shape[1]
    dn_width = dn_conv.shape[2] // 3
    hg_width = hg_lb_logits.shape[1]
    hg_heads = hg_width // LANES
    assert dn_width == dn_heads * LANES and hg_out_norm.shape[1] == LANES and rw_r_k.shape[2] == RW_HEAD
    dn_ba = 4 * dn_width
    hg_q = dn_ba + 2 * dn_heads
    assert 2 * dn_heads <= LANES

    h = x.reshape(m, d)
    v_first = None
    for layer in range(depth):
        gain = norm_gains[layer]
        if layer % 2 == 0:
            e = layer // 2
            w_in = mix_w_in[e]
            w_main = jnp.concatenate([w_in[:, :dn_ba], w_in[:, hg_q:]], axis=1).astype(BF16)
            w_ba = jnp.pad(w_in[:, dn_ba:hg_q], ((0, 0), (0, LANES - 2 * dn_heads))).astype(BF16)
            p, ba = _norm_matmul(h, gain, w_main, w_ba)
            p = p.reshape(b, t, -1)
            ba = ba.reshape(b, t, LANES)
            o_a = _deltanet(p, ba, dn_conv[e], dn_a_log[e], dn_dt_bias[e], dn_out_norm[e], dn_heads)
            o_b = _hgrn2(p, dn_ba // LANES, hg_lb_logits, hg_out_norm[e], hg_heads, e)
            h = _out_matmul([o_a.reshape(m, -1), o_b.reshape(m, -1)], mix_w_out[e].astype(BF16), h,
                            final_norm if layer == depth - 1 else None)
        else:
            o = layer // 2
            has_vres = v_first is not None
            rows = [0, 2, 3, 5, 1, 4] + ([3] if has_vres else [])
            downs = [rw_w1[o], rw_a1[o]] + ([rw_v1[o - 1]] if has_vres else [])
            a_down = jnp.stack([jnp.pad(a, ((0, 0), (0, LANES - a.shape[1]))) for a in downs]).astype(BF16)
            rkvz, mids = _mix_matmul(h.reshape(b, t, d), gain, rw_mu[o][jnp.array(rows)],
                                     rw_w_rkvz[o].astype(BF16), a_down)
            rkvz = rkvz.reshape(b, t, 4 * d)
            wl = _lora_up(mids, 0, rw_w2[o], True).reshape(b, t, d)
            al = _lora_up(mids, 1, rw_a2[o], False).reshape(b, t, d)
            vl = None
            v0 = jnp.zeros((d,), F32)
            if has_vres:
                vl = _lora_up(mids, 2, rw_v2[o - 1], False).reshape(b, t, d)
                v0 = rw_v0[o - 1]
            else:
                v_first = rkvz
            prm = jnp.stack([rw_w0[o], rw_a0[o], rw_k_k[o], rw_k_a[o], rw_r_k[o].reshape(d),
                             rw_ln_w[o], rw_ln_b[o], v0]).astype(F32)
            y = _rwkv7(rkvz, wl, al, vl, v_first, prm)
            h = _out_matmul([y.reshape(m, d)], rw_w_out[o].astype(BF16), h,
                            final_norm if layer == depth - 1 else None)
    return h.reshape(b, t, d)
```
